```python
import math
import jax, jax.numpy as jnp
from jax import lax
import numpy as np

D_MODEL = 1024
BATCH = 8
SEQ = 2048
DEPTH = 1
DEC_BATCH = 16
DEC_SEQ = 32
PAST_LEN = 1024

CHUNK = 64
Q_BLOCK = 128
SSD_HEADS = 16
SSD_HEAD_DIM = 64
SSD_D_INNER = SSD_HEADS * SSD_HEAD_DIM
SSD_GROUPS = 2
SSD_HEADS_PER_GROUP = SSD_HEADS // SSD_GROUPS
SSD_STATE = 128
CONV_W = 4
SSD_CONV_DIM = SSD_D_INNER + 2 * SSD_GROUPS * SSD_STATE
MLA_HEADS = 8
Q_LORA = 512
KV_LORA = 512
QK_NOPE = 128
QK_ROPE = 64
V_HEAD = 128
MLA_WIDTH = MLA_HEADS * V_HEAD
ROPE_THETA = 10000.0
D_MIX = SSD_D_INNER + MLA_WIDTH
D_IN_PROJ = SSD_D_INNER + SSD_CONV_DIM + SSD_HEADS + Q_LORA + KV_LORA + QK_ROPE
D_FF = 2816
ALPHA = (2 * DEPTH) ** 0.25
BETA = (8 * DEPTH) ** -0.25
EPS = 1e-5

kernel_name = "hybrid_ssd_mla_streaming_step"


def layer_norm(x, g, b):
    xf = x.astype(jnp.float32)
    mu = jnp.mean(xf, -1, keepdims=True)
    var = jnp.mean(jnp.square(xf - mu), -1, keepdims=True)
    return ((xf - mu) * lax.rsqrt(var + EPS) * g + b).astype(x.dtype)


def rms_norm(x, g):
    xf = x.astype(jnp.float32)
    return (xf * lax.rsqrt(jnp.mean(jnp.square(xf), -1, keepdims=True) + EPS) * g).astype(x.dtype)


def swiglu(x, w_gate, w_up, w_down):
    return (jax.nn.silu(x @ w_gate) * (x @ w_up)) @ w_down


def rope_cos_sin(pos):
    inv = ROPE_THETA ** (-jnp.arange(0, QK_ROPE, 2, dtype=jnp.float32) / QK_ROPE)
    ang = pos.astype(jnp.float32)[:, None] * inv[None, :]
    return jnp.cos(ang), jnp.sin(ang)


def rotate(x, cos, sin):
    x1, x2 = jnp.split(x.astype(jnp.float32), 2, axis=-1)
    return jnp.concatenate([x1 * cos - x2 * sin, x1 * sin + x2 * cos], -1).astype(x.dtype)


def causal_conv(xpad, w, b):
    L = xpad.shape[1] - (CONV_W - 1)
    out = b
    for k in range(CONV_W):
        out = out + xpad[:, k:k + L] * w[k]
    return out


def ssd_scan(xdt, dA, Bm, Cm, h0, chunk):
    b, L, H, P = xdt.shape
    G, Hg, N = SSD_GROUPS, SSD_HEADS_PER_GROUP, SSD_STATE
    nc = L // chunk
    dt_ = xdt.dtype
    x = xdt.reshape(b, nc, chunk, G, Hg, P)
    Bc = Bm.reshape(b, nc, chunk, G, N)
    Cc = Cm.reshape(b, nc, chunk, G, N)
    a_cs = jnp.cumsum(dA.reshape(b, nc, chunk, H), axis=2)
    seg = a_cs[:, :, :, None, :] - a_cs[:, :, None, :, :]
    tri = jnp.tril(jnp.ones((chunk, chunk), bool))[None, None, :, :, None]
    Lm = jnp.exp(jnp.where(tri, seg, -jnp.inf)).astype(dt_).reshape(b, nc, chunk, chunk, G, Hg)
    cb = jnp.einsum('bcign,bcjgn->bcijg', Cc, Bc)
    y_diag = jnp.einsum('bcijg,bcijgh,bcjghp->bcighp', cb, Lm, x)
    decay_states = jnp.exp(a_cs[:, :, -1:, :] - a_cs).astype(dt_).reshape(b, nc, chunk, G, Hg)
    states = jnp.einsum('bcjgn,bcjgh,bcjghp->bcghpn', Bc, decay_states, x)
    chunk_decay = jnp.exp(a_cs[:, :, -1, :]).astype(dt_).reshape(b, nc, G, Hg)

    def step(h, inp):
        dec, st = inp
        return dec[..., None, None] * h + st, h

    h_init = h0.astype(states.dtype).reshape(b, G, Hg, P, N)
    h_final, h_prev = lax.scan(step, h_init, (jnp.moveaxis(chunk_decay, 1, 0), jnp.moveaxis(states, 1, 0)))
    h_prev = jnp.moveaxis(h_prev, 0, 1)
    in_decay = jnp.exp(a_cs).astype(dt_).reshape(b, nc, chunk, G, Hg)
    y_off = jnp.einsum('bcign,bcigh,bcghpn->bcighp', Cc, in_decay, h_prev)
    y = (y_diag + y_off).reshape(b, L, H, P)
    return y, h_final.reshape(b, H, P, N)


def mla_attend(q_nope, q_rope, k_nope, k_rope, v, q_pos, k_pos):
    s = jnp.einsum('bqhd,bkhd->bhqk', q_nope, k_nope) + jnp.einsum('bqhd,bkd->bhqk', q_rope, k_rope)
    s = s.astype(jnp.float32) * ((QK_NOPE + QK_ROPE) ** -0.5)
    mask = (k_pos // CHUNK)[None, :] <= (q_pos // CHUNK)[:, None]
    s = jnp.where(mask, s, -1e30)
    p = jax.nn.softmax(s, axis=-1).astype(v.dtype)
    return jnp.einsum('bhqk,bkhd->bqhd', p, v)


def token_mix(h, conv_prev, h0, lat_prev, kr_prev, pos0,
              w_in, conv_w, conv_b, dt_bias, a_log, d_skip, ssd_norm_g,
              q_norm_g, w_uq, kv_norm_g, w_ukv, w_out):
    b, L, _ = h.shape
    proj = h @ w_in
    idx = [SSD_D_INNER, SSD_D_INNER + SSD_CONV_DIM, SSD_D_INNER + SSD_CONV_DIM + SSD_HEADS,
           SSD_D_INNER + SSD_CONV_DIM + SSD_HEADS + Q_LORA,
           SSD_D_INNER + SSD_CONV_DIM + SSD_HEADS + Q_LORA + KV_LORA]
    z, xbc, dt_raw, c_q, c_kv, k_r = jnp.split(proj, idx, axis=-1)

    xpad = jnp.concatenate([conv_prev.astype(xbc.dtype), xbc], axis=1)
    new_conv = xpad[:, -(CONV_W - 1):]
    xbc_c = jax.nn.silu(causal_conv(xpad, conv_w, conv_b))
    xs, Bm, Cm = jnp.split(xbc_c, [SSD_D_INNER, SSD_D_INNER + SSD_GROUPS * SSD_STATE], axis=-1)
    xs = xs.reshape(b, L, SSD_HEADS, SSD_HEAD_DIM)
    Bm = Bm.reshape(b, L, SSD_GROUPS, SSD_STATE)
    Cm = Cm.reshape(b, L, SSD_GROUPS, SSD_STATE)
    dt = jax.nn.softplus(dt_raw.astype(jnp.float32) + dt_bias)
    A = -jnp.exp(a_log.astype(jnp.float32))
    chunk = min(CHUNK, L)
    y, h_new = ssd_scan(xs * dt[..., None].astype(xs.dtype), dt * A, Bm, Cm, h0, chunk)
    y = y + d_skip[:, None] * xs
    y = y.reshape(b, L, SSD_D_INNER) * jax.nn.silu(z)
    y = rms_norm(y.reshape(b, L, SSD_GROUPS, SSD_D_INNER // SSD_GROUPS),
                 ssd_norm_g.reshape(SSD_GROUPS, SSD_D_INNER // SSD_GROUPS)).reshape(b, L, SSD_D_INNER)

    q_pos = pos0 + jnp.arange(L)
    k_pos = jnp.arange(pos0 + L)
    cos_q, sin_q = rope_cos_sin(q_pos)
    q = (rms_norm(c_q, q_norm_g) @ w_uq).reshape(b, L, MLA_HEADS, QK_NOPE + QK_ROPE)
    q_nope, q_rope = q[..., :QK_NOPE], rotate(q[..., QK_NOPE:], cos_q[:, None, :], sin_q[:, None, :])
    lat_new = rms_norm(c_kv, kv_norm_g)
    kr_new = rotate(k_r, cos_q, sin_q)
    all_lat = jnp.concatenate([lat_prev.astype(lat_new.dtype), lat_new], axis=1)
    all_kr = jnp.concatenate([kr_prev.astype(kr_new.dtype), kr_new], axis=1)
    K = all_lat.shape[1]
    kv = (all_lat @ w_ukv).reshape(b, K, MLA_HEADS, QK_NOPE + V_HEAD)
    k_nope, v = kv[..., :QK_NOPE], kv[..., QK_NOPE:]
    if L > Q_BLOCK and L % Q_BLOCK == 0:
        nb = L // Q_BLOCK
        qn_b = q_nope.reshape(b, nb, Q_BLOCK, MLA_HEADS, QK_NOPE).swapaxes(0, 1)
        qr_b = q_rope.reshape(b, nb, Q_BLOCK, MLA_HEADS, QK_ROPE).swapaxes(0, 1)
        qp_b = q_pos.reshape(nb, Q_BLOCK)
        o = lax.map(lambda a: mla_attend(a[0], a[1], k_nope, all_kr, v, a[2], k_pos), (qn_b, qr_b, qp_b))
        o = o.swapaxes(0, 1).reshape(b, L, MLA_WIDTH)
    else:
        o = mla_attend(q_nope, q_rope, k_nope, all_kr, v, q_pos, k_pos).reshape(b, L, MLA_WIDTH)

    out = jnp.concatenate([y, o], axis=-1) @ w_out
    return out, new_conv, h_new, lat_new, kr_new


def layer(x, conv_prev, h0, lat_prev, kr_prev, pos0,
          ln1_g, ln1_b, ffn1_w_gate, ffn1_w_up, ffn1_w_down,
          w_in, conv_w, conv_b, dt_bias, a_log, d_skip, ssd_norm_g,
          q_norm_g, w_uq, kv_norm_g, w_ukv, w_out, ln2_g, ln2_b,
          ffn2_w_gate, ffn2_w_up, ffn2_w_down, ln3_g, ln3_b):
    x = layer_norm(ALPHA * x + 0.5 * swiglu(x, ffn1_w_gate, ffn1_w_up, ffn1_w_down), ln1_g, ln1_b)
    mix, new_conv, h_new, lat_new, kr_new = token_mix(
        x, conv_prev, h0, lat_prev, kr_prev, pos0, w_in, conv_w, conv_b, dt_bias, a_log, d_skip,
        ssd_norm_g, q_norm_g, w_uq, kv_norm_g, w_ukv, w_out)
    x = layer_norm(ALPHA * x + mix, ln2_g, ln2_b)
    x = layer_norm(ALPHA * x + 0.5 * swiglu(x, ffn2_w_gate, ffn2_w_up, ffn2_w_down), ln3_g, ln3_b)
    return x, new_conv, h_new, lat_new, kr_new


def setup_inputs(seed: int = 0) -> dict:
    key = jax.random.key(seed)
    ks = jax.random.split(key, 40)
    f32 = jnp.float32

    def nrm(k, shape, scale):
        return jax.random.normal(k, shape, f32) * scale

    def gain(k, shape):
        return 1.0 + 0.02 * jax.random.normal(k, shape, f32)

    dt0 = jnp.exp(jax.random.uniform(ks[20], (DEPTH, SSD_HEADS), f32, math.log(1e-3), math.log(1e-1)))
    return {
        "x_prompt": nrm(ks[0], (BATCH, SEQ, D_MODEL), 1.0),
        "x_sample": nrm(ks[1], (DEC_BATCH, DEC_SEQ, D_MODEL), 1.0),
        "cache_latent": nrm(ks[2], (DEPTH, DEC_BATCH, PAST_LEN, KV_LORA), 1.0),
        "cache_k_rope": nrm(ks[3], (DEPTH, DEC_BATCH, PAST_LEN, QK_ROPE), 1.0),
        "state_conv": nrm(ks[4], (DEPTH, DEC_BATCH, CONV_W - 1, SSD_CONV_DIM), 1.0),
        "state_ssm": nrm(ks[5], (DEPTH, DEC_BATCH, SSD_HEADS, SSD_HEAD_DIM, SSD_STATE), 0.1),
        "ln1_g": gain(ks[6], (DEPTH, D_MODEL)),
        "ln1_b": nrm(ks[7], (DEPTH, D_MODEL), 0.02),
        "ffn1_w_gate": nrm(ks[8], (DEPTH, D_MODEL, D_FF), D_MODEL ** -0.5),
        "ffn1_w_up": nrm(ks[9], (DEPTH, D_MODEL, D_FF), D_MODEL ** -0.5),
        "ffn1_w_down": nrm(ks[10], (DEPTH, D_FF, D_MODEL), BETA * D_FF ** -0.5),
        "w_in": nrm(ks[11], (DEPTH, D_MODEL, D_IN_PROJ), D_MODEL ** -0.5),
        "conv_w": nrm(ks[12], (DEPTH, CONV_W, SSD_CONV_DIM), CONV_W ** -0.5),
        "conv_b": nrm(ks[13], (DEPTH, SSD_CONV_DIM), 0.02),
        "dt_bias": dt0 + jnp.log(-jnp.expm1(-dt0)),
        "a_log": jnp.log(jax.random.uniform(ks[14], (DEPTH, SSD_HEADS), f32, 1.0, 16.0)),
        "d_skip": gain(ks[15], (DEPTH, SSD_HEADS)),
        "ssd_norm_g": gain(ks[16], (DEPTH, SSD_D_INNER)),
        "q_norm_g": gain(ks[17], (DEPTH, Q_LORA)),
        "w_uq": nrm(ks[18], (DEPTH, Q_LORA, MLA_HEADS * (QK_NOPE + QK_ROPE)), Q_LORA ** -0.5),
        "kv_norm_g": gain(ks[19], (DEPTH, KV_LORA)),
        "w_ukv": nrm(ks[21], (DEPTH, KV_LORA, MLA_HEADS * (QK_NOPE + V_HEAD)), KV_LORA ** -0.5),
        "w_out": nrm(ks[22], (DEPTH, D_MIX, D_MODEL), BETA * D_MIX ** -0.5),
        "ln2_g": gain(ks[23], (DEPTH, D_MODEL)),
        "ln2_b": nrm(ks[24], (DEPTH, D_MODEL), 0.02),
        "ffn2_w_gate": nrm(ks[25], (DEPTH, D_MODEL, D_FF), D_MODEL ** -0.5),
        "ffn2_w_up": nrm(ks[26], (DEPTH, D_MODEL, D_FF), D_MODEL ** -0.5),
        "ffn2_w_down": nrm(ks[27], (DEPTH, D_FF, D_MODEL), BETA * D_FF ** -0.5),
        "ln3_g": gain(ks[28], (DEPTH, D_MODEL)),
        "ln3_b": nrm(ks[29], (DEPTH, D_MODEL), 0.02),
    }


def reference(x_prompt, x_sample, cache_latent, cache_k_rope, state_conv, state_ssm,
              ln1_g, ln1_b, ffn1_w_gate, ffn1_w_up, ffn1_w_down,
              w_in, conv_w, conv_b, dt_bias, a_log, d_skip, ssd_norm_g,
              q_norm_g, w_uq, kv_norm_g, w_ukv, w_out, ln2_g, ln2_b,
              ffn2_w_gate, ffn2_w_up, ffn2_w_down, ln3_g, ln3_b):
    xp, xs = x_prompt, x_sample
    bp = xp.shape[0]
    lat_p, kr_p, conv_p, ssm_p = [], [], [], []
    lat_s, kr_s, conv_s, ssm_s = [], [], [], []
    for l in range(DEPTH):
        lw = (ln1_g[l], ln1_b[l], ffn1_w_gate[l], ffn1_w_up[l], ffn1_w_down[l],
              w_in[l], conv_w[l], conv_b[l], dt_bias[l], a_log[l], d_skip[l], ssd_norm_g[l],
              q_norm_g[l], w_uq[l], kv_norm_g[l], w_ukv[l], w_out[l], ln2_g[l], ln2_b[l],
              ffn2_w_gate[l], ffn2_w_up[l], ffn2_w_down[l], ln3_g[l], ln3_b[l])
        xp, c1, h1, la1, kr1 = layer(
            xp, jnp.zeros((bp, CONV_W - 1, SSD_CONV_DIM), xp.dtype),
            jnp.zeros((bp, SSD_HEADS, SSD_HEAD_DIM, SSD_STATE), xp.dtype),
            jnp.zeros((bp, 0, KV_LORA), xp.dtype), jnp.zeros((bp, 0, QK_ROPE), xp.dtype), 0, *lw)
        xs, c2, h2, la2, kr2 = layer(
            xs, state_conv[l], state_ssm[l], cache_latent[l], cache_k_rope[l], PAST_LEN, *lw)
        lat_p.append(la1); kr_p.append(kr1); conv_p.append(c1); ssm_p.append(h1)
        lat_s.append(la2); kr_s.append(kr2); conv_s.append(c2); ssm_s.append(h2)
    return (xp, xs,
            jnp.stack(lat_p), jnp.stack(kr_p), jnp.stack(conv_p), jnp.stack(ssm_p),
            jnp.stack(lat_s), jnp.stack(kr_s), jnp.stack(conv_s), jnp.stack(ssm_s))
```

```python
import functools

import numpy as np
import jax
import jax.numpy as jnp
from jax import lax
from jax.experimental import pallas as pl
from jax.experimental.pallas import tpu as pltpu

F32 = jnp.float32
BF16 = jnp.bfloat16

D_MODEL = 1024
D_FF = 2816
CHUNK = 64
SSD_HEADS = 16
SSD_HEAD_DIM = 64
SSD_D_INNER = SSD_HEADS * SSD_HEAD_DIM
SSD_GROUPS = 2
SSD_HEADS_PER_GROUP = SSD_HEADS // SSD_GROUPS
SSD_STATE = 128
CONV_W = 4
SSD_CONV_DIM = SSD_D_INNER + 2 * SSD_GROUPS * SSD_STATE
MLA_HEADS = 8
Q_LORA = 512
KV_LORA = 512
QK_NOPE = 128
QK_ROPE = 64
V_HEAD = 128
MLA_WIDTH = MLA_HEADS * V_HEAD
ROPE_THETA = 10000.0
DEPTH = 1
ALPHA = (2 * DEPTH) ** 0.25
EPS = 1e-5
SM_SCALE = (QK_NOPE + QK_ROPE) ** -0.5
NEG_BIG = -1e30

LANES = 128
QK_PAD = 2 * LANES
SSD_Q = 128
FF_CHUNK = 256
VMEM_LIMIT = 56 * 1024 * 1024

_NT = (((1,), (1,)), ((), ()))


def _resident(shape):
    nd = len(shape)
    return pl.BlockSpec(shape, lambda *_: (0,) * nd, pipeline_mode=pl.Buffered(1))


def _dot(a, b):
    return jnp.dot(a, b, preferred_element_type=F32)


def _dot_nt(a, b):
    return lax.dot_general(a, b, _NT, preferred_element_type=F32)


def _sigmoid(x):
    return 1.0 / (1.0 + jnp.exp(-x))


def _silu(x):
    return x * _sigmoid(x)


def _layer_norm(y, g, b):
    mu = jnp.mean(y, axis=-1, keepdims=True)
    d = y - mu
    var = jnp.mean(d * d, axis=-1, keepdims=True)
    return d * lax.rsqrt(var + EPS) * g + b


def _rms_norm(x, g):
    return x * lax.rsqrt(jnp.mean(x * x, axis=-1, keepdims=True) + EPS) * g


def _swiglu(xb, wg_ref, wu_ref, wd_ref):
    acc = jnp.zeros((xb.shape[0], D_MODEL), F32)
    for c in range(D_FF // FF_CHUNK):
        cs = slice(c * FF_CHUNK, (c + 1) * FF_CHUNK)
        g = _dot(xb, wg_ref[:, cs])
        u = _dot(xb, wu_ref[:, cs])
        h = (_silu(g) * u).astype(BF16)
        acc = acc + _dot(h, wd_ref[cs, :])
    return acc


def _ffn_ln_kernel(x_ref, wg_ref, wu_ref, wd_ref, g_ref, b_ref, o_ref):
    x = x_ref[...]
    ff = _swiglu(x.astype(BF16), wg_ref, wu_ref, wd_ref)
    o_ref[...] = _layer_norm(ALPHA * x + 0.5 * ff, g_ref[...], b_ref[...])


def _ffn_ln(x, wg, wu, wd, g, b, tm):
    T = x.shape[0]
    tok = pl.BlockSpec((tm, D_MODEL), lambda i: (i, 0))
    return pl.pallas_call(
        _ffn_ln_kernel,
        grid=(T // tm,),
        in_specs=[tok, _resident(wg.shape), _resident(wu.shape), _resident(wd.shape),
                  _resident(g.shape), _resident(b.shape)],
        out_specs=tok,
        out_shape=jax.ShapeDtypeStruct((T, D_MODEL), F32),
        compiler_params=pltpu.CompilerParams(dimension_semantics=("parallel",),
                                             vmem_limit_bytes=VMEM_LIMIT),
        name="ffn_ln",
    )(x, wg, wu, wd, g, b)


def _mix_ffn_ln_kernel(x1_ref, y_ref, o_ref, wo_ref, g2_ref, b2_ref,
                       wg_ref, wu_ref, wd_ref, g3_ref, b3_ref, out_ref):
    mix = _dot(y_ref[...].astype(BF16), wo_ref[:SSD_D_INNER, :])
    mix = mix + _dot(o_ref[...].astype(BF16), wo_ref[SSD_D_INNER:, :])
    x2 = _layer_norm(ALPHA * x1_ref[...] + mix, g2_ref[...], b2_ref[...])
    ff = _swiglu(x2.astype(BF16), wg_ref, wu_ref, wd_ref)
    out_ref[...] = _layer_norm(ALPHA * x2 + 0.5 * ff, g3_ref[...], b3_ref[...])


def _mix_ffn_ln(x1, y, o, wo, g2, b2, wg, wu, wd, g3, b3, tm):
    T = x1.shape[0]
    tok = pl.BlockSpec((tm, D_MODEL), lambda i: (i, 0))
    return pl.pallas_call(
        _mix_ffn_ln_kernel,
        grid=(T // tm,),
        in_specs=[tok, tok, tok, _resident(wo.shape), _resident(g2.shape), _resident(b2.shape),
                  _resident(wg.shape), _resident(wu.shape), _resident(wd.shape),
                  _resident(g3.shape), _resident(b3.shape)],
        out_specs=tok,
        out_shape=jax.ShapeDtypeStruct((T, D_MODEL), F32),
        compiler_params=pltpu.CompilerParams(dimension_semantics=("parallel",),
                                             vmem_limit_bytes=VMEM_LIMIT),
        name="mix_ffn_ln",
    )(x1, y, o, wo, g2, b2, wg, wu, wd, g3, b3)


_Z0, _XBC0, _DT0, _CQ0, _CKV0, _KR0, _WIN_COLS = 0, 1024, 2560, 2688, 3200, 3712, 3840


def _rope_tail(rs, cc, ss):
    return rs * cc + pltpu.roll(rs, 64, 1) * ss


def _kcat_store(k_ref, h, k_nope, kr_tail):
    k_ref[h, :, :LANES] = k_nope.astype(BF16)
    k_ref[h, :, LANES:] = kr_tail.astype(BF16)


def _in_proj_kernel(x_ref, cc_ref, ss_ref, win_ref, qg_ref, wuq_ref, kvg_ref, wuk_ref, wuv_ref,
                    z_ref, xbc_ref, dt_ref, lat_ref, kr_ref, q_ref, k_ref, v_ref, *, v_transposed):
    xb = x_ref[...].astype(BF16)
    cc = cc_ref[...]
    ss = ss_ref[...]
    z_ref[...] = _dot(xb, win_ref[:, _Z0:_XBC0])
    xbc_ref[...] = _dot(xb, win_ref[:, _XBC0:_DT0])
    dt_ref[...] = _dot(xb, win_ref[:, _DT0:_CQ0])

    cq = _rms_norm(_dot(xb, win_ref[:, _CQ0:_CKV0]), qg_ref[...]).astype(BF16)
    for h in range(MLA_HEADS):
        qh = _dot(cq, wuq_ref[:, h * QK_PAD:(h + 1) * QK_PAD])
        q_ref[h, :, :LANES] = qh[:, :LANES].astype(BF16)
        q_ref[h, :, LANES:] = _rope_tail(qh[:, LANES:], cc, ss).astype(BF16)

    lat = _rms_norm(_dot(xb, win_ref[:, _CKV0:_KR0]), kvg_ref[...])
    lat_ref[...] = lat
    latb = lat.astype(BF16)
    kr_tail = _rope_tail(_dot(xb, win_ref[:, _KR0:_WIN_COLS]), cc, ss)
    kr_ref[...] = kr_tail[:, :QK_ROPE]
    for h in range(MLA_HEADS):
        _kcat_store(k_ref, h, _dot(latb, wuk_ref[:, h * QK_NOPE:(h + 1) * QK_NOPE]), kr_tail)
    if v_transposed:
        tk = v_ref.shape[-1]
        for h in range(MLA_HEADS):
            vt = _dot_nt(wuv_ref[h * V_HEAD:(h + 1) * V_HEAD, :], latb)
            for s in range(v_ref.shape[2]):
                v_ref[0, h, s] = vt[:, s * tk:(s + 1) * tk].astype(BF16)
    else:
        v_ref[...] = _dot(latb, wuv_ref[...]).astype(BF16)


def _in_proj(x1, cc, ss, win, qg, wuq, kvg, wuk, wuv, *, tm, seq_len, v_transposed, tk):
    T = x1.shape[0]
    nt = T // tm
    tok = lambda w: pl.BlockSpec((tm, w), lambda i: (i, 0))
    if cc.shape[0] == T:
        tab = pl.BlockSpec((tm, LANES), lambda i: (i, 0))
    else:
        per_seq = seq_len // tm
        tab = pl.BlockSpec((tm, LANES), lambda i: (i % per_seq, 0))
    head_tok = pl.BlockSpec((MLA_HEADS, tm, QK_PAD), lambda i: (0, i, 0))
    if v_transposed:
        per_seq = seq_len // tm
        nb = T // seq_len
        v_shape = jax.ShapeDtypeStruct((nb, MLA_HEADS, seq_len // tk, V_HEAD, tk), BF16)
        v_spec = pl.BlockSpec((1, MLA_HEADS, tm // tk, V_HEAD, tk),
                              lambda i: (i // per_seq, 0, i % per_seq, 0, 0))
    else:
        v_shape = jax.ShapeDtypeStruct((T, MLA_WIDTH), BF16)
        v_spec = tok(MLA_WIDTH)
    out_shape = (
        jax.ShapeDtypeStruct((T, SSD_D_INNER), F32),
        jax.ShapeDtypeStruct((T, SSD_CONV_DIM), F32),
        jax.ShapeDtypeStruct((T, LANES), F32),
        jax.ShapeDtypeStruct((T, KV_LORA), F32),
        jax.ShapeDtypeStruct((T, QK_ROPE), F32),
        jax.ShapeDtypeStruct((MLA_HEADS, T, QK_PAD), BF16),
        jax.ShapeDtypeStruct((MLA_HEADS, T, QK_PAD), BF16),
        v_shape,
    )
    out_specs = (tok(SSD_D_INNER), tok(SSD_CONV_DIM), tok(LANES), tok(KV_LORA), tok(QK_ROPE),
                 head_tok, head_tok, v_spec)
    return pl.pallas_call(
        functools.partial(_in_proj_kernel, v_transposed=v_transposed),
        grid=(nt,),
        in_specs=[tok(D_MODEL), tab, tab, _resident(win.shape), _resident(qg.shape),
                  _resident(wuq.shape), _resident(kvg.shape), _resident(wuk.shape),
                  _resident(wuv.shape)],
        out_specs=out_specs,
        out_shape=out_shape,
        compiler_params=pltpu.CompilerParams(dimension_semantics=("parallel",),
                                             vmem_limit_bytes=VMEM_LIMIT),
        name="in_proj",
    )(x1, cc, ss, win, qg, wuq, kvg, wuk, wuv)


def _cache_kv_kernel(lat_ref, kr_ref, wuk_ref, wuv_ref, k_ref, v_ref):
    latb = lat_ref[...].astype(BF16)
    kr = kr_ref[...].astype(BF16)
    zeros = jnp.zeros((kr.shape[0], LANES), BF16)
    for h in range(MLA_HEADS):
        k_ref[h, :, :LANES] = _dot(latb, wuk_ref[:, h * QK_NOPE:(h + 1) * QK_NOPE]).astype(BF16)
        k_ref[h, :, LANES:] = zeros
        k_ref[h, :, LANES:LANES + QK_ROPE] = kr
    v_ref[...] = _dot(latb, wuv_ref[...]).astype(BF16)


def _cache_kv(lat, kr, wuk, wuv, tm):
    T = lat.shape[0]
    return pl.pallas_call(
        _cache_kv_kernel,
        grid=(T // tm,),
        in_specs=[pl.BlockSpec((tm, KV_LORA), lambda i: (i, 0)),
                  pl.BlockSpec((tm, QK_ROPE), lambda i: (i, 0)),
                  _resident(wuk.shape), _resident(wuv.shape)],
        out_specs=(pl.BlockSpec((MLA_HEADS, tm, QK_PAD), lambda i: (0, i, 0)),
                   pl.BlockSpec((tm, MLA_WIDTH), lambda i: (i, 0))),
        out_shape=(jax.ShapeDtypeStruct((MLA_HEADS, T, QK_PAD), BF16),
                   jax.ShapeDtypeStruct((T, MLA_WIDTH), BF16)),
        compiler_params=pltpu.CompilerParams(dimension_semantics=("parallel",),
                                             vmem_limit_bytes=VMEM_LIMIT),
        name="cache_kv",
    )(lat, kr, wuk, wuv)


def _split3(x):
    hi = x.astype(BF16)
    r1 = x - hi.astype(F32)
    mid = r1.astype(BF16)
    lo = (r1 - mid.astype(F32)).astype(BF16)
    return hi, mid, lo


def _softplus(x):
    return jnp.maximum(x, 0.0) + jnp.log1p(jnp.exp(-jnp.abs(x)))


def _ssd_kernel(*refs, lb, zero_init):
    if zero_init:
        (xbc_ref, z_ref, dt_ref, cw_ref, cb_ref, dtb_ref, alog_ref, dsk_ref, ng_ref,
         y_ref, nconv_ref, hnew_ref, win_ref, h_ref, dtp_ref) = refs
        cprev_ref = h0_ref = None
    else:
        (xbc_ref, z_ref, dt_ref, cprev_ref, h0_ref, cw_ref, cb_ref, dtb_ref, alog_ref, dsk_ref,
         ng_ref, y_ref, nconv_ref, hnew_ref, win_ref, h_ref, dtp_ref) = refs
    Q = SSD_Q
    j = pl.program_id(1)
    last = pl.num_programs(1) - 1

    @pl.when(j == 0)
    def _():
        if zero_init:
            win_ref[0:8, :] = jnp.zeros((8, SSD_CONV_DIM), F32)
            h_ref[...] = jnp.zeros(h_ref.shape, F32)
        else:
            win_ref[8 - (CONV_W - 1):8, :] = cprev_ref[0]
            h_ref[...] = h0_ref[0]

    win_ref[8:8 + lb, :] = xbc_ref[...]
    if lb < Q:
        win_ref[8 + lb:, :] = jnp.zeros((Q - lb, SSD_CONV_DIM), F32)
    conv = cb_ref[...]
    for k in range(CONV_W):
        conv = conv + win_ref[8 - (CONV_W - 1) + k:8 - (CONV_W - 1) + k + Q, :] * cw_ref[k:k + 1, :]
    nconv_tail = win_ref[8 + lb - (CONV_W - 1):8 + lb, :]
    win_ref[0:8, :] = win_ref[lb:lb + 8, :]
    xc = _silu(conv)
    xs = xc[:, :SSD_D_INNER]

    if lb < Q:
        dtp_ref[...] = jnp.zeros((Q, LANES), F32)
        dtp_ref[0:lb, :] = dt_ref[...]
        dt_raw = dtp_ref[...]
    else:
        dt_raw = dt_ref[...]
    row_id = lax.broadcasted_iota(jnp.int32, (Q, LANES), 0)
    dt_col = jnp.where(row_id < lb, _softplus(dt_raw + dtb_ref[...]), 0.0)
    da_col = dt_col * (-jnp.exp(alog_ref[...]))
    ii = lax.broadcasted_iota(jnp.int32, (Q, Q), 0)
    jj = lax.broadcasted_iota(jnp.int32, (Q, Q), 1)
    causal = ii >= jj
    tri = causal.astype(BF16)
    a_col = sum(_dot(tri, p) for p in _split3(da_col))
    a_row = a_col.T
    dt_row = dt_col.T
    e_col = jnp.exp(a_col)
    a_last = a_row[:, Q - 1:Q]
    w_row = dt_row * jnp.exp(a_last - a_row)
    e_last = jnp.exp(jnp.broadcast_to(a_last, (LANES, LANES)))

    xs_t = xs.T
    lane = lax.broadcasted_iota(jnp.int32, (Q, LANES), 1)
    lo_half = lane < SSD_HEAD_DIM
    y_parts = []
    for g in range(SSD_GROUPS):
        b_g = xc[:, SSD_D_INNER + g * SSD_STATE:SSD_D_INNER + (g + 1) * SSD_STATE].astype(BF16)
        c0 = SSD_D_INNER + SSD_GROUPS * SSD_STATE + g * SSD_STATE
        c_g = xc[:, c0:c0 + SSD_STATE].astype(BF16)
        cb = _dot_nt(c_g, b_g)
        h0 = g * SSD_HEADS_PER_GROUP
        hp = h_ref[h0:h0 + SSD_HEADS_PER_GROUP].reshape(SSD_HEADS_PER_GROUP * SSD_HEAD_DIM, SSD_STATE)
        y_off = _dot_nt(c_g, hp.astype(BF16))
        lhs = []
        for hh in range(SSD_HEADS_PER_GROUP):
            h = h0 + hh
            lhs.append(xs_t[h * SSD_HEAD_DIM:(h + 1) * SSD_HEAD_DIM, :] * w_row[h:h + 1, :])
        st = _dot(jnp.concatenate(lhs, axis=0).astype(BF16), b_g)
        for hh in range(SSD_HEADS_PER_GROUP):
            h = h0 + hh
            h_ref[h] = (e_last[h:h + 1, :] * h_ref[h]
                        + st[hh * SSD_HEAD_DIM:(hh + 1) * SSD_HEAD_DIM, :])
        for pr in range(SSD_HEADS_PER_GROUP // 2):
            ha = h0 + 2 * pr
            ms = []
            for h in (ha, ha + 1):
                seg = a_col[:, h:h + 1] - a_row[h:h + 1, :]
                m = jnp.exp(jnp.where(causal, seg, -jnp.inf)) * cb * dt_row[h:h + 1, :]
                ms.append(m.astype(BF16))
            xp = xs[:, ha * SSD_HEAD_DIM:(ha + 2) * SSD_HEAD_DIM]
            rhs = jnp.concatenate([jnp.where(lo_half, xp, 0.0), jnp.where(lo_half, 0.0, xp)],
                                  axis=0).astype(BF16)
            y_d = _dot(jnp.concatenate(ms, axis=1), rhs)
            yo = y_off[:, 2 * pr * SSD_HEAD_DIM:(2 * pr + 2) * SSD_HEAD_DIM]
            dec = jnp.where(lo_half, e_col[:, ha:ha + 1], e_col[:, ha + 1:ha + 2])
            y_parts.append(y_d + yo * dec)
    y = jnp.concatenate(y_parts, axis=1) + dsk_ref[...] * xs
    y = y[:lb] * _silu(z_ref[...])
    half = SSD_D_INNER // SSD_GROUPS
    ng = ng_ref[...]
    for g in range(SSD_GROUPS):
        cs = slice(g * half, (g + 1) * half)
        y_ref[:, cs] = _rms_norm(y[:, cs], ng[:, cs])

    @pl.when(j == last)
    def _():
        nconv_ref[0] = nconv_tail
        hnew_ref[0] = h_ref[...]


def _ssd(xbc, z, dt, cprev, h0, cw, cb, dtb, alog, dsk, ng, *, nb, seq_len):
    lb = min(SSD_Q, seq_len)
    nblk = seq_len // lb
    zero_init = cprev is None
    tokb = lambda w: pl.BlockSpec((lb, w), lambda b, j: (b * nblk + j, 0))
    in_specs = [tokb(SSD_CONV_DIM), tokb(SSD_D_INNER), tokb(LANES)]
    args = [xbc, z, dt]
    if not zero_init:
        in_specs += [pl.BlockSpec((1, CONV_W - 1, SSD_CONV_DIM), lambda b, j: (b, 0, 0)),
                     pl.BlockSpec((1, SSD_HEADS, SSD_HEAD_DIM, SSD_STATE), lambda b, j: (b, 0, 0, 0))]
        args += [cprev, h0]
    params = [cw, cb, dtb, alog, dsk, ng]
    in_specs += [pl.BlockSpec(p.shape, lambda b, j: (0, 0)) for p in params]
    args += params
    return pl.pallas_call(
        functools.partial(_ssd_kernel, lb=lb, zero_init=zero_init),
        grid=(nb, nblk),
        in_specs=in_specs,
        out_specs=(tokb(SSD_D_INNER),
                   pl.BlockSpec((1, CONV_W - 1, SSD_CONV_DIM), lambda b, j: (b, 0, 0)),
                   pl.BlockSpec((1, SSD_HEADS, SSD_HEAD_DIM, SSD_STATE), lambda b, j: (b, 0, 0, 0))),
        out_shape=(jax.ShapeDtypeStruct((nb * seq_len, SSD_D_INNER), F32),
                   jax.ShapeDtypeStruct((nb, CONV_W - 1, SSD_CONV_DIM), F32),
                   jax.ShapeDtypeStruct((nb, SSD_HEADS, SSD_HEAD_DIM, SSD_STATE), F32)),
        scratch_shapes=[pltpu.VMEM((SSD_Q + 8, SSD_CONV_DIM), F32),
                        pltpu.VMEM((SSD_HEADS, SSD_HEAD_DIM, SSD_STATE), F32),
                        pltpu.VMEM((SSD_Q, LANES), F32)],
        compiler_params=pltpu.CompilerParams(dimension_semantics=("arbitrary", "arbitrary"),
                                             vmem_limit_bytes=VMEM_LIMIT),
        name="ssd",
    )(*args)


def _attn_prompt_kernel(q_ref, k_ref, vt_ref, o_ref, *, seq_len, tq):
    nq = seq_len // tq

    def tile(q, q0, k0, kt, carry, masked):
        m, l, acc = carry
        k = k_ref[0, pl.ds(k0, tq), :]
        s = _dot_nt(k, q) * SM_SCALE
        if masked:
            kpos = k0 + lax.broadcasted_iota(jnp.int32, (tq, tq), 0)
            qpos = q0 + lax.broadcasted_iota(jnp.int32, (tq, tq), 1)
            s = jnp.where((kpos // CHUNK) <= (qpos // CHUNK), s, NEG_BIG)
        m_new = jnp.maximum(m, jnp.max(s, axis=0, keepdims=True))
        alpha = jnp.exp(m - m_new)
        p = jnp.exp(s - m_new)
        l = alpha * l + jnp.sum(p, axis=0, keepdims=True)
        acc = alpha * acc + _dot(vt_ref[0, 0, kt], p.astype(BF16))
        return m_new, l, acc

    def q_body(qi, _):
        q0 = pl.multiple_of(qi * tq, tq)
        q = q_ref[0, pl.ds(q0, tq), :]
        init = (jnp.full((1, tq), NEG_BIG, F32), jnp.zeros((1, tq), F32),
                jnp.zeros((V_HEAD, tq), F32))
        carry = lax.fori_loop(
            0, qi, lambda kt, c: tile(q, q0, pl.multiple_of(kt * tq, tq), kt, c, False), init)
        m, l, acc = tile(q, q0, q0, qi, carry, True)
        o_ref[pl.ds(q0, tq), :] = (acc / l).T
        return 0

    lax.fori_loop(0, nq, q_body, 0)


def _attn_prompt(q, k, vt, *, nb, seq_len, tq):
    return pl.pallas_call(
        functools.partial(_attn_prompt_kernel, seq_len=seq_len, tq=tq),
        grid=(nb, MLA_HEADS),
        in_specs=[pl.BlockSpec((1, seq_len, QK_PAD), lambda b, h: (h, b, 0)),
                  pl.BlockSpec((1, seq_len, QK_PAD), lambda b, h: (h, b, 0)),
                  pl.BlockSpec((1, 1, seq_len // tq, V_HEAD, tq), lambda b, h: (b, h, 0, 0, 0))],
        out_specs=pl.BlockSpec((seq_len, V_HEAD), lambda b, h: (b, h)),
        out_shape=jax.ShapeDtypeStruct((nb * seq_len, MLA_WIDTH), F32),
        compiler_params=pltpu.CompilerParams(dimension_semantics=("parallel", "parallel"),
                                             vmem_limit_bytes=VMEM_LIMIT),
        name="attn_prompt",
    )(q, k, vt)


def _attn_sample_kernel(q_ref, kc_ref, kn_ref, vc_ref, vn_ref, o_ref, *, past_len, new_len):
    qpos = past_len + lax.broadcasted_iota(jnp.int32, (new_len, 1), 0)
    kpos_c = lax.broadcasted_iota(jnp.int32, (1, past_len), 1)
    kpos_n = past_len + lax.broadcasted_iota(jnp.int32, (1, new_len), 1)
    mask_c = (kpos_c // CHUNK) <= (qpos // CHUNK)
    mask_n = (kpos_n // CHUNK) <= (qpos // CHUNK)
    for h in range(MLA_HEADS):
        q = q_ref[h]
        sc = jnp.where(mask_c, _dot_nt(q, kc_ref[h]) * SM_SCALE, NEG_BIG)
        sn = jnp.where(mask_n, _dot_nt(q, kn_ref[h]) * SM_SCALE, NEG_BIG)
        m = jnp.maximum(jnp.max(sc, axis=-1, keepdims=True), jnp.max(sn, axis=-1, keepdims=True))
        pc = jnp.exp(sc - m)
        pn = jnp.exp(sn - m)
        l = jnp.sum(pc, axis=-1, keepdims=True) + jnp.sum(pn, axis=-1, keepdims=True)
        hs = slice(h * V_HEAD, (h + 1) * V_HEAD)
        o = _dot(pc.astype(BF16), vc_ref[:, hs]) + _dot(pn.astype(BF16), vn_ref[:, hs])
        o_ref[:, hs] = o / l


def _attn_sample(q, kc, kn, vc, vn, *, nb, past_len, new_len):
    return pl.pallas_call(
        functools.partial(_attn_sample_kernel, past_len=past_len, new_len=new_len),
        grid=(nb,),
        in_specs=[pl.BlockSpec((MLA_HEADS, new_len, QK_PAD), lambda b: (0, b, 0)),
                  pl.BlockSpec((MLA_HEADS, past_len, QK_PAD), lambda b: (0, b, 0)),
                  pl.BlockSpec((MLA_HEADS, new_len, QK_PAD), lambda b: (0, b, 0)),
                  pl.BlockSpec((past_len, MLA_WIDTH), lambda b: (b, 0)),
                  pl.BlockSpec((new_len, MLA_WIDTH), lambda b: (b, 0))],
        out_specs=pl.BlockSpec((new_len, MLA_WIDTH), lambda b: (b, 0)),
        out_shape=jax.ShapeDtypeStruct((nb * new_len, MLA_WIDTH), F32),
        compiler_params=pltpu.CompilerParams(dimension_semantics=("parallel",),
                                             vmem_limit_bytes=VMEM_LIMIT),
        name="attn_sample",
    )(q, kc, kn, vc, vn)


def _rope_tables(pos):
    inv = ROPE_THETA ** (-jnp.arange(0, QK_ROPE, 2, dtype=F32) / QK_ROPE)
    ang = pos.astype(F32)[:, None] * inv[None, :]
    cos, sin = jnp.cos(ang), jnp.sin(ang)
    pad = jnp.zeros((pos.shape[0], LANES - QK_ROPE), F32)
    return (jnp.concatenate([cos, cos, pad], axis=1), jnp.concatenate([-sin, sin, pad], axis=1))


def _pack_weights(w_in, conv_w, conv_b, dt_bias, a_log, d_skip, ssd_norm_g, q_norm_g, w_uq,
                  kv_norm_g, w_ukv, w_out):
    swap = np.concatenate([np.arange(QK_ROPE // 2, QK_ROPE), np.arange(QK_ROPE // 2)])
    i0 = SSD_D_INNER
    i1 = i0 + SSD_CONV_DIM
    i2 = i1 + SSD_HEADS
    i3 = i2 + Q_LORA
    i4 = i3 + KV_LORA
    w_dt = jnp.pad(w_in[:, i1:i2], ((0, 0), (0, LANES - SSD_HEADS)))
    w_kr = w_in[:, i4:]
    win = jnp.concatenate([w_in[:, :i0], w_in[:, i0:i1], w_dt, w_in[:, i2:i3], w_in[:, i3:i4],
                           w_kr, w_kr[:, swap]], axis=1).astype(BF16)
    uq = w_uq.reshape(Q_LORA, MLA_HEADS, QK_NOPE + QK_ROPE)
    wuq = jnp.concatenate([uq, uq[:, :, QK_NOPE:][:, :, swap]], axis=2)
    wuq = wuq.reshape(Q_LORA, MLA_HEADS * QK_PAD).astype(BF16)
    ukv = w_ukv.reshape(KV_LORA, MLA_HEADS, QK_NOPE + V_HEAD)
    wuk = ukv[:, :, :QK_NOPE].reshape(KV_LORA, MLA_HEADS * QK_NOPE).astype(BF16)
    wuv = ukv[:, :, QK_NOPE:].reshape(KV_LORA, MLA_WIDTH).astype(BF16)
    lane_pad = lambda v: jnp.pad(v, (0, LANES - SSD_HEADS)).reshape(1, LANES)
    return dict(
        win=win, wuq=wuq, wuk=wuk, wuv=wuv, wuv_t=wuv.T,
        qg=q_norm_g.reshape(1, Q_LORA), kvg=kv_norm_g.reshape(1, KV_LORA),
        cw=conv_w, cb=conv_b.reshape(1, SSD_CONV_DIM),
        dtb=lane_pad(dt_bias), alog=lane_pad(a_log),
        dsk=jnp.repeat(d_skip, SSD_HEAD_DIM).reshape(1, SSD_D_INNER),
        ng=ssd_norm_g.reshape(1, SSD_D_INNER),
        wo=w_out.astype(BF16),
    )


def _stream(x, conv_prev, h0, lat_prev, kr_prev, pos0, w, f1, f2, lns):
    nb, L, _ = x.shape
    T = nb * L
    tm = min(512, T)
    x2d = x.reshape(T, D_MODEL)
    (g1, b1), (g2, b2), (g3, b3) = lns
    x1 = _ffn_ln(x2d, *f1, g1, b1, tm)

    prompt = lat_prev is None
    cc, ss = _rope_tables(pos0 + jnp.arange(L))
    if L % tm != 0:
        cc, ss = jnp.tile(cc, (nb, 1)), jnp.tile(ss, (nb, 1))
    tq = 256
    z, xbc, dt, lat, kr, q, kcat, v = _in_proj(
        x1, cc, ss, w["win"], w["qg"], w["wuq"], w["kvg"], w["wuk"],
        w["wuv_t"] if prompt else w["wuv"], tm=tm, seq_len=L, v_transposed=prompt, tk=tq)

    y, nconv, hnew = _ssd(xbc, z, dt, conv_prev, h0, w["cw"], w["cb"], w["dtb"], w["alog"],
                          w["dsk"], w["ng"], nb=nb, seq_len=L)
    if prompt:
        o = _attn_prompt(q, kcat, v, nb=nb, seq_len=L, tq=tq)
    else:
        past = lat_prev.shape[1]
        kc, vc = _cache_kv(lat_prev.reshape(nb * past, KV_LORA), kr_prev.reshape(nb * past, QK_ROPE),
                           w["wuk"], w["wuv"], 512)
        o = _attn_sample(q, kc, kcat, vc, v, nb=nb, past_len=past, new_len=L)
    out = _mix_ffn_ln(x1, y, o, w["wo"], g2, b2, *f2, g3, b3, tm)
    return (out.reshape(nb, L, D_MODEL), lat.reshape(nb, L, KV_LORA), kr.reshape(nb, L, QK_ROPE),
            nconv, hnew)


def kernel(x_prompt, x_sample, cache_latent, cache_k_rope, state_conv, state_ssm, ln1_g, ln1_b, ffn1_w_gate, ffn1_w_up, ffn1_w_down, w_in, conv_w, conv_b, dt_bias, a_log, d_skip, ssd_norm_g, q_norm_g, w_uq, kv_norm_g, w_ukv, w_out, ln2_g, ln2_b, ffn2_w_gate, ffn2_w_up, ffn2_w_down, ln3_g, ln3_b):
    assert w_in.shape[0] == DEPTH == 1
    l = 0
    w = _pack_weights(w_in[l], conv_w[l], conv_b[l], dt_bias[l], a_log[l], d_skip[l],
                      ssd_norm_g[l], q_norm_g[l], w_uq[l], kv_norm_g[l], w_ukv[l], w_out[l])
    f1 = (ffn1_w_gate[l].astype(BF16), ffn1_w_up[l].astype(BF16), ffn1_w_down[l].astype(BF16))
    f2 = (ffn2_w_gate[l].astype(BF16), ffn2_w_up[l].astype(BF16), ffn2_w_down[l].astype(BF16))
    row = lambda v: v[l].reshape(1, D_MODEL)
    lns = ((row(ln1_g), row(ln1_b)), (row(ln2_g), row(ln2_b)), (row(ln3_g), row(ln3_b)))

    yp, lat_p, kr_p, conv_p, ssm_p = _stream(x_prompt, None, None, None, None, 0, w, f1, f2, lns)
    past = cache_latent.shape[2]
    ys, lat_s, kr_s, conv_s, ssm_s = _stream(x_sample, state_conv[l], state_ssm[l], cache_latent[l],
                                             cache_k_rope[l], past, w, f1, f2, lns)
    st = lambda a: a[None]
    return (yp, ys, st(lat_p), st(kr_p), st(conv_p), st(ssm_p),
            st(lat_s), st(kr_s), st(conv_s), st(ssm_s))
```

```python
import functools

import numpy as np
import jax
import jax.numpy as jnp
from jax import lax
from jax.experimental import pallas as pl
from jax.experimental.pallas import tpu as pltpu

F32 = jnp.float32
BF16 = jnp.bfloat16

D_MODEL = 1024
D_FF = 2816
CHUNK = 64
SSD_HEADS = 16
SSD_HEAD_DIM = 64
SSD_D_INNER = SSD_HEADS * SSD_HEAD_DIM
SSD_GROUPS = 2
SSD_HEADS_PER_GROUP = SSD_HEADS // SSD_GROUPS
SSD_STATE = 128
CONV_W = 4
SSD_CONV_DIM = SSD_D_INNER + 2 * SSD_GROUPS * SSD_STATE
MLA_HEADS = 8
Q_LORA = 512
KV_LORA = 512
QK_NOPE = 128
QK_ROPE = 64
V_HEAD = 128
MLA_WIDTH = MLA_HEADS * V_HEAD
ROPE_THETA = 10000.0
DEPTH = 1
ALPHA = (2 * DEPTH) ** 0.25
EPS = 1e-5
SM_SCALE = (QK_NOPE + QK_ROPE) ** -0.5
NEG_BIG = -1e30

LANES = 128
QK_PAD = 2 * LANES
SSD_Q = 128
FF_CHUNK = 256
ATT_TQ = 1024
ATT_TK = 256
VMEM_LIMIT = 56 * 1024 * 1024

_NT = (((1,), (1,)), ((), ()))


def _resident(shape):
    nd = len(shape)
    return pl.BlockSpec(shape, lambda *_: (0,) * nd, pipeline_mode=pl.Buffered(1))


def _dot(a, b):
    return jnp.dot(a, b, preferred_element_type=F32)


def _dot_nt(a, b):
    return lax.dot_general(a, b, _NT, preferred_element_type=F32)


def _sigmoid(x):
    return 1.0 / (1.0 + jnp.exp(-x))


def _silu(x):
    return x * _sigmoid(x)


def _layer_norm(y, g, b):
    mu = jnp.mean(y, axis=-1, keepdims=True)
    d = y - mu
    var = jnp.mean(d * d, axis=-1, keepdims=True)
    return d * lax.rsqrt(var + EPS) * g + b


def _rms_norm(x, g):
    return x * lax.rsqrt(jnp.mean(x * x, axis=-1, keepdims=True) + EPS) * g


def _swiglu(xb, wg_ref, wu_ref, wd_ref):
    acc = jnp.zeros((xb.shape[0], D_MODEL), F32)
    for c in range(D_FF // FF_CHUNK):
        cs = slice(c * FF_CHUNK, (c + 1) * FF_CHUNK)
        g = _dot(xb, wg_ref[:, cs])
        u = _dot(xb, wu_ref[:, cs])
        h = (_silu(g) * u).astype(BF16)
        acc = acc + _dot(h, wd_ref[cs, :])
    return acc


def _ffn_ln_kernel(x_ref, wg_ref, wu_ref, wd_ref, g_ref, b_ref, o_ref):
    x = x_ref[...]
    ff = _swiglu(x.astype(BF16), wg_ref, wu_ref, wd_ref)
    o_ref[...] = _layer_norm(ALPHA * x + 0.5 * ff, g_ref[...], b_ref[...])


def _ffn_ln(x, wg, wu, wd, g, b, tm):
    T = x.shape[0]
    tok = pl.BlockSpec((tm, D_MODEL), lambda i: (i, 0))
    return pl.pallas_call(
        _ffn_ln_kernel,
        grid=(T // tm,),
        in_specs=[tok, _resident(wg.shape), _resident(wu.shape), _resident(wd.shape),
                  _resident(g.shape), _resident(b.shape)],
        out_specs=tok,
        out_shape=jax.ShapeDtypeStruct((T, D_MODEL), F32),
        compiler_params=pltpu.CompilerParams(dimension_semantics=("parallel",),
                                             vmem_limit_bytes=VMEM_LIMIT),
        name="ffn_ln",
    )(x, wg, wu, wd, g, b)


def _mix_ffn_ln_kernel(x1_ref, y_ref, o_ref, wo_ref, g2_ref, b2_ref,
                       wg_ref, wu_ref, wd_ref, g3_ref, b3_ref, out_ref):
    mix = _dot(y_ref[...].astype(BF16), wo_ref[:SSD_D_INNER, :])
    mix = mix + _dot(o_ref[...].astype(BF16), wo_ref[SSD_D_INNER:, :])
    x2 = _layer_norm(ALPHA * x1_ref[...] + mix, g2_ref[...], b2_ref[...])
    ff = _swiglu(x2.astype(BF16), wg_ref, wu_ref, wd_ref)
    out_ref[...] = _layer_norm(ALPHA * x2 + 0.5 * ff, g3_ref[...], b3_ref[...])


def _mix_ffn_ln(x1, y, o, wo, g2, b2, wg, wu, wd, g3, b3, tm):
    T = x1.shape[0]
    tok = pl.BlockSpec((tm, D_MODEL), lambda i: (i, 0))
    return pl.pallas_call(
        _mix_ffn_ln_kernel,
        grid=(T // tm,),
        in_specs=[tok, tok, tok, _resident(wo.shape), _resident(g2.shape), _resident(b2.shape),
                  _resident(wg.shape), _resident(wu.shape), _resident(wd.shape),
                  _resident(g3.shape), _resident(b3.shape)],
        out_specs=tok,
        out_shape=jax.ShapeDtypeStruct((T, D_MODEL), F32),
        compiler_params=pltpu.CompilerParams(dimension_semantics=("parallel",),
                                             vmem_limit_bytes=VMEM_LIMIT),
        name="mix_ffn_ln",
    )(x1, y, o, wo, g2, b2, wg, wu, wd, g3, b3)


_Z0, _XBC0, _DT0, _CQ0, _CKV0, _KR0, _WIN_COLS = 0, 1024, 2560, 2688, 3200, 3712, 3840


def _rope_tail(rs, cc, ss):
    return rs * cc + pltpu.roll(rs, 64, 1) * ss


def _kcat_store(k_ref, h, k_nope, kr_tail):
    k_ref[h, :, :LANES] = k_nope.astype(BF16)
    k_ref[h, :, LANES:] = kr_tail.astype(BF16)


def _in_proj_kernel(x_ref, cc_ref, ss_ref, win_ref, qg_ref, wuq_ref, kvg_ref, wuk_ref, wuv_ref,
                    z_ref, xbc_ref, dt_ref, lat_ref, kr_ref, q_ref, k_ref, v_ref):
    xb = x_ref[...].astype(BF16)
    cc = cc_ref[...]
    ss = ss_ref[...]
    z_ref[...] = _dot(xb, win_ref[:, _Z0:_XBC0])
    xbc_ref[...] = _dot(xb, win_ref[:, _XBC0:_DT0])
    dt_ref[...] = _dot(xb, win_ref[:, _DT0:_CQ0])

    cq = _rms_norm(_dot(xb, win_ref[:, _CQ0:_CKV0]), qg_ref[...]).astype(BF16)
    for h in range(MLA_HEADS):
        qh = _dot(cq, wuq_ref[:, h * QK_PAD:(h + 1) * QK_PAD])
        q_ref[h, :, :LANES] = qh[:, :LANES].astype(BF16)
        q_ref[h, :, LANES:] = _rope_tail(qh[:, LANES:], cc, ss).astype(BF16)

    lat = _rms_norm(_dot(xb, win_ref[:, _CKV0:_KR0]), kvg_ref[...])
    lat_ref[...] = lat
    latb = lat.astype(BF16)
    kr_tail = _rope_tail(_dot(xb, win_ref[:, _KR0:_WIN_COLS]), cc, ss)
    kr_ref[...] = kr_tail[:, :QK_ROPE]
    for h in range(MLA_HEADS):
        _kcat_store(k_ref, h, _dot(latb, wuk_ref[:, h * QK_NOPE:(h + 1) * QK_NOPE]), kr_tail)
    v_ref[...] = _dot(latb, wuv_ref[...]).astype(BF16)


def _in_proj(x1, cc, ss, win, qg, wuq, kvg, wuk, wuv, *, tm, seq_len):
    T = x1.shape[0]
    nt = T // tm
    tok = lambda w: pl.BlockSpec((tm, w), lambda i: (i, 0))
    if cc.shape[0] == T:
        tab = pl.BlockSpec((tm, LANES), lambda i: (i, 0))
    else:
        per_seq = seq_len // tm
        tab = pl.BlockSpec((tm, LANES), lambda i: (i % per_seq, 0))
    head_tok = pl.BlockSpec((MLA_HEADS, tm, QK_PAD), lambda i: (0, i, 0))
    out_shape = (
        jax.ShapeDtypeStruct((T, SSD_D_INNER), F32),
        jax.ShapeDtypeStruct((T, SSD_CONV_DIM), F32),
        jax.ShapeDtypeStruct((T, LANES), F32),
        jax.ShapeDtypeStruct((T, KV_LORA), F32),
        jax.ShapeDtypeStruct((T, QK_ROPE), F32),
        jax.ShapeDtypeStruct((MLA_HEADS, T, QK_PAD), BF16),
        jax.ShapeDtypeStruct((MLA_HEADS, T, QK_PAD), BF16),
        jax.ShapeDtypeStruct((T, MLA_WIDTH), BF16),
    )
    out_specs = (tok(SSD_D_INNER), tok(SSD_CONV_DIM), tok(LANES), tok(KV_LORA), tok(QK_ROPE),
                 head_tok, head_tok, tok(MLA_WIDTH))
    return pl.pallas_call(
        _in_proj_kernel,
        grid=(nt,),
        in_specs=[tok(D_MODEL), tab, tab, _resident(win.shape), _resident(qg.shape),
                  _resident(wuq.shape), _resident(kvg.shape), _resident(wuk.shape),
                  _resident(wuv.shape)],
        out_specs=out_specs,
        out_shape=out_shape,
        compiler_params=pltpu.CompilerParams(dimension_semantics=("parallel",),
                                             vmem_limit_bytes=VMEM_LIMIT),
        name="in_proj",
    )(x1, cc, ss, win, qg, wuq, kvg, wuk, wuv)


def _cache_kv_kernel(lat_ref, kr_ref, wuk_ref, wuv_ref, k_ref, v_ref):
    latb = lat_ref[...].astype(BF16)
    kr = kr_ref[...].astype(BF16)
    zeros = jnp.zeros((kr.shape[0], LANES), BF16)
    for h in range(MLA_HEADS):
        k_ref[h, :, :LANES] = _dot(latb, wuk_ref[:, h * QK_NOPE:(h + 1) * QK_NOPE]).astype(BF16)
        k_ref[h, :, LANES:] = zeros
        k_ref[h, :, LANES:LANES + QK_ROPE] = kr
    v_ref[...] = _dot(latb, wuv_ref[...]).astype(BF16)


def _cache_kv(lat, kr, wuk, wuv, tm):
    T = lat.shape[0]
    return pl.pallas_call(
        _cache_kv_kernel,
        grid=(T // tm,),
        in_specs=[pl.BlockSpec((tm, KV_LORA), lambda i: (i, 0)),
                  pl.BlockSpec((tm, QK_ROPE), lambda i: (i, 0)),
                  _resident(wuk.shape), _resident(wuv.shape)],
        out_specs=(pl.BlockSpec((MLA_HEADS, tm, QK_PAD), lambda i: (0, i, 0)),
                   pl.BlockSpec((tm, MLA_WIDTH), lambda i: (i, 0))),
        out_shape=(jax.ShapeDtypeStruct((MLA_HEADS, T, QK_PAD), BF16),
                   jax.ShapeDtypeStruct((T, MLA_WIDTH), BF16)),
        compiler_params=pltpu.CompilerParams(dimension_semantics=("parallel",),
                                             vmem_limit_bytes=VMEM_LIMIT),
        name="cache_kv",
    )(lat, kr, wuk, wuv)


def _split3(x):
    hi = x.astype(BF16)
    r1 = x - hi.astype(F32)
    mid = r1.astype(BF16)
    lo = (r1 - mid.astype(F32)).astype(BF16)
    return hi, mid, lo


def _softplus(x):
    return jnp.maximum(x, 0.0) + jnp.log1p(jnp.exp(-jnp.abs(x)))


def _ssd_kernel(*refs, lb, zero_init):
    if zero_init:
        (xbc_ref, z_ref, dt_ref, cw_ref, cb_ref, dtb_ref, alog_ref, dsk_ref, ng_ref,
         y_ref, nconv_ref, hnew_ref, win_ref, h_ref, dtp_ref) = refs
        cprev_ref = h0_ref = None
    else:
        (xbc_ref, z_ref, dt_ref, cprev_ref, h0_ref, cw_ref, cb_ref, dtb_ref, alog_ref, dsk_ref,
         ng_ref, y_ref, nconv_ref, hnew_ref, win_ref, h_ref, dtp_ref) = refs
    Q = SSD_Q
    j = pl.program_id(1)
    last = pl.num_programs(1) - 1

    @pl.when(j == 0)
    def _():
        if zero_init:
            win_ref[0:8, :] = jnp.zeros((8, SSD_CONV_DIM), F32)
            h_ref[...] = jnp.zeros(h_ref.shape, F32)
        else:
            win_ref[8 - (CONV_W - 1):8, :] = cprev_ref[0]
            h_ref[...] = h0_ref[0]

    win_ref[8:8 + lb, :] = xbc_ref[...]
    if lb < Q:
        win_ref[8 + lb:, :] = jnp.zeros((Q - lb, SSD_CONV_DIM), F32)
    conv = cb_ref[...]
    for k in range(CONV_W):
        conv = conv + win_ref[8 - (CONV_W - 1) + k:8 - (CONV_W - 1) + k + Q, :] * cw_ref[k:k + 1, :]
    nconv_tail = win_ref[8 + lb - (CONV_W - 1):8 + lb, :]
    win_ref[0:8, :] = win_ref[lb:lb + 8, :]
    xc = _silu(conv)
    xs = xc[:, :SSD_D_INNER]

    if lb < Q:
        dtp_ref[...] = jnp.zeros((Q, LANES), F32)
        dtp_ref[0:lb, :] = dt_ref[...]
        dt_raw = dtp_ref[...]
    else:
        dt_raw = dt_ref[...]
    row_id = lax.broadcasted_iota(jnp.int32, (Q, LANES), 0)
    dt_col = jnp.where(row_id < lb, _softplus(dt_raw + dtb_ref[...]), 0.0)
    da_col = dt_col * (-jnp.exp(alog_ref[...]))
    ii = lax.broadcasted_iota(jnp.int32, (Q, Q), 0)
    jj = lax.broadcasted_iota(jnp.int32, (Q, Q), 1)
    causal = ii >= jj
    tri = causal.astype(BF16)
    a_col = sum(_dot(tri, p) for p in _split3(da_col))
    a_row = a_col.T
    dt_row = dt_col.T
    e_col = jnp.exp(a_col)
    a_last = a_row[:, Q - 1:Q]
    w_row = dt_row * jnp.exp(a_last - a_row)
    e_last = jnp.exp(jnp.broadcast_to(a_last, (LANES, LANES)))

    xs_t = xs.T
    lane = lax.broadcasted_iota(jnp.int32, (Q, LANES), 1)
    lo_half = lane < SSD_HEAD_DIM
    y_parts = []
    for g in range(SSD_GROUPS):
        b_g = xc[:, SSD_D_INNER + g * SSD_STATE:SSD_D_INNER + (g + 1) * SSD_STATE].astype(BF16)
        c0 = SSD_D_INNER + SSD_GROUPS * SSD_STATE + g * SSD_STATE
        c_g = xc[:, c0:c0 + SSD_STATE].astype(BF16)
        cb = _dot_nt(c_g, b_g)
        h0 = g * SSD_HEADS_PER_GROUP
        hp = h_ref[h0:h0 + SSD_HEADS_PER_GROUP].reshape(SSD_HEADS_PER_GROUP * SSD_HEAD_DIM, SSD_STATE)
        y_off = _dot_nt(c_g, hp.astype(BF16))
        lhs = []
        for hh in range(SSD_HEADS_PER_GROUP):
            h = h0 + hh
            lhs.append(xs_t[h * SSD_HEAD_DIM:(h + 1) * SSD_HEAD_DIM, :] * w_row[h:h + 1, :])
        st = _dot(jnp.concatenate(lhs, axis=0).astype(BF16), b_g)
        for hh in range(SSD_HEADS_PER_GROUP):
            h = h0 + hh
            h_ref[h] = (e_last[h:h + 1, :] * h_ref[h]
                        + st[hh * SSD_HEAD_DIM:(hh + 1) * SSD_HEAD_DIM, :])
        for pr in range(SSD_HEADS_PER_GROUP // 2):
            ha = h0 + 2 * pr
            ms = []
            for h in (ha, ha + 1):
                seg = a_col[:, h:h + 1] - a_row[h:h + 1, :]
                m = jnp.exp(jnp.where(causal, seg, -jnp.inf)) * cb * dt_row[h:h + 1, :]
                ms.append(m.astype(BF16))
            xp = xs[:, ha * SSD_HEAD_DIM:(ha + 2) * SSD_HEAD_DIM]
            rhs = jnp.concatenate([jnp.where(lo_half, xp, 0.0), jnp.where(lo_half, 0.0, xp)],
                                  axis=0).astype(BF16)
            y_d = _dot(jnp.concatenate(ms, axis=1), rhs)
            yo = y_off[:, 2 * pr * SSD_HEAD_DIM:(2 * pr + 2) * SSD_HEAD_DIM]
            dec = jnp.where(lo_half, e_col[:, ha:ha + 1], e_col[:, ha + 1:ha + 2])
            y_parts.append(y_d + yo * dec)
    y = jnp.concatenate(y_parts, axis=1) + dsk_ref[...] * xs
    y = y[:lb] * _silu(z_ref[...])
    half = SSD_D_INNER // SSD_GROUPS
    ng = ng_ref[...]
    for g in range(SSD_GROUPS):
        cs = slice(g * half, (g + 1) * half)
        y_ref[:, cs] = _rms_norm(y[:, cs], ng[:, cs])

    @pl.when(j == last)
    def _():
        nconv_ref[0] = nconv_tail
        hnew_ref[0] = h_ref[...]


def _ssd(xbc, z, dt, cprev, h0, cw, cb, dtb, alog, dsk, ng, *, nb, seq_len):
    lb = min(SSD_Q, seq_len)
    nblk = seq_len // lb
    zero_init = cprev is None
    tokb = lambda w: pl.BlockSpec((lb, w), lambda b, j: (b * nblk + j, 0))
    in_specs = [tokb(SSD_CONV_DIM), tokb(SSD_D_INNER), tokb(LANES)]
    args = [xbc, z, dt]
    if not zero_init:
        in_specs += [pl.BlockSpec((1, CONV_W - 1, SSD_CONV_DIM), lambda b, j: (b, 0, 0)),
                     pl.BlockSpec((1, SSD_HEADS, SSD_HEAD_DIM, SSD_STATE), lambda b, j: (b, 0, 0, 0))]
        args += [cprev, h0]
    params = [cw, cb, dtb, alog, dsk, ng]
    in_specs += [pl.BlockSpec(p.shape, lambda b, j: (0, 0)) for p in params]
    args += params
    return pl.pallas_call(
        functools.partial(_ssd_kernel, lb=lb, zero_init=zero_init),
        grid=(nb, nblk),
        in_specs=in_specs,
        out_specs=(tokb(SSD_D_INNER),
                   pl.BlockSpec((1, CONV_W - 1, SSD_CONV_DIM), lambda b, j: (b, 0, 0)),
                   pl.BlockSpec((1, SSD_HEADS, SSD_HEAD_DIM, SSD_STATE), lambda b, j: (b, 0, 0, 0))),
        out_shape=(jax.ShapeDtypeStruct((nb * seq_len, SSD_D_INNER), F32),
                   jax.ShapeDtypeStruct((nb, CONV_W - 1, SSD_CONV_DIM), F32),
                   jax.ShapeDtypeStruct((nb, SSD_HEADS, SSD_HEAD_DIM, SSD_STATE), F32)),
        scratch_shapes=[pltpu.VMEM((SSD_Q + 8, SSD_CONV_DIM), F32),
                        pltpu.VMEM((SSD_HEADS, SSD_HEAD_DIM, SSD_STATE), F32),
                        pltpu.VMEM((SSD_Q, LANES), F32)],
        compiler_params=pltpu.CompilerParams(dimension_semantics=("arbitrary", "arbitrary"),
                                             vmem_limit_bytes=VMEM_LIMIT),
        name="ssd",
    )(*args)


def _attn_prompt_kernel(q_ref, k_ref, v_ref, o_ref, m_ref, l_ref, acc_ref, *, seq_len, tq, tk):
    ri = lax.broadcasted_iota(jnp.int32, (tk, tk), 0)
    ci = lax.broadcasted_iota(jnp.int32, (tk, tk), 1)
    diag_mask = (ci // CHUNK) <= (ri // CHUNK)

    def step(r0, k0, row0, masked):
        q = q_ref[0, r0 + row0:r0 + tq, :]
        s = _dot_nt(q, k_ref[0, pl.ds(k0, tk), :]) * SM_SCALE
        if masked:
            top = jnp.where(diag_mask, s[:tk], NEG_BIG)
            s = top if tq - row0 == tk else jnp.concatenate([top, s[tk:]], axis=0)
        m_prev = m_ref[row0:, :]
        m_next = jnp.maximum(m_prev, jnp.max(s, axis=-1, keepdims=True))
        alpha = jnp.exp(m_prev - m_next)
        p = jnp.exp(s - jnp.concatenate([m_next] * (tk // LANES), axis=1))
        l_ref[row0:, :] = alpha * l_ref[row0:, :] + jnp.sum(p, axis=-1, keepdims=True)
        m_ref[row0:, :] = m_next
        pv = _dot(p.astype(BF16), v_ref[pl.ds(k0, tk), :])
        acc_ref[row0:, :] = alpha * acc_ref[row0:, :] + pv

    for qb in range(seq_len // tq):
        r0 = qb * tq
        m_ref[...] = jnp.full(m_ref.shape, NEG_BIG, F32)
        l_ref[...] = jnp.zeros(l_ref.shape, F32)
        acc_ref[...] = jnp.zeros(acc_ref.shape, F32)
        if r0 > 0:
            def body(j, c, r0=r0):
                step(r0, pl.multiple_of(j * tk, tk), 0, False)
                return c
            lax.fori_loop(0, r0 // tk, body, 0)
        for d in range(tq // tk):
            step(r0, r0 + d * tk, d * tk, True)
        o_ref[r0:r0 + tq, :] = acc_ref[...] / l_ref[...]


def _attn_prompt(q, k, v, *, nb, seq_len):
    tq = min(ATT_TQ, seq_len)
    tk = min(ATT_TK, tq)
    stat = pltpu.VMEM((tq, LANES), F32)
    return pl.pallas_call(
        functools.partial(_attn_prompt_kernel, seq_len=seq_len, tq=tq, tk=tk),
        grid=(nb, MLA_HEADS),
        in_specs=[pl.BlockSpec((1, seq_len, QK_PAD), lambda b, h: (h, b, 0)),
                  pl.BlockSpec((1, seq_len, QK_PAD), lambda b, h: (h, b, 0)),
                  pl.BlockSpec((seq_len, V_HEAD), lambda b, h: (b, h))],
        out_specs=pl.BlockSpec((seq_len, V_HEAD), lambda b, h: (b, h)),
        out_shape=jax.ShapeDtypeStruct((nb * seq_len, MLA_WIDTH), F32),
        scratch_shapes=[stat, stat, stat],
        compiler_params=pltpu.CompilerParams(dimension_semantics=("parallel", "parallel"),
                                             vmem_limit_bytes=VMEM_LIMIT),
        name="attn_prompt",
    )(q, k, v)


def _attn_sample_kernel(q_ref, kc_ref, kn_ref, vc_ref, vn_ref, o_ref, *, past_len, new_len):
    qpos = past_len + lax.broadcasted_iota(jnp.int32, (new_len, 1), 0)
    kpos_c = lax.broadcasted_iota(jnp.int32, (1, past_len), 1)
    kpos_n = past_len + lax.broadcasted_iota(jnp.int32, (1, new_len), 1)
    mask_c = (kpos_c // CHUNK) <= (qpos // CHUNK)
    mask_n = (kpos_n // CHUNK) <= (qpos // CHUNK)
    for h in range(MLA_HEADS):
        q = q_ref[h]
        sc = jnp.where(mask_c, _dot_nt(q, kc_ref[h]) * SM_SCALE, NEG_BIG)
        sn = jnp.where(mask_n, _dot_nt(q, kn_ref[h]) * SM_SCALE, NEG_BIG)
        m = jnp.maximum(jnp.max(sc, axis=-1, keepdims=True), jnp.max(sn, axis=-1, keepdims=True))
        pc = jnp.exp(sc - m)
        pn = jnp.exp(sn - m)
        l = jnp.sum(pc, axis=-1, keepdims=True) + jnp.sum(pn, axis=-1, keepdims=True)
        hs = slice(h * V_HEAD, (h + 1) * V_HEAD)
        o = _dot(pc.astype(BF16), vc_ref[:, hs]) + _dot(pn.astype(BF16), vn_ref[:, hs])
        o_ref[:, hs] = o / l


def _attn_sample(q, kc, kn, vc, vn, *, nb, past_len, new_len):
    return pl.pallas_call(
        functools.partial(_attn_sample_kernel, past_len=past_len, new_len=new_len),
        grid=(nb,),
        in_specs=[pl.BlockSpec((MLA_HEADS, new_len, QK_PAD), lambda b: (0, b, 0)),
                  pl.BlockSpec((MLA_HEADS, past_len, QK_PAD), lambda b: (0, b, 0)),
                  pl.BlockSpec((MLA_HEADS, new_len, QK_PAD), lambda b: (0, b, 0)),
                  pl.BlockSpec((past_len, MLA_WIDTH), lambda b: (b, 0)),
                  pl.BlockSpec((new_len, MLA_WIDTH), lambda b: (b, 0))],
        out_specs=pl.BlockSpec((new_len, MLA_WIDTH), lambda b: (b, 0)),
        out_shape=jax.ShapeDtypeStruct((nb * new_len, MLA_WIDTH), F32),
        compiler_params=pltpu.CompilerParams(dimension_semantics=("parallel",),
                                             vmem_limit_bytes=VMEM_LIMIT),
        name="attn_sample",
    )(q, kc, kn, vc, vn)


def _rope_tables(pos):
    inv = ROPE_THETA ** (-jnp.arange(0, QK_ROPE, 2, dtype=F32) / QK_ROPE)
    ang = pos.astype(F32)[:, None] * inv[None, :]
    cos, sin = jnp.cos(ang), jnp.sin(ang)
    pad = jnp.zeros((pos.shape[0], LANES - QK_ROPE), F32)
    return (jnp.concatenate([cos, cos, pad], axis=1), jnp.concatenate([-sin, sin, pad], axis=1))


def _pack_weights(w_in, conv_w, conv_b, dt_bias, a_log, d_skip, ssd_norm_g, q_norm_g, w_uq,
                  kv_norm_g, w_ukv, w_out):
    swap = np.concatenate([np.arange(QK_ROPE // 2, QK_ROPE), np.arange(QK_ROPE // 2)])
    i0 = SSD_D_INNER
    i1 = i0 + SSD_CONV_DIM
    i2 = i1 + SSD_HEADS
    i3 = i2 + Q_LORA
    i4 = i3 + KV_LORA
    w_dt = jnp.pad(w_in[:, i1:i2], ((0, 0), (0, LANES - SSD_HEADS)))
    w_kr = w_in[:, i4:]
    win = jnp.concatenate([w_in[:, :i0], w_in[:, i0:i1], w_dt, w_in[:, i2:i3], w_in[:, i3:i4],
                           w_kr, w_kr[:, swap]], axis=1).astype(BF16)
    uq = w_uq.reshape(Q_LORA, MLA_HEADS, QK_NOPE + QK_ROPE)
    wuq = jnp.concatenate([uq, uq[:, :, QK_NOPE:][:, :, swap]], axis=2)
    wuq = wuq.reshape(Q_LORA, MLA_HEADS * QK_PAD).astype(BF16)
    ukv = w_ukv.reshape(KV_LORA, MLA_HEADS, QK_NOPE + V_HEAD)
    wuk = ukv[:, :, :QK_NOPE].reshape(KV_LORA, MLA_HEADS * QK_NOPE).astype(BF16)
    wuv = ukv[:, :, QK_NOPE:].reshape(KV_LORA, MLA_WIDTH).astype(BF16)
    lane_pad = lambda v: jnp.pad(v, (0, LANES - SSD_HEADS)).reshape(1, LANES)
    return dict(
        win=win, wuq=wuq, wuk=wuk, wuv=wuv,
        qg=q_norm_g.reshape(1, Q_LORA), kvg=kv_norm_g.reshape(1, KV_LORA),
        cw=conv_w, cb=conv_b.reshape(1, SSD_CONV_DIM),
        dtb=lane_pad(dt_bias), alog=lane_pad(a_log),
        dsk=jnp.repeat(d_skip, SSD_HEAD_DIM).reshape(1, SSD_D_INNER),
        ng=ssd_norm_g.reshape(1, SSD_D_INNER),
        wo=w_out.astype(BF16),
    )


def _stream(x, conv_prev, h0, lat_prev, kr_prev, pos0, w, f1, f2, lns):
    nb, L, _ = x.shape
    T = nb * L
    tm = min(512, T)
    x2d = x.reshape(T, D_MODEL)
    (g1, b1), (g2, b2), (g3, b3) = lns
    x1 = _ffn_ln(x2d, *f1, g1, b1, tm)

    prompt = lat_prev is None
    cc, ss = _rope_tables(pos0 + jnp.arange(L))
    if L % tm != 0:
        cc, ss = jnp.tile(cc, (nb, 1)), jnp.tile(ss, (nb, 1))
    z, xbc, dt, lat, kr, q, kcat, v = _in_proj(
        x1, cc, ss, w["win"], w["qg"], w["wuq"], w["kvg"], w["wuk"], w["wuv"], tm=tm, seq_len=L)

    y, nconv, hnew = _ssd(xbc, z, dt, conv_prev, h0, w["cw"], w["cb"], w["dtb"], w["alog"],
                          w["dsk"], w["ng"], nb=nb, seq_len=L)
    if prompt:
        o = _attn_prompt(q, kcat, v, nb=nb, seq_len=L)
    else:
        past = lat_prev.shape[1]
        kc, vc = _cache_kv(lat_prev.reshape(nb * past, KV_LORA), kr_prev.reshape(nb * past, QK_ROPE),
                           w["wuk"], w["wuv"], 512)
        o = _attn_sample(q, kc, kcat, vc, v, nb=nb, past_len=past, new_len=L)
    out = _mix_ffn_ln(x1, y, o, w["wo"], g2, b2, *f2, g3, b3, tm)
    return (out.reshape(nb, L, D_MODEL), lat.reshape(nb, L, KV_LORA), kr.reshape(nb, L, QK_ROPE),
            nconv, hnew)


def kernel(x_prompt, x_sample, cache_latent, cache_k_rope, state_conv, state_ssm, ln1_g, ln1_b, ffn1_w_gate, ffn1_w_up, ffn1_w_down, w_in, conv_w, conv_b, dt_bias, a_log, d_skip, ssd_norm_g, q_norm_g, w_uq, kv_norm_g, w_ukv, w_out, ln2_g, ln2_b, ffn2_w_gate, ffn2_w_up, ffn2_w_down, ln3_g, ln3_b):
    assert w_in.shape[0] == DEPTH == 1
    l = 0
    w = _pack_weights(w_in[l], conv_w[l], conv_b[l], dt_bias[l], a_log[l], d_skip[l],
                      ssd_norm_g[l], q_norm_g[l], w_uq[l], kv_norm_g[l], w_ukv[l], w_out[l])
    f1 = (ffn1_w_gate[l].astype(BF16), ffn1_w_up[l].astype(BF16), ffn1_w_down[l].astype(BF16))
    f2 = (ffn2_w_gate[l].astype(BF16), ffn2_w_up[l].astype(BF16), ffn2_w_down[l].astype(BF16))
    row = lambda v: v[l].reshape(1, D_MODEL)
    lns = ((row(ln1_g), row(ln1_b)), (row(ln2_g), row(ln2_b)), (row(ln3_g), row(ln3_b)))

    yp, lat_p, kr_p, conv_p, ssm_p = _stream(x_prompt, None, None, None, None, 0, w, f1, f2, lns)
    past = cache_latent.shape[2]
    ys, lat_s, kr_s, conv_s, ssm_s = _stream(x_sample, state_conv[l], state_ssm[l], cache_latent[l],
                                             cache_k_rope[l], past, w, f1, f2, lns)
    st = lambda a: a[None]
    return (yp, ys, st(lat_p), st(kr_p), st(conv_p), st(ssm_p),
            st(lat_s), st(kr_s), st(conv_s), st(ssm_s))
```

```python
import functools

import numpy as np
import jax
import jax.numpy as jnp
from jax import lax
from jax.experimental import pallas as pl
from jax.experimental.pallas import tpu as pltpu

F32 = jnp.float32
BF16 = jnp.bfloat16

D_MODEL = 1024
D_FF = 2816
CHUNK = 64
SSD_HEADS = 16
SSD_HEAD_DIM = 64
SSD_D_INNER = SSD_HEADS * SSD_HEAD_DIM
SSD_GROUPS = 2
SSD_HEADS_PER_GROUP = SSD_HEADS // SSD_GROUPS
SSD_STATE = 128
CONV_W = 4
SSD_CONV_DIM = SSD_D_INNER + 2 * SSD_GROUPS * SSD_STATE
MLA_HEADS = 8
Q_LORA = 512
KV_LORA = 512
QK_NOPE = 128
QK_ROPE = 64
V_HEAD = 128
MLA_WIDTH = MLA_HEADS * V_HEAD
ROPE_THETA = 10000.0
DEPTH = 1
ALPHA = (2 * DEPTH) ** 0.25
EPS = 1e-5
SM_SCALE = (QK_NOPE + QK_ROPE) ** -0.5
LOG2E = 1.4426950408889634
NEG_BIG = -1e30

LANES = 128
QK_PAD = 2 * LANES
SSD_Q = 128
FF_CHUNK = 256
ATT_TQ = 1024
ATT_TK = 256
VMEM_LIMIT = 56 * 1024 * 1024

_NT = (((1,), (1,)), ((), ()))


def _resident(shape):
    nd = len(shape)
    return pl.BlockSpec(shape, lambda *_: (0,) * nd, pipeline_mode=pl.Buffered(1))


def _dot(a, b):
    return jnp.dot(a, b, preferred_element_type=F32)


def _dot_nt(a, b):
    return lax.dot_general(a, b, _NT, preferred_element_type=F32)


def _sigmoid(x):
    return 1.0 / (1.0 + jnp.exp2(x * (-LOG2E)))


def _silu(x):
    return x * _sigmoid(x)


def _layer_norm(y, g, b):
    mu = jnp.mean(y, axis=-1, keepdims=True)
    d = y - mu
    var = jnp.mean(d * d, axis=-1, keepdims=True)
    return d * lax.rsqrt(var + EPS) * g + b


def _rms_norm(x, g):
    return x * lax.rsqrt(jnp.mean(x * x, axis=-1, keepdims=True) + EPS) * g


def _swiglu(xb, wg_ref, wu_ref, wd_ref):
    acc = jnp.zeros((xb.shape[0], D_MODEL), F32)
    for c in range(D_FF // FF_CHUNK):
        cs = slice(c * FF_CHUNK, (c + 1) * FF_CHUNK)
        g = _dot(xb, wg_ref[:, cs])
        u = _dot(xb, wu_ref[:, cs])
        h = (_silu(g) * u).astype(BF16)
        acc = acc + _dot(h, wd_ref[cs, :])
    return acc


def _ffn_ln_kernel(x_ref, wg_ref, wu_ref, wd_ref, g_ref, b_ref, o_ref):
    x = x_ref[...]
    ff = _swiglu(x.astype(BF16), wg_ref, wu_ref, wd_ref)
    o_ref[...] = _layer_norm(ALPHA * x + 0.5 * ff, g_ref[...], b_ref[...])


def _ffn_ln(x, wg, wu, wd, g, b, tm):
    T = x.shape[0]
    tok = pl.BlockSpec((tm, D_MODEL), lambda i: (i, 0))
    return pl.pallas_call(
        _ffn_ln_kernel,
        grid=(T // tm,),
        in_specs=[tok, _resident(wg.shape), _resident(wu.shape), _resident(wd.shape),
                  _resident(g.shape), _resident(b.shape)],
        out_specs=tok,
        out_shape=jax.ShapeDtypeStruct((T, D_MODEL), F32),
        compiler_params=pltpu.CompilerParams(dimension_semantics=("parallel",),
                                             vmem_limit_bytes=VMEM_LIMIT),
        name="ffn_ln",
    )(x, wg, wu, wd, g, b)


def _mix_ffn_ln_kernel(x1_ref, y_ref, o_ref, wo_ref, g2_ref, b2_ref,
                       wg_ref, wu_ref, wd_ref, g3_ref, b3_ref, out_ref):
    mix = _dot(y_ref[...].astype(BF16), wo_ref[:SSD_D_INNER, :])
    mix = mix + _dot(o_ref[...].astype(BF16), wo_ref[SSD_D_INNER:, :])
    x2 = _layer_norm(ALPHA * x1_ref[...] + mix, g2_ref[...], b2_ref[...])
    ff = _swiglu(x2.astype(BF16), wg_ref, wu_ref, wd_ref)
    out_ref[...] = _layer_norm(ALPHA * x2 + 0.5 * ff, g3_ref[...], b3_ref[...])


def _mix_ffn_ln(x1, y, o, wo, g2, b2, wg, wu, wd, g3, b3, tm):
    T = x1.shape[0]
    tok = pl.BlockSpec((tm, D_MODEL), lambda i: (i, 0))
    return pl.pallas_call(
        _mix_ffn_ln_kernel,
        grid=(T // tm,),
        in_specs=[tok, tok, tok, _resident(wo.shape), _resident(g2.shape), _resident(b2.shape),
                  _resident(wg.shape), _resident(wu.shape), _resident(wd.shape),
                  _resident(g3.shape), _resident(b3.shape)],
        out_specs=tok,
        out_shape=jax.ShapeDtypeStruct((T, D_MODEL), F32),
        compiler_params=pltpu.CompilerParams(dimension_semantics=("parallel",),
                                             vmem_limit_bytes=VMEM_LIMIT),
        name="mix_ffn_ln",
    )(x1, y, o, wo, g2, b2, wg, wu, wd, g3, b3)


_Z0, _XBC0, _CQ0, _CKV0, _DT0, _KR0, _WIN_COLS = 0, 1024, 2560, 3072, 3584, 3712, 3840


def _rope_tail(rs, cc, ss):
    return rs * cc + pltpu.roll(rs, 64, 1) * ss


def _k_nope_pairs(latb, wuk_ref):
    for pr in range(MLA_HEADS // 2):
        kk = _dot(latb, wuk_ref[:, pr * 2 * QK_NOPE:(pr + 1) * 2 * QK_NOPE])
        yield 2 * pr, kk[:, :QK_NOPE]
        yield 2 * pr + 1, kk[:, QK_NOPE:]


def _in_proj_kernel(x_ref, cc_ref, ss_ref, win_ref, qg_ref, wuq_ref, kvg_ref, wuk_ref, wuv_ref,
                    z_ref, xbc_ref, dt_ref, lat_ref, kr_ref, q_ref, k_ref, v_ref):
    xb = x_ref[...].astype(BF16)
    cc = cc_ref[...]
    ss = ss_ref[...]
    z_ref[...] = _dot(xb, win_ref[:, _Z0:_XBC0])
    xbc_ref[...] = _dot(xb, win_ref[:, _XBC0:_CQ0])
    dk = _dot(xb, win_ref[:, _DT0:_WIN_COLS])
    dt_ref[...] = dk[:, :LANES]

    cq = _rms_norm(_dot(xb, win_ref[:, _CQ0:_CKV0]), qg_ref[...]).astype(BF16)
    for h in range(MLA_HEADS):
        qh = _dot(cq, wuq_ref[:, h * QK_PAD:(h + 1) * QK_PAD])
        q_ref[h, :, :LANES] = qh[:, :LANES].astype(BF16)
        q_ref[h, :, LANES:] = _rope_tail(qh[:, LANES:], cc, ss).astype(BF16)

    lat = _rms_norm(_dot(xb, win_ref[:, _CKV0:_DT0]), kvg_ref[...])
    lat_ref[...] = lat
    latb = lat.astype(BF16)
    kr_tail = _rope_tail(dk[:, LANES:], cc, ss)
    kr_ref[...] = kr_tail[:, :QK_ROPE]
    kr_tail = kr_tail.astype(BF16)
    for h, k_nope in _k_nope_pairs(latb, wuk_ref):
        k_ref[h, :, :LANES] = k_nope.astype(BF16)
        k_ref[h, :, LANES:] = kr_tail
    v_ref[...] = _dot(latb, wuv_ref[...]).astype(BF16)


def _in_proj(x1, cc, ss, win, qg, wuq, kvg, wuk, wuv, *, tm, seq_len):
    T = x1.shape[0]
    nt = T // tm
    tok = lambda w: pl.BlockSpec((tm, w), lambda i: (i, 0))
    if cc.shape[0] == T:
        tab = pl.BlockSpec((tm, LANES), lambda i: (i, 0))
    else:
        per_seq = seq_len // tm
        tab = pl.BlockSpec((tm, LANES), lambda i: (i % per_seq, 0))
    head_tok = pl.BlockSpec((MLA_HEADS, tm, QK_PAD), lambda i: (0, i, 0))
    out_shape = (
        jax.ShapeDtypeStruct((T, SSD_D_INNER), F32),
        jax.ShapeDtypeStruct((T, SSD_CONV_DIM), F32),
        jax.ShapeDtypeStruct((T, LANES), F32),
        jax.ShapeDtypeStruct((T, KV_LORA), F32),
        jax.ShapeDtypeStruct((T, QK_ROPE), F32),
        jax.ShapeDtypeStruct((MLA_HEADS, T, QK_PAD), BF16),
        jax.ShapeDtypeStruct((MLA_HEADS, T, QK_PAD), BF16),
        jax.ShapeDtypeStruct((T, MLA_WIDTH), BF16),
    )
    out_specs = (tok(SSD_D_INNER), tok(SSD_CONV_DIM), tok(LANES), tok(KV_LORA), tok(QK_ROPE),
                 head_tok, head_tok, tok(MLA_WIDTH))
    return pl.pallas_call(
        _in_proj_kernel,
        grid=(nt,),
        in_specs=[tok(D_MODEL), tab, tab, _resident(win.shape), _resident(qg.shape),
                  _resident(wuq.shape), _resident(kvg.shape), _resident(wuk.shape),
                  _resident(wuv.shape)],
        out_specs=out_specs,
        out_shape=out_shape,
        compiler_params=pltpu.CompilerParams(dimension_semantics=("parallel",),
                                             vmem_limit_bytes=VMEM_LIMIT),
        name="in_proj",
    )(x1, cc, ss, win, qg, wuq, kvg, wuk, wuv)


def _cache_kv_kernel(lat_ref, kr_ref, wuk_ref, wuv_ref, k_ref, v_ref):
    latb = lat_ref[...].astype(BF16)
    kr = kr_ref[...].astype(BF16)
    zeros = jnp.zeros((kr.shape[0], LANES), BF16)
    for h, k_nope in _k_nope_pairs(latb, wuk_ref):
        k_ref[h, :, :LANES] = k_nope.astype(BF16)
        k_ref[h, :, LANES:] = zeros
        k_ref[h, :, LANES:LANES + QK_ROPE] = kr
    v_ref[...] = _dot(latb, wuv_ref[...]).astype(BF16)


def _cache_kv(lat, kr, wuk, wuv, tm):
    T = lat.shape[0]
    return pl.pallas_call(
        _cache_kv_kernel,
        grid=(T // tm,),
        in_specs=[pl.BlockSpec((tm, KV_LORA), lambda i: (i, 0)),
                  pl.BlockSpec((tm, QK_ROPE), lambda i: (i, 0)),
                  _resident(wuk.shape), _resident(wuv.shape)],
        out_specs=(pl.BlockSpec((MLA_HEADS, tm, QK_PAD), lambda i: (0, i, 0)),
                   pl.BlockSpec((tm, MLA_WIDTH), lambda i: (i, 0))),
        out_shape=(jax.ShapeDtypeStruct((MLA_HEADS, T, QK_PAD), BF16),
                   jax.ShapeDtypeStruct((T, MLA_WIDTH), BF16)),
        compiler_params=pltpu.CompilerParams(dimension_semantics=("parallel",),
                                             vmem_limit_bytes=VMEM_LIMIT),
        name="cache_kv",
    )(lat, kr, wuk, wuv)


def _split3(x):
    hi = x.astype(BF16)
    r1 = x - hi.astype(F32)
    mid = r1.astype(BF16)
    lo = (r1 - mid.astype(F32)).astype(BF16)
    return hi, mid, lo


def _softplus(x):
    return jnp.maximum(x, 0.0) + jnp.log1p(jnp.exp(-jnp.abs(x)))


def _ssd_kernel(*refs, lb, zero_init):
    if zero_init:
        (xbc_ref, z_ref, dt_ref, cw_ref, cb_ref, dtb_ref, alog_ref, dsk_ref, ng_ref,
         y_ref, nconv_ref, hnew_ref, win_ref, h_ref, dtp_ref) = refs
        cprev_ref = h0_ref = None
    else:
        (xbc_ref, z_ref, dt_ref, cprev_ref, h0_ref, cw_ref, cb_ref, dtb_ref, alog_ref, dsk_ref,
         ng_ref, y_ref, nconv_ref, hnew_ref, win_ref, h_ref, dtp_ref) = refs
    Q = SSD_Q
    j = pl.program_id(1)
    last = pl.num_programs(1) - 1

    @pl.when(j == 0)
    def _():
        win_ref[0:8, :] = jnp.zeros((8, SSD_CONV_DIM), F32)
        if zero_init:
            h_ref[...] = jnp.zeros(h_ref.shape, F32)
        else:
            win_ref[8 - (CONV_W - 1):8, :] = cprev_ref[0]
            h_ref[...] = h0_ref[0]

    win_ref[8:8 + lb, :] = xbc_ref[...]
    if lb < Q:
        win_ref[8 + lb:, :] = jnp.zeros((Q - lb, SSD_CONV_DIM), F32)
    xw = win_ref[...]
    conv = xw * cw_ref[0:1, :]
    for k in range(1, CONV_W):
        conv = pltpu.roll(conv, 1, 0) + xw * cw_ref[k:k + 1, :]
    conv = conv[8:, :] + cb_ref[...]
    nconv_tail = win_ref[8 + lb - (CONV_W - 1):8 + lb, :]
    win_ref[0:8, :] = win_ref[lb:lb + 8, :]
    xc = _silu(conv)
    xs = xc[:, :SSD_D_INNER]

    if lb < Q:
        dtp_ref[...] = jnp.zeros((Q, LANES), F32)
        dtp_ref[0:lb, :] = dt_ref[...]
        dt_raw = dtp_ref[...]
    else:
        dt_raw = dt_ref[...]
    row_id = lax.broadcasted_iota(jnp.int32, (Q, LANES), 0)
    dt_col = jnp.where(row_id < lb, _softplus(dt_raw + dtb_ref[...]), 0.0)
    da_col = dt_col * (-jnp.exp(alog_ref[...]))
    ii = lax.broadcasted_iota(jnp.int32, (Q, Q), 0)
    jj = lax.broadcasted_iota(jnp.int32, (Q, Q), 1)
    causal = ii >= jj
    tri = causal.astype(BF16)
    a_col = sum(_dot(tri, p) for p in _split3(da_col))
    a_row = a_col.T
    dt_row = dt_col.T
    e_col = jnp.exp(a_col)
    a_last = a_row[:, Q - 1:Q]
    w_row = dt_row * jnp.exp(a_last - a_row)
    e_last = jnp.exp(jnp.broadcast_to(a_last, (LANES, LANES)))

    xs_t = xs.T
    lane = lax.broadcasted_iota(jnp.int32, (Q, LANES), 1)
    lo_half = lane < SSD_HEAD_DIM
    y_parts = []
    for g in range(SSD_GROUPS):
        b_g = xc[:, SSD_D_INNER + g * SSD_STATE:SSD_D_INNER + (g + 1) * SSD_STATE].astype(BF16)
        c0 = SSD_D_INNER + SSD_GROUPS * SSD_STATE + g * SSD_STATE
        c_g = xc[:, c0:c0 + SSD_STATE].astype(BF16)
        cb = _dot_nt(c_g, b_g)
        h0 = g * SSD_HEADS_PER_GROUP
        hp = h_ref[h0:h0 + SSD_HEADS_PER_GROUP].reshape(SSD_HEADS_PER_GROUP * SSD_HEAD_DIM, SSD_STATE)
        y_off = _dot_nt(c_g, hp.astype(BF16))
        lhs = []
        for hh in range(SSD_HEADS_PER_GROUP):
            h = h0 + hh
            lhs.append(xs_t[h * SSD_HEAD_DIM:(h + 1) * SSD_HEAD_DIM, :] * w_row[h:h + 1, :])
        st = _dot(jnp.concatenate(lhs, axis=0).astype(BF16), b_g)
        for hh in range(SSD_HEADS_PER_GROUP):
            h = h0 + hh
            h_ref[h] = (e_last[h:h + 1, :] * h_ref[h]
                        + st[hh * SSD_HEAD_DIM:(hh + 1) * SSD_HEAD_DIM, :])
        for pr in range(SSD_HEADS_PER_GROUP // 2):
            ha = h0 + 2 * pr
            ms = []
            for h in (ha, ha + 1):
                seg = a_col[:, h:h + 1] - a_row[h:h + 1, :]
                m = jnp.exp(jnp.where(causal, seg, -jnp.inf)) * cb * dt_row[h:h + 1, :]
                ms.append(m.astype(BF16))
            xp = xs[:, ha * SSD_HEAD_DIM:(ha + 2) * SSD_HEAD_DIM]
            rhs = jnp.concatenate([jnp.where(lo_half, xp, 0.0), jnp.where(lo_half, 0.0, xp)],
                                  axis=0).astype(BF16)
            y_d = _dot(jnp.concatenate(ms, axis=1), rhs)
            yo = y_off[:, 2 * pr * SSD_HEAD_DIM:(2 * pr + 2) * SSD_HEAD_DIM]
            dec = jnp.where(lo_half, e_col[:, ha:ha + 1], e_col[:, ha + 1:ha + 2])
            y_parts.append(y_d + yo * dec)
    y = jnp.concatenate(y_parts, axis=1) + dsk_ref[...] * xs
    y = y[:lb] * _silu(z_ref[...])
    half = SSD_D_INNER // SSD_GROUPS
    ng = ng_ref[...]
    for g in range(SSD_GROUPS):
        cs = slice(g * half, (g + 1) * half)
        y_ref[:, cs] = _rms_norm(y[:, cs], ng[:, cs])

    @pl.when(j == last)
    def _():
        nconv_ref[0] = nconv_tail
        hnew_ref[0] = h_ref[...]


def _ssd(xbc, z, dt, cprev, h0, cw, cb, dtb, alog, dsk, ng, *, nb, seq_len):
    lb = min(SSD_Q, seq_len)
    nblk = seq_len // lb
    zero_init = cprev is None
    tokb = lambda w: pl.BlockSpec((lb, w), lambda b, j: (b * nblk + j, 0))
    in_specs = [tokb(SSD_CONV_DIM), tokb(SSD_D_INNER), tokb(LANES)]
    args = [xbc, z, dt]
    if not zero_init:
        in_specs += [pl.BlockSpec((1, CONV_W - 1, SSD_CONV_DIM), lambda b, j: (b, 0, 0)),
                     pl.BlockSpec((1, SSD_HEADS, SSD_HEAD_DIM, SSD_STATE), lambda b, j: (b, 0, 0, 0))]
        args += [cprev, h0]
    params = [cw, cb, dtb, alog, dsk, ng]
    in_specs += [pl.BlockSpec(p.shape, lambda b, j: (0, 0)) for p in params]
    args += params
    return pl.pallas_call(
        functools.partial(_ssd_kernel, lb=lb, zero_init=zero_init),
        grid=(nb, nblk),
        in_specs=in_specs,
        out_specs=(tokb(SSD_D_INNER),
                   pl.BlockSpec((1, CONV_W - 1, SSD_CONV_DIM), lambda b, j: (b, 0, 0)),
                   pl.BlockSpec((1, SSD_HEADS, SSD_HEAD_DIM, SSD_STATE), lambda b, j: (b, 0, 0, 0))),
        out_shape=(jax.ShapeDtypeStruct((nb * seq_len, SSD_D_INNER), F32),
                   jax.ShapeDtypeStruct((nb, CONV_W - 1, SSD_CONV_DIM), F32),
                   jax.ShapeDtypeStruct((nb, SSD_HEADS, SSD_HEAD_DIM, SSD_STATE), F32)),
        scratch_shapes=[pltpu.VMEM((SSD_Q + 8, SSD_CONV_DIM), F32),
                        pltpu.VMEM((SSD_HEADS, SSD_HEAD_DIM, SSD_STATE), F32),
                        pltpu.VMEM((SSD_Q, LANES), F32)],
        compiler_params=pltpu.CompilerParams(dimension_semantics=("arbitrary", "arbitrary"),
                                             vmem_limit_bytes=VMEM_LIMIT),
        name="ssd",
    )(*args)


def _lane_tile(x, width):
    return jnp.concatenate([x] * (width // LANES), axis=1)


def _attn_prompt_kernel(q_ref, k_ref, v_ref, o_ref, vx_ref, m_ref, acc_ref, *, seq_len, tq, tk):
    vx_ref[:, :V_HEAD] = v_ref[...]
    vx_ref[:, V_HEAD:] = jnp.ones((seq_len, LANES), BF16)
    ri = lax.broadcasted_iota(jnp.int32, (tk, tk), 0)
    ci = lax.broadcasted_iota(jnp.int32, (tk, tk), 1)
    diag_mask = (ci // CHUNK) <= (ri // CHUNK)

    def step(r0, k0, row0, masked):
        s = _dot_nt(q_ref[0, r0 + row0:r0 + tq, :], k_ref[0, k0:k0 + tk, :])
        if masked:
            top = jnp.where(diag_mask, s[:tk], NEG_BIG)
            s = top if tq - row0 == tk else jnp.concatenate([top, s[tk:]], axis=0)
        m_prev = m_ref[row0:, :]
        m_next = jnp.maximum(m_prev, jnp.max(s, axis=-1, keepdims=True))
        alpha = jnp.exp2(m_prev - m_next)
        p = jnp.exp2(s - _lane_tile(m_next, tk))
        m_ref[row0:, :] = m_next
        acc_ref[row0:, :] = (_lane_tile(alpha, V_HEAD + LANES) * acc_ref[row0:, :]
                             + _dot(p.astype(BF16), vx_ref[k0:k0 + tk, :]))

    for qb in range(seq_len // tq):
        r0 = qb * tq
        m_ref[...] = jnp.full(m_ref.shape, NEG_BIG, F32)
        acc_ref[...] = jnp.zeros(acc_ref.shape, F32)
        for j in range(r0 // tk):
            step(r0, j * tk, 0, False)
        for d in range(tq // tk):
            step(r0, r0 + d * tk, d * tk, True)
        o_ref[r0:r0 + tq, :] = acc_ref[:, :V_HEAD] / acc_ref[:, V_HEAD:]


def _attn_prompt(q, k, v, *, nb, seq_len):
    tq = min(ATT_TQ, seq_len)
    tk = min(ATT_TK, tq)
    scratch = [pltpu.VMEM((seq_len, V_HEAD + LANES), BF16), pltpu.VMEM((tq, LANES), F32),
               pltpu.VMEM((tq, V_HEAD + LANES), F32)]
    return pl.pallas_call(
        functools.partial(_attn_prompt_kernel, seq_len=seq_len, tq=tq, tk=tk),
        grid=(nb, MLA_HEADS),
        in_specs=[pl.BlockSpec((1, seq_len, QK_PAD), lambda b, h: (h, b, 0)),
                  pl.BlockSpec((1, seq_len, QK_PAD), lambda b, h: (h, b, 0)),
                  pl.BlockSpec((seq_len, V_HEAD), lambda b, h: (b, h))],
        out_specs=pl.BlockSpec((seq_len, V_HEAD), lambda b, h: (b, h)),
        out_shape=jax.ShapeDtypeStruct((nb * seq_len, MLA_WIDTH), F32),
        scratch_shapes=scratch,
        compiler_params=pltpu.CompilerParams(dimension_semantics=("parallel", "parallel"),
                                             vmem_limit_bytes=VMEM_LIMIT),
        name="attn_prompt",
    )(q, k, v)


def _attn_sample_kernel(q_ref, kc_ref, kn_ref, vc_ref, vn_ref, o_ref, *, past_len, new_len):
    qpos = past_len + lax.broadcasted_iota(jnp.int32, (new_len, 1), 0)
    kpos_c = lax.broadcasted_iota(jnp.int32, (1, past_len), 1)
    kpos_n = past_len + lax.broadcasted_iota(jnp.int32, (1, new_len), 1)
    mask_c = (kpos_c // CHUNK) <= (qpos // CHUNK)
    mask_n = (kpos_n // CHUNK) <= (qpos // CHUNK)
    for h in range(MLA_HEADS):
        q = q_ref[h]
        sc = jnp.where(mask_c, _dot_nt(q, kc_ref[h]), NEG_BIG)
        sn = jnp.where(mask_n, _dot_nt(q, kn_ref[h]), NEG_BIG)
        m = jnp.maximum(jnp.max(sc, axis=-1, keepdims=True), jnp.max(sn, axis=-1, keepdims=True))
        pc = jnp.exp2(sc - m)
        pn = jnp.exp2(sn - m)
        l = jnp.sum(pc, axis=-1, keepdims=True) + jnp.sum(pn, axis=-1, keepdims=True)
        hs = slice(h * V_HEAD, (h + 1) * V_HEAD)
        o = _dot(pc.astype(BF16), vc_ref[:, hs]) + _dot(pn.astype(BF16), vn_ref[:, hs])
        o_ref[:, hs] = o / l


def _attn_sample(q, kc, kn, vc, vn, *, nb, past_len, new_len):
    return pl.pallas_call(
        functools.partial(_attn_sample_kernel, past_len=past_len, new_len=new_len),
        grid=(nb,),
        in_specs=[pl.BlockSpec((MLA_HEADS, new_len, QK_PAD), lambda b: (0, b, 0)),
                  pl.BlockSpec((MLA_HEADS, past_len, QK_PAD), lambda b: (0, b, 0)),
                  pl.BlockSpec((MLA_HEADS, new_len, QK_PAD), lambda b: (0, b, 0)),
                  pl.BlockSpec((past_len, MLA_WIDTH), lambda b: (b, 0)),
                  pl.BlockSpec((new_len, MLA_WIDTH), lambda b: (b, 0))],
        out_specs=pl.BlockSpec((new_len, MLA_WIDTH), lambda b: (b, 0)),
        out_shape=jax.ShapeDtypeStruct((nb * new_len, MLA_WIDTH), F32),
        compiler_params=pltpu.CompilerParams(dimension_semantics=("parallel",),
                                             vmem_limit_bytes=VMEM_LIMIT),
        name="attn_sample",
    )(q, kc, kn, vc, vn)


def _rope_tables(pos):
    inv = ROPE_THETA ** (-jnp.arange(0, QK_ROPE, 2, dtype=F32) / QK_ROPE)
    ang = pos.astype(F32)[:, None] * inv[None, :]
    cos, sin = jnp.cos(ang), jnp.sin(ang)
    pad = jnp.zeros((pos.shape[0], LANES - QK_ROPE), F32)
    return (jnp.concatenate([cos, cos, pad], axis=1), jnp.concatenate([-sin, sin, pad], axis=1))


def _pack_weights(w_in, conv_w, conv_b, dt_bias, a_log, d_skip, ssd_norm_g, q_norm_g, w_uq,
                  kv_norm_g, w_ukv, w_out):
    swap = np.concatenate([np.arange(QK_ROPE // 2, QK_ROPE), np.arange(QK_ROPE // 2)])
    i0 = SSD_D_INNER
    i1 = i0 + SSD_CONV_DIM
    i2 = i1 + SSD_HEADS
    i3 = i2 + Q_LORA
    i4 = i3 + KV_LORA
    w_dt = jnp.pad(w_in[:, i1:i2], ((0, 0), (0, LANES - SSD_HEADS)))
    w_kr = w_in[:, i4:]
    win = jnp.concatenate([w_in[:, :i0], w_in[:, i0:i1], w_in[:, i2:i3], w_in[:, i3:i4], w_dt,
                           w_kr, w_kr[:, swap]], axis=1).astype(BF16)
    uq = (w_uq * (SM_SCALE * LOG2E)).reshape(Q_LORA, MLA_HEADS, QK_NOPE + QK_ROPE)
    wuq = jnp.concatenate([uq, uq[:, :, QK_NOPE:][:, :, swap]], axis=2)
    wuq = wuq.reshape(Q_LORA, MLA_HEADS * QK_PAD).astype(BF16)
    ukv = w_ukv.reshape(KV_LORA, MLA_HEADS, QK_NOPE + V_HEAD)
    wuk = ukv[:, :, :QK_NOPE].reshape(KV_LORA, MLA_HEADS * QK_NOPE).astype(BF16)
    wuv = ukv[:, :, QK_NOPE:].reshape(KV_LORA, MLA_WIDTH).astype(BF16)
    lane_pad = lambda v: jnp.pad(v, (0, LANES - SSD_HEADS)).reshape(1, LANES)
    return dict(
        win=win, wuq=wuq, wuk=wuk, wuv=wuv,
        qg=q_norm_g.reshape(1, Q_LORA), kvg=kv_norm_g.reshape(1, KV_LORA),
        cw=conv_w, cb=conv_b.reshape(1, SSD_CONV_DIM),
        dtb=lane_pad(dt_bias), alog=lane_pad(a_log),
        dsk=jnp.repeat(d_skip, SSD_HEAD_DIM).reshape(1, SSD_D_INNER),
        ng=ssd_norm_g.reshape(1, SSD_D_INNER),
        wo=w_out.astype(BF16),
    )


def _stream(x, conv_prev, h0, lat_prev, kr_prev, pos0, w, f1, f2, lns):
    nb, L, _ = x.shape
    T = nb * L
    tm = min(512, T)
    x2d = x.reshape(T, D_MODEL)
    (g1, b1), (g2, b2), (g3, b3) = lns
    x1 = _ffn_ln(x2d, *f1, g1, b1, tm)

    prompt = lat_prev is None
    cc, ss = _rope_tables(pos0 + jnp.arange(L))
    if L % tm != 0:
        cc, ss = jnp.tile(cc, (nb, 1)), jnp.tile(ss, (nb, 1))
    z, xbc, dt, lat, kr, q, kcat, v = _in_proj(
        x1, cc, ss, w["win"], w["qg"], w["wuq"], w["kvg"], w["wuk"], w["wuv"], tm=tm, seq_len=L)

    y, nconv, hnew = _ssd(xbc, z, dt, conv_prev, h0, w["cw"], w["cb"], w["dtb"], w["alog"],
                          w["dsk"], w["ng"], nb=nb, seq_len=L)
    if prompt:
        o = _attn_prompt(q, kcat, v, nb=nb, seq_len=L)
    else:
        past = lat_prev.shape[1]
        kc, vc = _cache_kv(lat_prev.reshape(nb * past, KV_LORA), kr_prev.reshape(nb * past, QK_ROPE),
                           w["wuk"], w["wuv"], 512)
        o = _attn_sample(q, kc, kcat, vc, v, nb=nb, past_len=past, new_len=L)
    out = _mix_ffn_ln(x1, y, o, w["wo"], g2, b2, *f2, g3, b3, tm)
    return (out.reshape(nb, L, D_MODEL), lat.reshape(nb, L, KV_LORA), kr.reshape(nb, L, QK_ROPE),
            nconv, hnew)


def kernel(x_prompt, x_sample, cache_latent, cache_k_rope, state_conv, state_ssm, ln1_g, ln1_b, ffn1_w_gate, ffn1_w_up, ffn1_w_down, w_in, conv_w, conv_b, dt_bias, a_log, d_skip, ssd_norm_g, q_norm_g, w_uq, kv_norm_g, w_ukv, w_out, ln2_g, ln2_b, ffn2_w_gate, ffn2_w_up, ffn2_w_down, ln3_g, ln3_b):
    assert w_in.shape[0] == DEPTH == 1
    l = 0
    w = _pack_weights(w_in[l], conv_w[l], conv_b[l], dt_bias[l], a_log[l], d_skip[l],
                      ssd_norm_g[l], q_norm_g[l], w_uq[l], kv_norm_g[l], w_ukv[l], w_out[l])
    f1 = (ffn1_w_gate[l].astype(BF16), ffn1_w_up[l].astype(BF16), ffn1_w_down[l].astype(BF16))
    f2 = (ffn2_w_gate[l].astype(BF16), ffn2_w_up[l].astype(BF16), ffn2_w_down[l].astype(BF16))
    row = lambda v: v[l].reshape(1, D_MODEL)
    lns = ((row(ln1_g), row(ln1_b)), (row(ln2_g), row(ln2_b)), (row(ln3_g), row(ln3_b)))

    yp, lat_p, kr_p, conv_p, ssm_p = _stream(x_prompt, None, None, None, None, 0, w, f1, f2, lns)
    past = cache_latent.shape[2]
    ys, lat_s, kr_s, conv_s, ssm_s = _stream(x_sample, state_conv[l], state_ssm[l], cache_latent[l],
                                             cache_k_rope[l], past, w, f1, f2, lns)
    st = lambda a: a[None]
    return (yp, ys, st(lat_p), st(kr_p), st(conv_p), st(ssm_p),
            st(lat_s), st(kr_s), st(conv_s), st(ssm_s))
```

```python
import functools

import numpy as np
import jax
import jax.numpy as jnp
from jax import lax
from jax.experimental import pallas as pl
from jax.experimental.pallas import tpu as pltpu

F32 = jnp.float32
BF16 = jnp.bfloat16

D_MODEL = 1024
D_FF = 2816
CHUNK = 64
SSD_HEADS = 16
SSD_HEAD_DIM = 64
SSD_D_INNER = SSD_HEADS * SSD_HEAD_DIM
SSD_GROUPS = 2
SSD_HEADS_PER_GROUP = SSD_HEADS // SSD_GROUPS
SSD_STATE = 128
CONV_W = 4
SSD_CONV_DIM = SSD_D_INNER + 2 * SSD_GROUPS * SSD_STATE
MLA_HEADS = 8
Q_LORA = 512
KV_LORA = 512
QK_NOPE = 128
QK_ROPE = 64
V_HEAD = 128
MLA_WIDTH = MLA_HEADS * V_HEAD
ROPE_THETA = 10000.0
DEPTH = 1
ALPHA = (2 * DEPTH) ** 0.25
EPS = 1e-5
SM_SCALE = (QK_NOPE + QK_ROPE) ** -0.5
LOG2E = 1.4426950408889634
NEG_BIG = -1e30

LANES = 128
QK_PAD = 2 * LANES
SSD_Q = 128
FF_CHUNK = 256
ATT_TQ = 1024
ATT_TK = 256
VMEM_LIMIT = 56 * 1024 * 1024

_NT = (((1,), (1,)), ((), ()))


def _resident(shape):
    nd = len(shape)
    return pl.BlockSpec(shape, lambda *_: (0,) * nd, pipeline_mode=pl.Buffered(1))


def _dot(a, b):
    return jnp.dot(a, b, preferred_element_type=F32)


def _dot_nt(a, b):
    return lax.dot_general(a, b, _NT, preferred_element_type=F32)


def _sigmoid(x):
    return 1.0 / (1.0 + jnp.exp2(x * (-LOG2E)))


def _silu(x):
    return x * _sigmoid(x)


def _layer_norm(y, g, b):
    mu = jnp.mean(y, axis=-1, keepdims=True)
    d = y - mu
    var = jnp.mean(d * d, axis=-1, keepdims=True)
    return d * lax.rsqrt(var + EPS) * g + b


def _rms_norm(x, g):
    return x * lax.rsqrt(jnp.mean(x * x, axis=-1, keepdims=True) + EPS) * g


def _swiglu(xb, wg_ref, wu_ref, wd_ref):
    acc = jnp.zeros((xb.shape[0], D_MODEL), F32)
    for c in range(D_FF // FF_CHUNK):
        cs = slice(c * FF_CHUNK, (c + 1) * FF_CHUNK)
        g = _dot(xb, wg_ref[:, cs])
        u = _dot(xb, wu_ref[:, cs])
        h = (_silu(g) * u).astype(BF16)
        acc = acc + _dot(h, wd_ref[cs, :])
    return acc


def _ffn_ln_kernel(x_ref, wg_ref, wu_ref, wd_ref, g_ref, b_ref, o_ref):
    x = x_ref[...]
    ff = _swiglu(x.astype(BF16), wg_ref, wu_ref, wd_ref)
    o_ref[...] = _layer_norm(ALPHA * x + 0.5 * ff, g_ref[...], b_ref[...])


def _ffn_ln(x, wg, wu, wd, g, b, tm):
    T = x.shape[0]
    tok = pl.BlockSpec((tm, D_MODEL), lambda i: (i, 0))
    return pl.pallas_call(
        _ffn_ln_kernel,
        grid=(T // tm,),
        in_specs=[tok, _resident(wg.shape), _resident(wu.shape), _resident(wd.shape),
                  _resident(g.shape), _resident(b.shape)],
        out_specs=tok,
        out_shape=jax.ShapeDtypeStruct((T, D_MODEL), F32),
        compiler_params=pltpu.CompilerParams(dimension_semantics=("parallel",),
                                             vmem_limit_bytes=VMEM_LIMIT),
        name="ffn_ln",
    )(x, wg, wu, wd, g, b)


def _mix_ffn_ln_kernel(x1_ref, y_ref, o_ref, *refs, absorbed):
    if absorbed:
        wuv_ref, *refs = refs
        o = jnp.concatenate(
            [_dot(o_ref[h], wuv_ref[:, h * V_HEAD:(h + 1) * V_HEAD]) for h in range(MLA_HEADS)], axis=1)
    else:
        o = o_ref[...]
    wo_ref, g2_ref, b2_ref, wg_ref, wu_ref, wd_ref, g3_ref, b3_ref, out_ref = refs
    mix = _dot(y_ref[...].astype(BF16), wo_ref[:SSD_D_INNER, :])
    mix = mix + _dot(o.astype(BF16), wo_ref[SSD_D_INNER:, :])
    x2 = _layer_norm(ALPHA * x1_ref[...] + mix, g2_ref[...], b2_ref[...])
    ff = _swiglu(x2.astype(BF16), wg_ref, wu_ref, wd_ref)
    out_ref[...] = _layer_norm(ALPHA * x2 + 0.5 * ff, g3_ref[...], b3_ref[...])


def _mix_ffn_ln(x1, y, o, wuv, wo, g2, b2, wg, wu, wd, g3, b3, tm):
    T = x1.shape[0]
    tok = pl.BlockSpec((tm, D_MODEL), lambda i: (i, 0))
    absorbed = wuv is not None
    o_spec = pl.BlockSpec((MLA_HEADS, tm, KV_LORA), lambda i: (0, i, 0)) if absorbed else tok
    weights = ([wuv] if absorbed else []) + [wo, g2, b2, wg, wu, wd, g3, b3]
    return pl.pallas_call(
        functools.partial(_mix_ffn_ln_kernel, absorbed=absorbed),
        grid=(T // tm,),
        in_specs=[tok, tok, o_spec] + [_resident(w.shape) for w in weights],
        out_specs=tok,
        out_shape=jax.ShapeDtypeStruct((T, D_MODEL), F32),
        compiler_params=pltpu.CompilerParams(dimension_semantics=("parallel",),
                                             vmem_limit_bytes=VMEM_LIMIT),
        name="mix_ffn_ln",
    )(x1, y, o, *weights)


_Z0, _XBC0, _CQ0, _CKV0, _DT0, _KR0, _WIN_COLS = 0, 1024, 2560, 3072, 3584, 3712, 3840


def _rope_tail(rs, cc, ss):
    return rs * cc + pltpu.roll(rs, 64, 1) * ss


def _k_nope_pairs(latb, wuk_ref):
    for pr in range(MLA_HEADS // 2):
        kk = _dot(latb, wuk_ref[:, pr * 2 * QK_NOPE:(pr + 1) * 2 * QK_NOPE])
        yield 2 * pr, kk[:, :QK_NOPE]
        yield 2 * pr + 1, kk[:, QK_NOPE:]


def _in_proj_kernel(x_ref, cc_ref, ss_ref, win_ref, qg_ref, wuq_ref, kvg_ref, wuk_ref, *refs, absorbed):
    if absorbed:
        z_ref, xbc_ref, dt_ref, lat_ref, kr_ref, q_ref = refs
    else:
        wuv_ref, z_ref, xbc_ref, dt_ref, lat_ref, kr_ref, q_ref, k_ref, v_ref = refs
    xb = x_ref[...].astype(BF16)
    cc = cc_ref[...]
    ss = ss_ref[...]
    z_ref[...] = _dot(xb, win_ref[:, _Z0:_XBC0])
    xbc_ref[...] = _dot(xb, win_ref[:, _XBC0:_CQ0])
    dk = _dot(xb, win_ref[:, _DT0:_WIN_COLS])
    dt_ref[...] = dk[:, :LANES]

    cq = _rms_norm(_dot(xb, win_ref[:, _CQ0:_CKV0]), qg_ref[...]).astype(BF16)
    for h in range(MLA_HEADS):
        qh = _dot(cq, wuq_ref[:, h * QK_PAD:(h + 1) * QK_PAD])
        q_nope = qh[:, :LANES].astype(BF16)
        if absorbed:
            q_nope = _dot_nt(q_nope, wuk_ref[:, h * QK_NOPE:(h + 1) * QK_NOPE]).astype(BF16)
        q_ref[h, :, :q_nope.shape[1]] = q_nope
        q_ref[h, :, q_nope.shape[1]:] = _rope_tail(qh[:, LANES:], cc, ss).astype(BF16)

    lat = _rms_norm(_dot(xb, win_ref[:, _CKV0:_DT0]), kvg_ref[...])
    lat_ref[...] = lat
    kr_tail = _rope_tail(dk[:, LANES:], cc, ss)
    kr_ref[...] = kr_tail[:, :QK_ROPE]
    if absorbed:
        return
    latb = lat.astype(BF16)
    kr_tail = kr_tail.astype(BF16)
    for h, k_nope in _k_nope_pairs(latb, wuk_ref):
        k_ref[h, :, :LANES] = k_nope.astype(BF16)
        k_ref[h, :, LANES:] = kr_tail
    v_ref[...] = _dot(latb, wuv_ref[...]).astype(BF16)


def _in_proj(x1, cc, ss, win, qg, wuq, kvg, wuk, wuv, *, tm, seq_len, absorbed):
    T = x1.shape[0]
    nt = T // tm
    tok = lambda w: pl.BlockSpec((tm, w), lambda i: (i, 0))
    if cc.shape[0] == T:
        tab = pl.BlockSpec((tm, LANES), lambda i: (i, 0))
    else:
        per_seq = seq_len // tm
        tab = pl.BlockSpec((tm, LANES), lambda i: (i % per_seq, 0))
    q_width = KV_LORA + LANES if absorbed else QK_PAD
    head_tok = lambda w: pl.BlockSpec((MLA_HEADS, tm, w), lambda i: (0, i, 0))
    head_shape = lambda w: jax.ShapeDtypeStruct((MLA_HEADS, T, w), BF16)
    out_shape = [
        jax.ShapeDtypeStruct((T, SSD_D_INNER), F32),
        jax.ShapeDtypeStruct((T, SSD_CONV_DIM), F32),
        jax.ShapeDtypeStruct((T, LANES), F32),
        jax.ShapeDtypeStruct((T, KV_LORA), F32),
        jax.ShapeDtypeStruct((T, QK_ROPE), F32),
        head_shape(q_width),
    ]
    out_specs = [tok(SSD_D_INNER), tok(SSD_CONV_DIM), tok(LANES), tok(KV_LORA), tok(QK_ROPE),
                 head_tok(q_width)]
    weights = [win, qg, wuq, kvg, wuk]
    if not absorbed:
        weights.append(wuv)
        out_shape += [head_shape(QK_PAD), jax.ShapeDtypeStruct((T, MLA_WIDTH), BF16)]
        out_specs += [head_tok(QK_PAD), tok(MLA_WIDTH)]
    return pl.pallas_call(
        functools.partial(_in_proj_kernel, absorbed=absorbed),
        grid=(nt,),
        in_specs=[tok(D_MODEL), tab, tab] + [_resident(w.shape) for w in weights],
        out_specs=out_specs,
        out_shape=out_shape,
        compiler_params=pltpu.CompilerParams(dimension_semantics=("parallel",),
                                             vmem_limit_bytes=VMEM_LIMIT),
        name="in_proj",
    )(x1, cc, ss, *weights)


def _split3(x):
    hi = x.astype(BF16)
    r1 = x - hi.astype(F32)
    mid = r1.astype(BF16)
    lo = (r1 - mid.astype(F32)).astype(BF16)
    return hi, mid, lo


def _softplus(x):
    return jnp.maximum(x, 0.0) + jnp.log1p(jnp.exp(-jnp.abs(x)))


def _ssd_kernel(*refs, lb, zero_init):
    if zero_init:
        (xbc_ref, z_ref, dt_ref, cw_ref, cb_ref, dtb_ref, alog_ref, dsk_ref, ng_ref,
         y_ref, nconv_ref, hnew_ref, win_ref, h_ref, dtp_ref) = refs
        cprev_ref = h0_ref = None
    else:
        (xbc_ref, z_ref, dt_ref, cprev_ref, h0_ref, cw_ref, cb_ref, dtb_ref, alog_ref, dsk_ref,
         ng_ref, y_ref, nconv_ref, hnew_ref, win_ref, h_ref, dtp_ref) = refs
    Q = SSD_Q
    j = pl.program_id(1)
    last = pl.num_programs(1) - 1

    @pl.when(j == 0)
    def _():
        win_ref[0:8, :] = jnp.zeros((8, SSD_CONV_DIM), F32)
        if zero_init:
            h_ref[...] = jnp.zeros(h_ref.shape, F32)
        else:
            win_ref[8 - (CONV_W - 1):8, :] = cprev_ref[0]
            h_ref[...] = h0_ref[0]

    win_ref[8:8 + lb, :] = xbc_ref[...]
    if lb < Q:
        win_ref[8 + lb:, :] = jnp.zeros((Q - lb, SSD_CONV_DIM), F32)
    xw = win_ref[...]
    conv = xw * cw_ref[0:1, :]
    for k in range(1, CONV_W):
        conv = pltpu.roll(conv, 1, 0) + xw * cw_ref[k:k + 1, :]
    conv = conv[8:, :] + cb_ref[...]
    nconv_tail = win_ref[8 + lb - (CONV_W - 1):8 + lb, :]
    win_ref[0:8, :] = win_ref[lb:lb + 8, :]
    xc = _silu(conv)
    xs = xc[:, :SSD_D_INNER]

    if lb < Q:
        dtp_ref[...] = jnp.zeros((Q, LANES), F32)
        dtp_ref[0:lb, :] = dt_ref[...]
        dt_raw = dtp_ref[...]
    else:
        dt_raw = dt_ref[...]
    row_id = lax.broadcasted_iota(jnp.int32, (Q, LANES), 0)
    dt_col = jnp.where(row_id < lb, _softplus(dt_raw + dtb_ref[...]), 0.0)
    da_col = dt_col * (-jnp.exp(alog_ref[...]))
    ii = lax.broadcasted_iota(jnp.int32, (Q, Q), 0)
    jj = lax.broadcasted_iota(jnp.int32, (Q, Q), 1)
    causal = ii >= jj
    tri = causal.astype(BF16)
    a_col = sum(_dot(tri, p) for p in _split3(da_col))
    a_row = a_col.T
    dt_row = dt_col.T
    e_col = jnp.exp(a_col)
    a_last = a_row[:, Q - 1:Q]
    w_row = dt_row * jnp.exp(a_last - a_row)
    e_last = jnp.exp(jnp.broadcast_to(a_last, (LANES, LANES)))

    xs_t = xs.T
    lane = lax.broadcasted_iota(jnp.int32, (Q, LANES), 1)
    lo_half = lane < SSD_HEAD_DIM
    y_parts = []
    for g in range(SSD_GROUPS):
        b_g = xc[:, SSD_D_INNER + g * SSD_STATE:SSD_D_INNER + (g + 1) * SSD_STATE].astype(BF16)
        c0 = SSD_D_INNER + SSD_GROUPS * SSD_STATE + g * SSD_STATE
        c_g = xc[:, c0:c0 + SSD_STATE].astype(BF16)
        cb = _dot_nt(c_g, b_g)
        h0 = g * SSD_HEADS_PER_GROUP
        hp = h_ref[h0:h0 + SSD_HEADS_PER_GROUP].reshape(SSD_HEADS_PER_GROUP * SSD_HEAD_DIM, SSD_STATE)
        y_off = _dot_nt(c_g, hp.astype(BF16))
        lhs = []
        for hh in range(SSD_HEADS_PER_GROUP):
            h = h0 + hh
            lhs.append(xs_t[h * SSD_HEAD_DIM:(h + 1) * SSD_HEAD_DIM, :] * w_row[h:h + 1, :])
        st = _dot(jnp.concatenate(lhs, axis=0).astype(BF16), b_g)
        for hh in range(SSD_HEADS_PER_GROUP):
            h = h0 + hh
            h_ref[h] = (e_last[h:h + 1, :] * h_ref[h]
                        + st[hh * SSD_HEAD_DIM:(hh + 1) * SSD_HEAD_DIM, :])
        for pr in range(SSD_HEADS_PER_GROUP // 2):
            ha = h0 + 2 * pr
            ms = []
            for h in (ha, ha + 1):
                seg = a_col[:, h:h + 1] - a_row[h:h + 1, :]
                m = jnp.exp(jnp.where(causal, seg, -jnp.inf)) * cb * dt_row[h:h + 1, :]
                ms.append(m.astype(BF16))
            xp = xs[:, ha * SSD_HEAD_DIM:(ha + 2) * SSD_HEAD_DIM]
            rhs = jnp.concatenate([jnp.where(lo_half, xp, 0.0), jnp.where(lo_half, 0.0, xp)],
                                  axis=0).astype(BF16)
            y_d = _dot(jnp.concatenate(ms, axis=1), rhs)
            yo = y_off[:, 2 * pr * SSD_HEAD_DIM:(2 * pr + 2) * SSD_HEAD_DIM]
            dec = jnp.where(lo_half, e_col[:, ha:ha + 1], e_col[:, ha + 1:ha + 2])
            y_parts.append(y_d + yo * dec)
    y = jnp.concatenate(y_parts, axis=1) + dsk_ref[...] * xs
    y = y[:lb] * _silu(z_ref[...])
    half = SSD_D_INNER // SSD_GROUPS
    ng = ng_ref[...]
    for g in range(SSD_GROUPS):
        cs = slice(g * half, (g + 1) * half)
        y_ref[:, cs] = _rms_norm(y[:, cs], ng[:, cs])

    @pl.when(j == last)
    def _():
        nconv_ref[0] = nconv_tail
        hnew_ref[0] = h_ref[...]


def _ssd(xbc, z, dt, cprev, h0, cw, cb, dtb, alog, dsk, ng, *, nb, seq_len):
    lb = min(SSD_Q, seq_len)
    nblk = seq_len // lb
    zero_init = cprev is None
    tokb = lambda w: pl.BlockSpec((lb, w), lambda b, j: (b * nblk + j, 0))
    in_specs = [tokb(SSD_CONV_DIM), tokb(SSD_D_INNER), tokb(LANES)]
    args = [xbc, z, dt]
    if not zero_init:
        in_specs += [pl.BlockSpec((1, CONV_W - 1, SSD_CONV_DIM), lambda b, j: (b, 0, 0)),
                     pl.BlockSpec((1, SSD_HEADS, SSD_HEAD_DIM, SSD_STATE), lambda b, j: (b, 0, 0, 0))]
        args += [cprev, h0]
    params = [cw, cb, dtb, alog, dsk, ng]
    in_specs += [pl.BlockSpec(p.shape, lambda b, j: (0, 0)) for p in params]
    args += params
    return pl.pallas_call(
        functools.partial(_ssd_kernel, lb=lb, zero_init=zero_init),
        grid=(nb, nblk),
        in_specs=in_specs,
        out_specs=(tokb(SSD_D_INNER),
                   pl.BlockSpec((1, CONV_W - 1, SSD_CONV_DIM), lambda b, j: (b, 0, 0)),
                   pl.BlockSpec((1, SSD_HEADS, SSD_HEAD_DIM, SSD_STATE), lambda b, j: (b, 0, 0, 0))),
        out_shape=(jax.ShapeDtypeStruct((nb * seq_len, SSD_D_INNER), F32),
                   jax.ShapeDtypeStruct((nb, CONV_W - 1, SSD_CONV_DIM), F32),
                   jax.ShapeDtypeStruct((nb, SSD_HEADS, SSD_HEAD_DIM, SSD_STATE), F32)),
        scratch_shapes=[pltpu.VMEM((SSD_Q + 8, SSD_CONV_DIM), F32),
                        pltpu.VMEM((SSD_HEADS, SSD_HEAD_DIM, SSD_STATE), F32),
                        pltpu.VMEM((SSD_Q, LANES), F32)],
        compiler_params=pltpu.CompilerParams(dimension_semantics=("arbitrary", "arbitrary"),
                                             vmem_limit_bytes=VMEM_LIMIT),
        name="ssd",
    )(*args)


def _lane_tile(x, width):
    return jnp.concatenate([x] * (width // LANES), axis=1)


def _attn_prompt_kernel(q_ref, k_ref, v_ref, o_ref, vx_ref, m_ref, acc_ref, *, seq_len, tq, tk):
    vx_ref[:, :V_HEAD] = v_ref[...]
    vx_ref[:, V_HEAD:] = jnp.ones((seq_len, LANES), BF16)
    ri = lax.broadcasted_iota(jnp.int32, (tk, tk), 0)
    ci = lax.broadcasted_iota(jnp.int32, (tk, tk), 1)
    diag_mask = (ci // CHUNK) <= (ri // CHUNK)

    def step(r0, k0, row0, masked):
        s = _dot_nt(q_ref[0, r0 + row0:r0 + tq, :], k_ref[0, k0:k0 + tk, :])
        if masked:
            top = jnp.where(diag_mask, s[:tk], NEG_BIG)
            s = top if tq - row0 == tk else jnp.concatenate([top, s[tk:]], axis=0)
        m_prev = m_ref[row0:, :]
        m_next = jnp.maximum(m_prev, jnp.max(s, axis=-1, keepdims=True))
        alpha = jnp.exp2(m_prev - m_next)
        p = jnp.exp2(s - _lane_tile(m_next, tk))
        m_ref[row0:, :] = m_next
        acc_ref[row0:, :] = (_lane_tile(alpha, V_HEAD + LANES) * acc_ref[row0:, :]
                             + _dot(p.astype(BF16), vx_ref[k0:k0 + tk, :]))

    for qb in range(seq_len // tq):
        r0 = qb * tq
        m_ref[...] = jnp.full(m_ref.shape, NEG_BIG, F32)
        acc_ref[...] = jnp.zeros(acc_ref.shape, F32)
        for j in range(r0 // tk):
            step(r0, j * tk, 0, False)
        for d in range(tq // tk):
            step(r0, r0 + d * tk, d * tk, True)
        o_ref[r0:r0 + tq, :] = acc_ref[:, :V_HEAD] / acc_ref[:, V_HEAD:]


def _attn_prompt(q, k, v, *, nb, seq_len):
    tq = min(ATT_TQ, seq_len)
    tk = min(ATT_TK, tq)
    scratch = [pltpu.VMEM((seq_len, V_HEAD + LANES), BF16), pltpu.VMEM((tq, LANES), F32),
               pltpu.VMEM((tq, V_HEAD + LANES), F32)]
    return pl.pallas_call(
        functools.partial(_attn_prompt_kernel, seq_len=seq_len, tq=tq, tk=tk),
        grid=(nb, MLA_HEADS),
        in_specs=[pl.BlockSpec((1, seq_len, QK_PAD), lambda b, h: (h, b, 0)),
                  pl.BlockSpec((1, seq_len, QK_PAD), lambda b, h: (h, b, 0)),
                  pl.BlockSpec((seq_len, V_HEAD), lambda b, h: (b, h))],
        out_specs=pl.BlockSpec((seq_len, V_HEAD), lambda b, h: (b, h)),
        out_shape=jax.ShapeDtypeStruct((nb * seq_len, MLA_WIDTH), F32),
        scratch_shapes=scratch,
        compiler_params=pltpu.CompilerParams(dimension_semantics=("parallel", "parallel"),
                                             vmem_limit_bytes=VMEM_LIMIT),
        name="attn_prompt",
    )(q, k, v)


def _attn_sample_kernel(q_ref, latc_ref, krc_ref, latn_ref, krn_ref, o_ref, *, past_len, new_len):
    rows = MLA_HEADS * new_len
    qa = jnp.concatenate([q_ref[h] for h in range(MLA_HEADS)], axis=0)
    q_lat = qa[:, :KV_LORA]
    q_rope = qa[:, KV_LORA:KV_LORA + QK_ROPE]
    latc = latc_ref[...].astype(BF16)
    latn = latn_ref[...].astype(BF16)
    sc = _dot_nt(q_lat, latc) + _dot_nt(q_rope, krc_ref[...].astype(BF16))
    sn = _dot_nt(q_lat, latn) + _dot_nt(q_rope, krn_ref[...].astype(BF16))
    qpos = past_len + lax.broadcasted_iota(jnp.int32, (rows, 1), 0) % new_len
    kpos_c = lax.broadcasted_iota(jnp.int32, (1, past_len), 1)
    kpos_n = past_len + lax.broadcasted_iota(jnp.int32, (1, new_len), 1)
    sc = jnp.where((kpos_c // CHUNK) <= (qpos // CHUNK), sc, NEG_BIG)
    sn = jnp.where((kpos_n // CHUNK) <= (qpos // CHUNK), sn, NEG_BIG)
    m = jnp.maximum(jnp.max(sc, axis=-1, keepdims=True), jnp.max(sn, axis=-1, keepdims=True))
    pc = jnp.exp2(sc - m)
    pn = jnp.exp2(sn - m)
    l = jnp.sum(pc, axis=-1, keepdims=True) + jnp.sum(pn, axis=-1, keepdims=True)
    o = (_dot(pc.astype(BF16), latc) + _dot(pn.astype(BF16), latn)) / l
    for h in range(MLA_HEADS):
        o_ref[h] = o[h * new_len:(h + 1) * new_len].astype(BF16)


def _attn_sample(qa, lat_c, kr_c, lat_n, kr_n, *, nb, past_len, new_len):
    return pl.pallas_call(
        functools.partial(_attn_sample_kernel, past_len=past_len, new_len=new_len),
        grid=(nb,),
        in_specs=[pl.BlockSpec((MLA_HEADS, new_len, KV_LORA + LANES), lambda b: (0, b, 0)),
                  pl.BlockSpec((past_len, KV_LORA), lambda b: (b, 0)),
                  pl.BlockSpec((past_len, QK_ROPE), lambda b: (b, 0)),
                  pl.BlockSpec((new_len, KV_LORA), lambda b: (b, 0)),
                  pl.BlockSpec((new_len, QK_ROPE), lambda b: (b, 0))],
        out_specs=pl.BlockSpec((MLA_HEADS, new_len, KV_LORA), lambda b: (0, b, 0)),
        out_shape=jax.ShapeDtypeStruct((MLA_HEADS, nb * new_len, KV_LORA), BF16),
        compiler_params=pltpu.CompilerParams(dimension_semantics=("parallel",),
                                             vmem_limit_bytes=VMEM_LIMIT),
        name="attn_sample",
    )(qa, lat_c, kr_c, lat_n, kr_n)


def _rope_tables(pos):
    inv = ROPE_THETA ** (-jnp.arange(0, QK_ROPE, 2, dtype=F32) / QK_ROPE)
    ang = pos.astype(F32)[:, None] * inv[None, :]
    cos, sin = jnp.cos(ang), jnp.sin(ang)
    pad = jnp.zeros((pos.shape[0], LANES - QK_ROPE), F32)
    return (jnp.concatenate([cos, cos, pad], axis=1), jnp.concatenate([-sin, sin, pad], axis=1))


def _pack_weights(w_in, conv_w, conv_b, dt_bias, a_log, d_skip, ssd_norm_g, q_norm_g, w_uq,
                  kv_norm_g, w_ukv, w_out):
    swap = np.concatenate([np.arange(QK_ROPE // 2, QK_ROPE), np.arange(QK_ROPE // 2)])
    i0 = SSD_D_INNER
    i1 = i0 + SSD_CONV_DIM
    i2 = i1 + SSD_HEADS
    i3 = i2 + Q_LORA
    i4 = i3 + KV_LORA
    w_dt = jnp.pad(w_in[:, i1:i2], ((0, 0), (0, LANES - SSD_HEADS)))
    w_kr = w_in[:, i4:]
    win = jnp.concatenate([w_in[:, :i0], w_in[:, i0:i1], w_in[:, i2:i3], w_in[:, i3:i4], w_dt,
                           w_kr, w_kr[:, swap]], axis=1).astype(BF16)
    uq = (w_uq * (SM_SCALE * LOG2E)).reshape(Q_LORA, MLA_HEADS, QK_NOPE + QK_ROPE)
    wuq = jnp.concatenate([uq, uq[:, :, QK_NOPE:][:, :, swap]], axis=2)
    wuq = wuq.reshape(Q_LORA, MLA_HEADS * QK_PAD).astype(BF16)
    ukv = w_ukv.reshape(KV_LORA, MLA_HEADS, QK_NOPE + V_HEAD)
    wuk = ukv[:, :, :QK_NOPE].reshape(KV_LORA, MLA_HEADS * QK_NOPE).astype(BF16)
    wuv = ukv[:, :, QK_NOPE:].reshape(KV_LORA, MLA_WIDTH).astype(BF16)
    lane_pad = lambda v: jnp.pad(v, (0, LANES - SSD_HEADS)).reshape(1, LANES)
    return dict(
        win=win, wuq=wuq, wuk=wuk, wuv=wuv,
        qg=q_norm_g.reshape(1, Q_LORA), kvg=kv_norm_g.reshape(1, KV_LORA),
        cw=conv_w, cb=conv_b.reshape(1, SSD_CONV_DIM),
        dtb=lane_pad(dt_bias), alog=lane_pad(a_log),
        dsk=jnp.repeat(d_skip, SSD_HEAD_DIM).reshape(1, SSD_D_INNER),
        ng=ssd_norm_g.reshape(1, SSD_D_INNER),
        wo=w_out.astype(BF16),
    )


def _stream(x, conv_prev, h0, lat_prev, kr_prev, pos0, w, f1, f2, lns):
    nb, L, _ = x.shape
    T = nb * L
    tm = min(512, T)
    x2d = x.reshape(T, D_MODEL)
    (g1, b1), (g2, b2), (g3, b3) = lns
    x1 = _ffn_ln(x2d, *f1, g1, b1, tm)

    prompt = lat_prev is None
    cc, ss = _rope_tables(pos0 + jnp.arange(L))
    if L % tm != 0:
        cc, ss = jnp.tile(cc, (nb, 1)), jnp.tile(ss, (nb, 1))
    z, xbc, dt, lat, kr, q, *kv = _in_proj(
        x1, cc, ss, w["win"], w["qg"], w["wuq"], w["kvg"], w["wuk"], w["wuv"], tm=tm, seq_len=L,
        absorbed=not prompt)

    y, nconv, hnew = _ssd(xbc, z, dt, conv_prev, h0, w["cw"], w["cb"], w["dtb"], w["alog"],
                          w["dsk"], w["ng"], nb=nb, seq_len=L)
    if prompt:
        o = _attn_prompt(q, *kv, nb=nb, seq_len=L)
    else:
        past = lat_prev.shape[1]
        o = _attn_sample(q, lat_prev.reshape(nb * past, KV_LORA), kr_prev.reshape(nb * past, QK_ROPE),
                         lat, kr, nb=nb, past_len=past, new_len=L)
    out = _mix_ffn_ln(x1, y, o, None if prompt else w["wuv"], w["wo"], g2, b2, *f2, g3, b3, tm)
    return (out.reshape(nb, L, D_MODEL), lat.reshape(nb, L, KV_LORA), kr.reshape(nb, L, QK_ROPE),
            nconv, hnew)


def kernel(x_prompt, x_sample, cache_latent, cache_k_rope, state_conv, state_ssm, ln1_g, ln1_b, ffn1_w_gate, ffn1_w_up, ffn1_w_down, w_in, conv_w, conv_b, dt_bias, a_log, d_skip, ssd_norm_g, q_norm_g, w_uq, kv_norm_g, w_ukv, w_out, ln2_g, ln2_b, ffn2_w_gate, ffn2_w_up, ffn2_w_down, ln3_g, ln3_b):
    assert w_in.shape[0] == DEPTH == 1
    l = 0
    w = _pack_weights(w_in[l], conv_w[l], conv_b[l], dt_bias[l], a_log[l], d_skip[l],
                      ssd_norm_g[l], q_norm_g[l], w_uq[l], kv_norm_g[l], w_ukv[l], w_out[l])
    f1 = (ffn1_w_gate[l].astype(BF16), ffn1_w_up[l].astype(BF16), ffn1_w_down[l].astype(BF16))
    f2 = (ffn2_w_gate[l].astype(BF16), ffn2_w_up[l].astype(BF16), ffn2_w_down[l].astype(BF16))
    row = lambda v: v[l].reshape(1, D_MODEL)
    lns = ((row(ln1_g), row(ln1_b)), (row(ln2_g), row(ln2_b)), (row(ln3_g), row(ln3_b)))

    yp, lat_p, kr_p, conv_p, ssm_p = _stream(x_prompt, None, None, None, None, 0, w, f1, f2, lns)
    past = cache_latent.shape[2]
    ys, lat_s, kr_s, conv_s, ssm_s = _stream(x_sample, state_conv[l], state_ssm[l], cache_latent[l],
                                             cache_k_rope[l], past, w, f1, f2, lns)
    st = lambda a: a[None]
    return (yp, ys, st(lat_p), st(kr_p), st(conv_p), st(ssm_p),
            st(lat_s), st(kr_s), st(conv_s), st(ssm_s))
```

```python
import functools

import numpy as np
import jax
import jax.numpy as jnp
from jax import lax
from jax.experimental import pallas as pl
from jax.experimental.pallas import tpu as pltpu

F32 = jnp.float32
BF16 = jnp.bfloat16

D_MODEL = 1024
D_FF = 2816
CHUNK = 64
SSD_HEADS = 16
SSD_HEAD_DIM = 64
SSD_D_INNER = SSD_HEADS * SSD_HEAD_DIM
SSD_GROUPS = 2
SSD_HEADS_PER_GROUP = SSD_HEADS // SSD_GROUPS
SSD_STATE = 128
CONV_W = 4
SSD_CONV_DIM = SSD_D_INNER + 2 * SSD_GROUPS * SSD_STATE
MLA_HEADS = 8
Q_LORA = 512
KV_LORA = 512
QK_NOPE = 128
QK_ROPE = 64
V_HEAD = 128
MLA_WIDTH = MLA_HEADS * V_HEAD
ROPE_THETA = 10000.0
DEPTH = 1
ALPHA = (2 * DEPTH) ** 0.25
EPS = 1e-5
SM_SCALE = (QK_NOPE + QK_ROPE) ** -0.5
LOG2E = 1.4426950408889634
NEG_BIG = -1e30

LANES = 128
QK_PAD = 2 * LANES
SSD_Q = 128
FF_CHUNK = 256
ATT_TQ = 1024
ATT_TK = 256
VMEM_LIMIT = 56 * 1024 * 1024

_NT = (((1,), (1,)), ((), ()))


def _resident(shape):
    nd = len(shape)
    return pl.BlockSpec(shape, lambda *_: (0,) * nd, pipeline_mode=pl.Buffered(1))


def _dot(a, b):
    return jnp.dot(a, b, preferred_element_type=F32)


def _dot_nt(a, b):
    return lax.dot_general(a, b, _NT, preferred_element_type=F32)


def _sigmoid(x):
    return 1.0 / (1.0 + jnp.exp2(x * (-LOG2E)))


def _silu(x):
    return x * _sigmoid(x)


def _layer_norm(y, g, b):
    mu = jnp.mean(y, axis=-1, keepdims=True)
    d = y - mu
    var = jnp.mean(d * d, axis=-1, keepdims=True)
    return d * lax.rsqrt(var + EPS) * g + b


def _rms_norm(x, g):
    return x * lax.rsqrt(jnp.mean(x * x, axis=-1, keepdims=True) + EPS) * g


def _swiglu(xb, wg_ref, wu_ref, wd_ref):
    acc = jnp.zeros((xb.shape[0], D_MODEL), F32)
    for c in range(D_FF // FF_CHUNK):
        cs = slice(c * FF_CHUNK, (c + 1) * FF_CHUNK)
        g = _dot(xb, wg_ref[:, cs])
        u = _dot(xb, wu_ref[:, cs])
        h = (_silu(g) * u).astype(BF16)
        acc = acc + _dot(h, wd_ref[cs, :])
    return acc


def _ffn_ln_kernel(x_ref, wg_ref, wu_ref, wd_ref, g_ref, b_ref, o_ref):
    x = x_ref[...]
    ff = _swiglu(x.astype(BF16), wg_ref, wu_ref, wd_ref)
    o_ref[...] = _layer_norm(ALPHA * x + 0.5 * ff, g_ref[...], b_ref[...])


def _ffn_ln(x, wg, wu, wd, g, b, tm):
    T = x.shape[0]
    tok = pl.BlockSpec((tm, D_MODEL), lambda i: (i, 0))
    return pl.pallas_call(
        _ffn_ln_kernel,
        grid=(T // tm,),
        in_specs=[tok, _resident(wg.shape), _resident(wu.shape), _resident(wd.shape),
                  _resident(g.shape), _resident(b.shape)],
        out_specs=tok,
        out_shape=jax.ShapeDtypeStruct((T, D_MODEL), F32),
        compiler_params=pltpu.CompilerParams(dimension_semantics=("parallel",),
                                             vmem_limit_bytes=VMEM_LIMIT),
        name="ffn_ln",
    )(x, wg, wu, wd, g, b)


def _mix_ffn_ln_kernel(x1_ref, y_ref, o_ref, *refs, absorbed):
    if absorbed:
        wuv_ref, *refs = refs
        o = jnp.concatenate(
            [_dot(o_ref[h], wuv_ref[:, h * V_HEAD:(h + 1) * V_HEAD]) for h in range(MLA_HEADS)], axis=1)
    else:
        o = o_ref[...]
    wo_ref, g2_ref, b2_ref, wg_ref, wu_ref, wd_ref, g3_ref, b3_ref, out_ref = refs
    mix = _dot(y_ref[...].astype(BF16), wo_ref[:SSD_D_INNER, :])
    mix = mix + _dot(o.astype(BF16), wo_ref[SSD_D_INNER:, :])
    x2 = _layer_norm(ALPHA * x1_ref[...] + mix, g2_ref[...], b2_ref[...])
    ff = _swiglu(x2.astype(BF16), wg_ref, wu_ref, wd_ref)
    out_ref[...] = _layer_norm(ALPHA * x2 + 0.5 * ff, g3_ref[...], b3_ref[...])


def _mix_ffn_ln(x1, y, o, wuv, wo, g2, b2, wg, wu, wd, g3, b3, tm):
    T = x1.shape[0]
    tok = pl.BlockSpec((tm, D_MODEL), lambda i: (i, 0))
    absorbed = wuv is not None
    o_spec = pl.BlockSpec((MLA_HEADS, tm, KV_LORA), lambda i: (0, i, 0)) if absorbed else tok
    weights = ([wuv] if absorbed else []) + [wo, g2, b2, wg, wu, wd, g3, b3]
    return pl.pallas_call(
        functools.partial(_mix_ffn_ln_kernel, absorbed=absorbed),
        grid=(T // tm,),
        in_specs=[tok, tok, o_spec] + [_resident(w.shape) for w in weights],
        out_specs=tok,
        out_shape=jax.ShapeDtypeStruct((T, D_MODEL), F32),
        compiler_params=pltpu.CompilerParams(dimension_semantics=("parallel",),
                                             vmem_limit_bytes=VMEM_LIMIT),
        name="mix_ffn_ln",
    )(x1, y, o, *weights)


_Z0, _XBC0, _CQ0, _CKV0, _DT0, _KR0, _WIN_COLS = 0, 1024, 2560, 3072, 3584, 3712, 3840


def _rope_tail(rs, cc, ss):
    return rs * cc + pltpu.roll(rs, 64, 1) * ss


def _k_nope_pairs(latb, wuk_ref):
    for pr in range(MLA_HEADS // 2):
        kk = _dot(latb, wuk_ref[:, pr * 2 * QK_NOPE:(pr + 1) * 2 * QK_NOPE])
        yield 2 * pr, kk[:, :QK_NOPE]
        yield 2 * pr + 1, kk[:, QK_NOPE:]


def _in_proj_units(x_ref, cc_ref, ss_ref, win_ref, qg_ref, wuq_ref, kvg_ref, wuk_ref, *refs, absorbed):
    if absorbed:
        z_ref, xbc_ref, dt_ref, lat_ref, kr_ref, q_ref = refs
    else:
        wuv_ref, z_ref, xbc_ref, dt_ref, lat_ref, kr_ref, q_ref, k_ref, v_ref = refs
    wide = 2 * LANES
    xb = x_ref[...].astype(BF16)

    def proj(c0, c1):
        return _dot(xb, win_ref[:, c0:c1])

    for c in range(0, SSD_D_INNER, wide):
        z_ref[:, c:c + wide] = proj(_Z0 + c, _Z0 + c + wide)
        yield
    dk = proj(_DT0, _WIN_COLS)
    dt_ref[...] = dk[:, :LANES]
    yield
    cq = _rms_norm(proj(_CQ0, _CKV0), qg_ref[...]).astype(BF16)
    yield
    for c in range(0, SSD_CONV_DIM, wide):
        xbc_ref[:, c:c + wide] = proj(_XBC0 + c, _XBC0 + c + wide)
        yield
    for h in range(MLA_HEADS):
        qh = _dot(cq, wuq_ref[:, h * QK_PAD:(h + 1) * QK_PAD])
        q_nope = qh[:, :LANES].astype(BF16)
        if absorbed:
            q_nope = _dot_nt(q_nope, wuk_ref[:, h * QK_NOPE:(h + 1) * QK_NOPE]).astype(BF16)
        q_ref[h, :, :q_nope.shape[1]] = q_nope
        q_ref[h, :, q_nope.shape[1]:] = _rope_tail(qh[:, LANES:], cc_ref[...], ss_ref[...]).astype(BF16)
        yield
    lat = _rms_norm(proj(_CKV0, _DT0), kvg_ref[...])
    lat_ref[...] = lat
    kr_tail = _rope_tail(dk[:, LANES:], cc_ref[...], ss_ref[...])
    kr_ref[...] = kr_tail[:, :QK_ROPE]
    yield
    if absorbed:
        return
    latb = lat.astype(BF16)
    kr_tail = kr_tail.astype(BF16)
    for h, k_nope in _k_nope_pairs(latb, wuk_ref):
        k_ref[h, :, :LANES] = k_nope.astype(BF16)
        k_ref[h, :, LANES:] = kr_tail
        if h % 2 == 1:
            yield
    for c in range(0, MLA_WIDTH, wide):
        v_ref[:, c:c + wide] = _dot(latb, wuv_ref[:, c:c + wide]).astype(BF16)
        yield


def _in_proj_kernel(*refs, absorbed):
    for _ in _in_proj_units(*refs, absorbed=absorbed):
        pass


def _in_proj(x1, cc, ss, win, qg, wuq, kvg, wuk, wuv, *, tm, seq_len, absorbed):
    T = x1.shape[0]
    nt = T // tm
    tok = lambda w: pl.BlockSpec((tm, w), lambda i: (i, 0))
    if cc.shape[0] == T:
        tab = pl.BlockSpec((tm, LANES), lambda i: (i, 0))
    else:
        per_seq = seq_len // tm
        tab = pl.BlockSpec((tm, LANES), lambda i: (i % per_seq, 0))
    q_width = KV_LORA + LANES if absorbed else QK_PAD
    head_tok = lambda w: pl.BlockSpec((MLA_HEADS, tm, w), lambda i: (0, i, 0))
    head_shape = lambda w: jax.ShapeDtypeStruct((MLA_HEADS, T, w), BF16)
    out_shape = [
        jax.ShapeDtypeStruct((T, SSD_D_INNER), F32),
        jax.ShapeDtypeStruct((T, SSD_CONV_DIM), F32),
        jax.ShapeDtypeStruct((T, LANES), F32),
        jax.ShapeDtypeStruct((T, KV_LORA), F32),
        jax.ShapeDtypeStruct((T, QK_ROPE), F32),
        head_shape(q_width),
    ]
    out_specs = [tok(SSD_D_INNER), tok(SSD_CONV_DIM), tok(LANES), tok(KV_LORA), tok(QK_ROPE),
                 head_tok(q_width)]
    weights = [win, qg, wuq, kvg, wuk]
    if not absorbed:
        weights.append(wuv)
        out_shape += [head_shape(QK_PAD), jax.ShapeDtypeStruct((T, MLA_WIDTH), BF16)]
        out_specs += [head_tok(QK_PAD), tok(MLA_WIDTH)]
    return pl.pallas_call(
        functools.partial(_in_proj_kernel, absorbed=absorbed),
        grid=(nt,),
        in_specs=[tok(D_MODEL), tab, tab] + [_resident(w.shape) for w in weights],
        out_specs=out_specs,
        out_shape=out_shape,
        compiler_params=pltpu.CompilerParams(dimension_semantics=("parallel",),
                                             vmem_limit_bytes=VMEM_LIMIT),
        name="in_proj",
    )(x1, cc, ss, *weights)


def _split3(x):
    hi = x.astype(BF16)
    r1 = x - hi.astype(F32)
    mid = r1.astype(BF16)
    lo = (r1 - mid.astype(F32)).astype(BF16)
    return hi, mid, lo


def _softplus(x):
    return jnp.maximum(x, 0.0) + jnp.log1p(jnp.exp(-jnp.abs(x)))


def _ssd_block(out, xbc, hist, z, dt_raw, h_in, lb, cw_ref, cb_ref, dtb_ref, alog_ref, dsk_ref, ng_ref):
    Q = SSD_Q
    if lb < Q:
        xbc = jnp.concatenate([xbc, jnp.zeros((Q - lb, SSD_CONV_DIM), F32)], axis=0)
        dt_raw = jnp.concatenate([dt_raw, jnp.zeros((Q - lb, LANES), F32)], axis=0)
    xw = jnp.concatenate([hist, xbc], axis=0)
    conv = xw * cw_ref[0:1, :]
    for k in range(1, CONV_W):
        conv = pltpu.roll(conv, 1, 0) + xw * cw_ref[k:k + 1, :]
    conv = conv[8:, :] + cb_ref[...]
    xc = _silu(conv)
    xs = xc[:, :SSD_D_INNER]
    yield

    row_id = lax.broadcasted_iota(jnp.int32, (Q, LANES), 0)
    dt_col = jnp.where(row_id < lb, _softplus(dt_raw + dtb_ref[...]), 0.0)
    da_col = dt_col * (-jnp.exp(alog_ref[...]))
    ii = lax.broadcasted_iota(jnp.int32, (Q, Q), 0)
    jj = lax.broadcasted_iota(jnp.int32, (Q, Q), 1)
    causal = ii >= jj
    tri = causal.astype(BF16)
    a_col = sum(_dot(tri, p) for p in _split3(da_col)) * LOG2E
    a_row = a_col.T
    dt_row = dt_col.T
    e_col = jnp.exp2(a_col)
    a_last = a_row[:, Q - 1:Q]
    w_row = dt_row * jnp.exp2(a_last - a_row)
    e_last = jnp.exp2(jnp.broadcast_to(a_last, (LANES, LANES)))

    xs_t = xs.T
    yield
    lane = lax.broadcasted_iota(jnp.int32, (Q, LANES), 1)
    lo_half = lane < SSD_HEAD_DIM
    y_parts = []
    h_out = []
    for g in range(SSD_GROUPS):
        b_g = xc[:, SSD_D_INNER + g * SSD_STATE:SSD_D_INNER + (g + 1) * SSD_STATE].astype(BF16)
        c0 = SSD_D_INNER + SSD_GROUPS * SSD_STATE + g * SSD_STATE
        c_g = xc[:, c0:c0 + SSD_STATE].astype(BF16)
        cb = _dot_nt(c_g, b_g)
        h0 = g * SSD_HEADS_PER_GROUP
        hp = h_in[h0:h0 + SSD_HEADS_PER_GROUP].reshape(SSD_HEADS_PER_GROUP * SSD_HEAD_DIM, SSD_STATE)
        y_off = _dot_nt(c_g, hp.astype(BF16))
        lhs = []
        for hh in range(SSD_HEADS_PER_GROUP):
            h = h0 + hh
            lhs.append(xs_t[h * SSD_HEAD_DIM:(h + 1) * SSD_HEAD_DIM, :] * w_row[h:h + 1, :])
        st = _dot(jnp.concatenate(lhs, axis=0).astype(BF16), b_g)
        for hh in range(SSD_HEADS_PER_GROUP):
            h = h0 + hh
            h_out.append(e_last[h:h + 1, :] * h_in[h]
                         + st[hh * SSD_HEAD_DIM:(hh + 1) * SSD_HEAD_DIM, :])
        yield
        for pr in range(SSD_HEADS_PER_GROUP // 2):
            ha = h0 + 2 * pr
            ms = []
            for h in (ha, ha + 1):
                seg = a_col[:, h:h + 1] - a_row[h:h + 1, :]
                m = jnp.exp2(jnp.where(causal, seg, -jnp.inf)) * cb * dt_row[h:h + 1, :]
                ms.append(m.astype(BF16))
            xp = xs[:, ha * SSD_HEAD_DIM:(ha + 2) * SSD_HEAD_DIM]
            rhs = jnp.concatenate([jnp.where(lo_half, xp, 0.0), jnp.where(lo_half, 0.0, xp)],
                                  axis=0).astype(BF16)
            y_d = _dot(jnp.concatenate(ms, axis=1), rhs)
            yo = y_off[:, 2 * pr * SSD_HEAD_DIM:(2 * pr + 2) * SSD_HEAD_DIM]
            dec = jnp.where(lo_half, e_col[:, ha:ha + 1], e_col[:, ha + 1:ha + 2])
            y_parts.append(y_d + yo * dec)
            if pr % 2 == 1:
                yield
    y = jnp.concatenate(y_parts, axis=1) + dsk_ref[...] * xs
    y = y[:lb] * _silu(z)
    half = SSD_D_INNER // SSD_GROUPS
    ng = ng_ref[...]
    y = jnp.concatenate([_rms_norm(y[:, g * half:(g + 1) * half], ng[:, g * half:(g + 1) * half])
                         for g in range(SSD_GROUPS)], axis=1)
    out["y"] = y
    out["h"] = jnp.stack(h_out)


def _ssd_kernel(*refs, lb, zero_init):
    if zero_init:
        xbc_ref, z_ref, dt_ref, *params, y_ref, nconv_ref, hnew_ref, tail_ref, h_ref = refs
        cprev_ref = h0_ref = None
    else:
        (xbc_ref, z_ref, dt_ref, cprev_ref, h0_ref, *params,
         y_ref, nconv_ref, hnew_ref, tail_ref, h_ref) = refs
    j = pl.program_id(1)

    @pl.when(j == 0)
    def _():
        tail_ref[...] = jnp.zeros(tail_ref.shape, F32)
        if zero_init:
            h_ref[...] = jnp.zeros(h_ref.shape, F32)
        else:
            tail_ref[8 - (CONV_W - 1):, :] = cprev_ref[0]
            h_ref[...] = h0_ref[0]

    out = {}
    for _ in _ssd_block(out, xbc_ref[...], tail_ref[...], z_ref[...], dt_ref[...], h_ref[...], lb, *params):
        pass
    y_ref[...] = out["y"]
    h_ref[...] = out["h"]
    tail_ref[...] = xbc_ref[lb - 8:lb, :]

    @pl.when(j == pl.num_programs(1) - 1)
    def _():
        nconv_ref[0] = xbc_ref[lb - (CONV_W - 1):lb, :]
        hnew_ref[0] = h_ref[...]


def _in_proj_ssd_kernel(x_ref, cc_ref, ss_ref, win_w, qg, wuq, kvg, wuk, wuv, cw, cb, dtb, alog, dsk, ng,
                        lat_ref, kr_ref, q_ref, k_ref, v_ref, y_ref, nconv_ref, hnew_ref,
                        zs_ref, xs_ref, dts_ref, tail_ref, h_ref, *, tiles_per_seq):
    i = pl.program_id(0)
    slot = i % 2
    tm = x_ref.shape[0]

    @pl.when(i == 0)
    def _():
        zs_ref[1] = jnp.zeros(zs_ref.shape[1:], F32)
        xs_ref[1] = jnp.zeros(xs_ref.shape[1:], F32)
        dts_ref[1] = jnp.zeros(dts_ref.shape[1:], F32)
        tail_ref[...] = jnp.zeros(tail_ref.shape, F32)
        h_ref[...] = jnp.zeros(h_ref.shape, F32)

    proj_units = _in_proj_units(x_ref, cc_ref, ss_ref, win_w, qg, wuq, kvg, wuk, wuv,
                                zs_ref.at[slot], xs_ref.at[slot], dts_ref.at[slot],
                                lat_ref, kr_ref, q_ref, k_ref, v_ref, absorbed=False)

    prev = 1 - slot
    fresh = (i - 1) % tiles_per_seq == 0
    hist = jnp.where(fresh, 0.0, tail_ref[...])
    h = jnp.where(fresh, 0.0, h_ref[...])
    for sb in range(tm // SSD_Q):
        r0 = sb * SSD_Q
        if sb > 0:
            hist = xs_ref[prev, r0 - 8:r0, :]
        out = {}
        for _ in _ssd_block(out, xs_ref[prev, r0:r0 + SSD_Q, :], hist, zs_ref[prev, r0:r0 + SSD_Q, :],
                            dts_ref[prev, r0:r0 + SSD_Q, :], h, SSD_Q, cw, cb, dtb, alog, dsk, ng):
            next(proj_units, None)
        y_ref[r0:r0 + SSD_Q, :] = out["y"]
        h = out["h"]
    for _ in proj_units:
        pass
    tail_ref[...] = xs_ref[prev, tm - 8:tm, :]
    h_ref[...] = h
    nconv_ref[0] = xs_ref[prev, tm - (CONV_W - 1):tm, :]
    hnew_ref[0] = h


def _in_proj_ssd(x1, cc, ss, win, qg, wuq, kvg, wuk, wuv, cw, cb, dtb, alog, dsk, ng, *, tm, seq_len):
    T = x1.shape[0]
    nt = T // tm
    per_seq = seq_len // tm
    nb = T // seq_len
    cur = lambda i: jnp.minimum(i, nt - 1)
    old = lambda i: jnp.maximum(i - 1, 0)
    tok = lambda w: pl.BlockSpec((tm, w), lambda i: (cur(i), 0))
    tab = pl.BlockSpec((tm, LANES), lambda i: (cur(i) % per_seq, 0))
    head_tok = pl.BlockSpec((MLA_HEADS, tm, QK_PAD), lambda i: (0, cur(i), 0))
    head_shape = jax.ShapeDtypeStruct((MLA_HEADS, T, QK_PAD), BF16)
    weights = [win, qg, wuq, kvg, wuk, wuv, cw, cb, dtb, alog, dsk, ng]
    return pl.pallas_call(
        functools.partial(_in_proj_ssd_kernel, tiles_per_seq=per_seq),
        grid=(nt + 1,),
        in_specs=[tok(D_MODEL), tab, tab] + [_resident(w.shape) for w in weights],
        out_specs=[tok(KV_LORA), tok(QK_ROPE), head_tok, head_tok, tok(MLA_WIDTH),
                   pl.BlockSpec((tm, SSD_D_INNER), lambda i: (old(i), 0)),
                   pl.BlockSpec((1, CONV_W - 1, SSD_CONV_DIM), lambda i: (old(i) // per_seq, 0, 0)),
                   pl.BlockSpec((1, SSD_HEADS, SSD_HEAD_DIM, SSD_STATE),
                                lambda i: (old(i) // per_seq, 0, 0, 0))],
        out_shape=[jax.ShapeDtypeStruct((T, KV_LORA), F32),
                   jax.ShapeDtypeStruct((T, QK_ROPE), F32),
                   head_shape, head_shape,
                   jax.ShapeDtypeStruct((T, MLA_WIDTH), BF16),
                   jax.ShapeDtypeStruct((T, SSD_D_INNER), F32),
                   jax.ShapeDtypeStruct((nb, CONV_W - 1, SSD_CONV_DIM), F32),
                   jax.ShapeDtypeStruct((nb, SSD_HEADS, SSD_HEAD_DIM, SSD_STATE), F32)],
        scratch_shapes=[pltpu.VMEM((2, tm, SSD_D_INNER), F32),
                        pltpu.VMEM((2, tm, SSD_CONV_DIM), F32),
                        pltpu.VMEM((2, tm, LANES), F32),
                        pltpu.VMEM((8, SSD_CONV_DIM), F32),
                        pltpu.VMEM((SSD_HEADS, SSD_HEAD_DIM, SSD_STATE), F32)],
        compiler_params=pltpu.CompilerParams(dimension_semantics=("arbitrary",),
                                             vmem_limit_bytes=VMEM_LIMIT),
        name="in_proj_ssd",
    )(x1, cc, ss, *weights)


def _ssd(xbc, z, dt, cprev, h0, cw, cb, dtb, alog, dsk, ng, *, nb, seq_len):
    lb = min(SSD_Q, seq_len)
    nblk = seq_len // lb
    zero_init = cprev is None
    tokb = lambda w: pl.BlockSpec((lb, w), lambda b, j: (b * nblk + j, 0))
    in_specs = [tokb(SSD_CONV_DIM), tokb(SSD_D_INNER), tokb(LANES)]
    args = [xbc, z, dt]
    if not zero_init:
        in_specs += [pl.BlockSpec((1, CONV_W - 1, SSD_CONV_DIM), lambda b, j: (b, 0, 0)),
                     pl.BlockSpec((1, SSD_HEADS, SSD_HEAD_DIM, SSD_STATE), lambda b, j: (b, 0, 0, 0))]
        args += [cprev, h0]
    params = [cw, cb, dtb, alog, dsk, ng]
    in_specs += [pl.BlockSpec(p.shape, lambda b, j: (0, 0)) for p in params]
    args += params
    return pl.pallas_call(
        functools.partial(_ssd_kernel, lb=lb, zero_init=zero_init),
        grid=(nb, nblk),
        in_specs=in_specs,
        out_specs=(tokb(SSD_D_INNER),
                   pl.BlockSpec((1, CONV_W - 1, SSD_CONV_DIM), lambda b, j: (b, 0, 0)),
                   pl.BlockSpec((1, SSD_HEADS, SSD_HEAD_DIM, SSD_STATE), lambda b, j: (b, 0, 0, 0))),
        out_shape=(jax.ShapeDtypeStruct((nb * seq_len, SSD_D_INNER), F32),
                   jax.ShapeDtypeStruct((nb, CONV_W - 1, SSD_CONV_DIM), F32),
                   jax.ShapeDtypeStruct((nb, SSD_HEADS, SSD_HEAD_DIM, SSD_STATE), F32)),
        scratch_shapes=[pltpu.VMEM((8, SSD_CONV_DIM), F32),
                        pltpu.VMEM((SSD_HEADS, SSD_HEAD_DIM, SSD_STATE), F32)],
        compiler_params=pltpu.CompilerParams(dimension_semantics=("arbitrary", "arbitrary"),
                                             vmem_limit_bytes=VMEM_LIMIT),
        name="ssd",
    )(*args)


def _lane_tile(x, width):
    return jnp.concatenate([x] * (width // LANES), axis=1)


def _attn_prompt_kernel(q_ref, k_ref, v_ref, o_ref, vx_ref, m_ref, acc_ref, *, seq_len, tq, tk):
    vx_ref[:, :V_HEAD] = v_ref[...]
    vx_ref[:, V_HEAD:] = jnp.ones((seq_len, LANES), BF16)
    ri = lax.broadcasted_iota(jnp.int32, (tk, tk), 0)
    ci = lax.broadcasted_iota(jnp.int32, (tk, tk), 1)
    diag_mask = (ci // CHUNK) <= (ri // CHUNK)

    def step(r0, k0, row0, masked):
        s = _dot_nt(q_ref[0, r0 + row0:r0 + tq, :], k_ref[0, k0:k0 + tk, :])
        if masked:
            top = jnp.where(diag_mask, s[:tk], NEG_BIG)
            s = top if tq - row0 == tk else jnp.concatenate([top, s[tk:]], axis=0)
        m_prev = m_ref[row0:, :]
        m_next = jnp.maximum(m_prev, jnp.max(s, axis=-1, keepdims=True))
        alpha = jnp.exp2(m_prev - m_next)
        p = jnp.exp2(s - _lane_tile(m_next, tk))
        m_ref[row0:, :] = m_next
        acc_ref[row0:, :] = (_lane_tile(alpha, V_HEAD + LANES) * acc_ref[row0:, :]
                             + _dot(p.astype(BF16), vx_ref[k0:k0 + tk, :]))

    for qb in range(seq_len // tq):
        r0 = qb * tq
        m_ref[...] = jnp.full(m_ref.shape, NEG_BIG, F32)
        acc_ref[...] = jnp.zeros(acc_ref.shape, F32)
        for j in range(r0 // tk):
            step(r0, j * tk, 0, False)
        for d in range(tq // tk):
            step(r0, r0 + d * tk, d * tk, True)
        o_ref[r0:r0 + tq, :] = acc_ref[:, :V_HEAD] / acc_ref[:, V_HEAD:]


def _attn_prompt(q, k, v, *, nb, seq_len):
    tq = min(ATT_TQ, seq_len)
    tk = min(ATT_TK, tq)
    scratch = [pltpu.VMEM((seq_len, V_HEAD + LANES), BF16), pltpu.VMEM((tq, LANES), F32),
               pltpu.VMEM((tq, V_HEAD + LANES), F32)]
    return pl.pallas_call(
        functools.partial(_attn_prompt_kernel, seq_len=seq_len, tq=tq, tk=tk),
        grid=(nb, MLA_HEADS),
        in_specs=[pl.BlockSpec((1, seq_len, QK_PAD), lambda b, h: (h, b, 0)),
                  pl.BlockSpec((1, seq_len, QK_PAD), lambda b, h: (h, b, 0)),
                  pl.BlockSpec((seq_len, V_HEAD), lambda b, h: (b, h))],
        out_specs=pl.BlockSpec((seq_len, V_HEAD), lambda b, h: (b, h)),
        out_shape=jax.ShapeDtypeStruct((nb * seq_len, MLA_WIDTH), F32),
        scratch_shapes=scratch,
        compiler_params=pltpu.CompilerParams(dimension_semantics=("parallel", "parallel"),
                                             vmem_limit_bytes=VMEM_LIMIT),
        name="attn_prompt",
    )(q, k, v)


def _attn_sample_kernel(q_ref, latc_ref, krc_ref, latn_ref, krn_ref, o_ref, *, past_len, new_len):
    rows = MLA_HEADS * new_len
    qa = jnp.concatenate([q_ref[h] for h in range(MLA_HEADS)], axis=0)
    q_lat = qa[:, :KV_LORA]
    q_rope = qa[:, KV_LORA:KV_LORA + QK_ROPE]
    latc = latc_ref[...].astype(BF16)
    latn = latn_ref[...].astype(BF16)
    sc = _dot_nt(q_lat, latc) + _dot_nt(q_rope, krc_ref[...].astype(BF16))
    sn = _dot_nt(q_lat, latn) + _dot_nt(q_rope, krn_ref[...].astype(BF16))
    qpos = past_len + lax.broadcasted_iota(jnp.int32, (rows, 1), 0) % new_len
    kpos_c = lax.broadcasted_iota(jnp.int32, (1, past_len), 1)
    kpos_n = past_len + lax.broadcasted_iota(jnp.int32, (1, new_len), 1)
    sc = jnp.where((kpos_c // CHUNK) <= (qpos // CHUNK), sc, NEG_BIG)
    sn = jnp.where((kpos_n // CHUNK) <= (qpos // CHUNK), sn, NEG_BIG)
    m = jnp.maximum(jnp.max(sc, axis=-1, keepdims=True), jnp.max(sn, axis=-1, keepdims=True))
    pc = jnp.exp2(sc - m)
    pn = jnp.exp2(sn - m)
    l = jnp.sum(pc, axis=-1, keepdims=True) + jnp.sum(pn, axis=-1, keepdims=True)
    o = (_dot(pc.astype(BF16), latc) + _dot(pn.astype(BF16), latn)) / l
    for h in range(MLA_HEADS):
        o_ref[h] = o[h * new_len:(h + 1) * new_len].astype(BF16)


def _attn_sample(qa, lat_c, kr_c, lat_n, kr_n, *, nb, past_len, new_len):
    return pl.pallas_call(
        functools.partial(_attn_sample_kernel, past_len=past_len, new_len=new_len),
        grid=(nb,),
        in_specs=[pl.BlockSpec((MLA_HEADS, new_len, KV_LORA + LANES), lambda b: (0, b, 0)),
                  pl.BlockSpec((past_len, KV_LORA), lambda b: (b, 0)),
                  pl.BlockSpec((past_len, QK_ROPE), lambda b: (b, 0)),
                  pl.BlockSpec((new_len, KV_LORA), lambda b: (b, 0)),
                  pl.BlockSpec((new_len, QK_ROPE), lambda b: (b, 0))],
        out_specs=pl.BlockSpec((MLA_HEADS, new_len, KV_LORA), lambda b: (0, b, 0)),
        out_shape=jax.ShapeDtypeStruct((MLA_HEADS, nb * new_len, KV_LORA), BF16),
        compiler_params=pltpu.CompilerParams(dimension_semantics=("parallel",),
                                             vmem_limit_bytes=VMEM_LIMIT),
        name="attn_sample",
    )(qa, lat_c, kr_c, lat_n, kr_n)


def _rope_tables(pos):
    inv = ROPE_THETA ** (-jnp.arange(0, QK_ROPE, 2, dtype=F32) / QK_ROPE)
    ang = pos.astype(F32)[:, None] * inv[None, :]
    cos, sin = jnp.cos(ang), jnp.sin(ang)
    pad = jnp.zeros((pos.shape[0], LANES - QK_ROPE), F32)
    return (jnp.concatenate([cos, cos, pad], axis=1), jnp.concatenate([-sin, sin, pad], axis=1))


def _pack_weights(w_in, conv_w, conv_b, dt_bias, a_log, d_skip, ssd_norm_g, q_norm_g, w_uq,
                  kv_norm_g, w_ukv, w_out):
    swap = np.concatenate([np.arange(QK_ROPE // 2, QK_ROPE), np.arange(QK_ROPE // 2)])
    i0 = SSD_D_INNER
    i1 = i0 + SSD_CONV_DIM
    i2 = i1 + SSD_HEADS
    i3 = i2 + Q_LORA
    i4 = i3 + KV_LORA
    w_dt = jnp.pad(w_in[:, i1:i2], ((0, 0), (0, LANES - SSD_HEADS)))
    w_kr = w_in[:, i4:]
    win = jnp.concatenate([w_in[:, :i0], w_in[:, i0:i1], w_in[:, i2:i3], w_in[:, i3:i4], w_dt,
                           w_kr, w_kr[:, swap]], axis=1).astype(BF16)
    uq = (w_uq * (SM_SCALE * LOG2E)).reshape(Q_LORA, MLA_HEADS, QK_NOPE + QK_ROPE)
    wuq = jnp.concatenate([uq, uq[:, :, QK_NOPE:][:, :, swap]], axis=2)
    wuq = wuq.reshape(Q_LORA, MLA_HEADS * QK_PAD).astype(BF16)
    ukv = w_ukv.reshape(KV_LORA, MLA_HEADS, QK_NOPE + V_HEAD)
    wuk = ukv[:, :, :QK_NOPE].reshape(KV_LORA, MLA_HEADS * QK_NOPE).astype(BF16)
    wuv = ukv[:, :, QK_NOPE:].reshape(KV_LORA, MLA_WIDTH).astype(BF16)
    lane_pad = lambda v: jnp.pad(v, (0, LANES - SSD_HEADS)).reshape(1, LANES)
    return dict(
        win=win, wuq=wuq, wuk=wuk, wuv=wuv,
        qg=q_norm_g.reshape(1, Q_LORA), kvg=kv_norm_g.reshape(1, KV_LORA),
        cw=conv_w, cb=conv_b.reshape(1, SSD_CONV_DIM),
        dtb=lane_pad(dt_bias), alog=lane_pad(a_log),
        dsk=jnp.repeat(d_skip, SSD_HEAD_DIM).reshape(1, SSD_D_INNER),
        ng=ssd_norm_g.reshape(1, SSD_D_INNER),
        wo=w_out.astype(BF16),
    )


def _stream(x, conv_prev, h0, lat_prev, kr_prev, pos0, w, f1, f2, lns):
    nb, L, _ = x.shape
    T = nb * L
    tm = min(512, T)
    x2d = x.reshape(T, D_MODEL)
    (g1, b1), (g2, b2), (g3, b3) = lns
    x1 = _ffn_ln(x2d, *f1, g1, b1, tm)

    prompt = lat_prev is None
    cc, ss = _rope_tables(pos0 + jnp.arange(L))
    if L % tm != 0:
        cc, ss = jnp.tile(cc, (nb, 1)), jnp.tile(ss, (nb, 1))
    proj_w = (w["win"], w["qg"], w["wuq"], w["kvg"], w["wuk"], w["wuv"])
    ssd_w = (w["cw"], w["cb"], w["dtb"], w["alog"], w["dsk"], w["ng"])
    if prompt:
        assert L % tm == 0 and tm % SSD_Q == 0
        lat, kr, q, kcat, v, y, nconv, hnew = _in_proj_ssd(x1, cc, ss, *proj_w, *ssd_w, tm=tm, seq_len=L)
        o = _attn_prompt(q, kcat, v, nb=nb, seq_len=L)
    else:
        z, xbc, dt, lat, kr, q = _in_proj(x1, cc, ss, *proj_w, tm=tm, seq_len=L, absorbed=True)
        y, nconv, hnew = _ssd(xbc, z, dt, conv_prev, h0, *ssd_w, nb=nb, seq_len=L)
        past = lat_prev.shape[1]
        o = _attn_sample(q, lat_prev.reshape(nb * past, KV_LORA), kr_prev.reshape(nb * past, QK_ROPE),
                         lat, kr, nb=nb, past_len=past, new_len=L)
    out = _mix_ffn_ln(x1, y, o, None if prompt else w["wuv"], w["wo"], g2, b2, *f2, g3, b3, tm)
    return (out.reshape(nb, L, D_MODEL), lat.reshape(nb, L, KV_LORA), kr.reshape(nb, L, QK_ROPE),
            nconv, hnew)


def kernel(x_prompt, x_sample, cache_latent, cache_k_rope, state_conv, state_ssm, ln1_g, ln1_b, ffn1_w_gate, ffn1_w_up, ffn1_w_down, w_in, conv_w, conv_b, dt_bias, a_log, d_skip, ssd_norm_g, q_norm_g, w_uq, kv_norm_g, w_ukv, w_out, ln2_g, ln2_b, ffn2_w_gate, ffn2_w_up, ffn2_w_down, ln3_g, ln3_b):
    assert w_in.shape[0] == DEPTH == 1
    l = 0
    w = _pack_weights(w_in[l], conv_w[l], conv_b[l], dt_bias[l], a_log[l], d_skip[l],
                      ssd_norm_g[l], q_norm_g[l], w_uq[l], kv_norm_g[l], w_ukv[l], w_out[l])
    f1 = (ffn1_w_gate[l].astype(BF16), ffn1_w_up[l].astype(BF16), ffn1_w_down[l].astype(BF16))
    f2 = (ffn2_w_gate[l].astype(BF16), ffn2_w_up[l].astype(BF16), ffn2_w_down[l].astype(BF16))
    row = lambda v: v[l].reshape(1, D_MODEL)
    lns = ((row(ln1_g), row(ln1_b)), (row(ln2_g), row(ln2_b)), (row(ln3_g), row(ln3_b)))

    yp, lat_p, kr_p, conv_p, ssm_p = _stream(x_prompt, None, None, None, None, 0, w, f1, f2, lns)
    past = cache_latent.shape[2]
    ys, lat_s, kr_s, conv_s, ssm_s = _stream(x_sample, state_conv[l], state_ssm[l], cache_latent[l],
                                             cache_k_rope[l], past, w, f1, f2, lns)
    st = lambda a: a[None]
    return (yp, ys, st(lat_p), st(kr_p), st(conv_p), st(ssm_p),
            st(lat_s), st(kr_s), st(conv_s), st(ssm_s))
```

```python
import functools

import numpy as np
import jax
import jax.numpy as jnp
from jax import lax
from jax.experimental import pallas as pl
from jax.experimental.pallas import tpu as pltpu

F32 = jnp.float32
BF16 = jnp.bfloat16

D_MODEL = 1024
D_FF = 2816
CHUNK = 64
SSD_HEADS = 16
SSD_HEAD_DIM = 64
SSD_D_INNER = SSD_HEADS * SSD_HEAD_DIM
SSD_GROUPS = 2
SSD_HEADS_PER_GROUP = SSD_HEADS // SSD_GROUPS
SSD_STATE = 128
CONV_W = 4
SSD_CONV_DIM = SSD_D_INNER + 2 * SSD_GROUPS * SSD_STATE
MLA_HEADS = 8
Q_LORA = 512
KV_LORA = 512
QK_NOPE = 128
QK_ROPE = 64
V_HEAD = 128
MLA_WIDTH = MLA_HEADS * V_HEAD
ROPE_THETA = 10000.0
DEPTH = 1
ALPHA = (2 * DEPTH) ** 0.25
EPS = 1e-5
SM_SCALE = (QK_NOPE + QK_ROPE) ** -0.5
LOG2E = 1.4426950408889634
NEG_BIG = -1e30

LANES = 128
BF16_SUBLANES = 16
QK_PAD = 2 * LANES
SSD_Q = 128
FF_CHUNK = 256
TOKEN_TILE = 512
ATT_TQ = 1024
ATT_TK = 256
ATT_HEADS_PER_STEP = 2
VMEM_LIMIT = 56 * 1024 * 1024

_NT = (((1,), (1,)), ((), ()))


def _resident(shape):
    nd = len(shape)
    return pl.BlockSpec(shape, lambda *_: (0,) * nd, pipeline_mode=pl.Buffered(1))


def _dot(a, b):
    return jnp.dot(a, b, preferred_element_type=F32)


def _dot_nt(a, b):
    return lax.dot_general(a, b, _NT, preferred_element_type=F32)


def _sigmoid(x):
    return 1.0 / (1.0 + jnp.exp2(x * (-LOG2E)))


def _silu(x):
    return x * _sigmoid(x)


def _layer_norm(y, g, b):
    mu = jnp.mean(y, axis=-1, keepdims=True)
    d = y - mu
    var = jnp.mean(d * d, axis=-1, keepdims=True)
    return d * lax.rsqrt(var + EPS) * g + b


def _rms_norm(x, g):
    return x * lax.rsqrt(jnp.mean(x * x, axis=-1, keepdims=True) + EPS) * g


def _swiglu(xb, wg_ref, wu_ref, wd_ref):
    acc = jnp.zeros((xb.shape[0], D_MODEL), F32)
    for c in range(D_FF // FF_CHUNK):
        cs = slice(c * FF_CHUNK, (c + 1) * FF_CHUNK)
        g = _dot(xb, wg_ref[:, cs])
        u = _dot(xb, wu_ref[:, cs])
        h = (_silu(g) * u).astype(BF16)
        acc = acc + _dot(h, wd_ref[cs, :])
    return acc


def _ffn_ln_kernel(x_ref, wg_ref, wu_ref, wd_ref, g_ref, b_ref, *refs):
    n_cast = (len(refs) - 1) // 2
    o_ref = refs[n_cast]
    x = x_ref[...]
    ff = _swiglu(x.astype(BF16), wg_ref, wu_ref, wd_ref)
    o_ref[...] = _layer_norm(ALPHA * x + 0.5 * ff, g_ref[...], b_ref[...])
    for src, dst in zip(refs[:n_cast], refs[n_cast + 1:]):
        dst[...] = src[...].astype(BF16)


def _cast_row_block(rows, steps):
    units = rows // BF16_SUBLANES
    nblk = max(d for d in range(1, min(units, steps) + 1) if units % d == 0)
    return rows // nblk, nblk


def _ffn_ln(x, wg, wu, wd, g, b, tm, casts=()):
    T = x.shape[0]
    nt = T // tm
    tok = pl.BlockSpec((tm, D_MODEL), lambda i: (i, 0))
    cast_specs = []
    for a in casts:
        rpb, nblk = _cast_row_block(a.shape[0], nt)
        cast_specs.append(pl.BlockSpec((rpb, a.shape[1]), lambda i, last=nblk - 1: (jnp.minimum(i, last), 0)))
    out = pl.pallas_call(
        _ffn_ln_kernel,
        grid=(nt,),
        in_specs=[tok, _resident(wg.shape), _resident(wu.shape), _resident(wd.shape),
                  _resident(g.shape), _resident(b.shape)] + cast_specs,
        out_specs=[tok] + cast_specs,
        out_shape=[jax.ShapeDtypeStruct((T, D_MODEL), F32)]
        + [jax.ShapeDtypeStruct(a.shape, BF16) for a in casts],
        compiler_params=pltpu.CompilerParams(dimension_semantics=("arbitrary",),
                                             vmem_limit_bytes=VMEM_LIMIT),
        name="ffn_ln",
    )(x, wg, wu, wd, g, b, *casts)
    return out[0], out[1:]


def _mix_ffn_ln_kernel(x1_ref, y_ref, o_ref, *refs, absorbed):
    if absorbed:
        wukv_ref, *refs = refs
        kv_w = QK_NOPE + V_HEAD
        o = jnp.concatenate([_dot(o_ref[h], wukv_ref[:, h * kv_w + QK_NOPE:(h + 1) * kv_w])
                             for h in range(MLA_HEADS)], axis=1)
    else:
        o = o_ref[...]
    wo_ref, g2_ref, b2_ref, wg_ref, wu_ref, wd_ref, g3_ref, b3_ref, out_ref = refs
    mix = _dot(y_ref[...].astype(BF16), wo_ref[:SSD_D_INNER, :])
    mix = mix + _dot(o.astype(BF16), wo_ref[SSD_D_INNER:, :])
    x2 = _layer_norm(ALPHA * x1_ref[...] + mix, g2_ref[...], b2_ref[...])
    ff = _swiglu(x2.astype(BF16), wg_ref, wu_ref, wd_ref)
    out_ref[...] = _layer_norm(ALPHA * x2 + 0.5 * ff, g3_ref[...], b3_ref[...])


def _mix_ffn_ln(x1, y, o, wukv, wo, g2, b2, wg, wu, wd, g3, b3, tm):
    T = x1.shape[0]
    tok = pl.BlockSpec((tm, D_MODEL), lambda i: (i, 0))
    absorbed = wukv is not None
    o_spec = pl.BlockSpec((MLA_HEADS, tm, KV_LORA), lambda i: (0, i, 0)) if absorbed else tok
    weights = ([wukv] if absorbed else []) + [wo, g2, b2, wg, wu, wd, g3, b3]
    return pl.pallas_call(
        functools.partial(_mix_ffn_ln_kernel, absorbed=absorbed),
        grid=(T // tm,),
        in_specs=[tok, tok, o_spec] + [_resident(w.shape) for w in weights],
        out_specs=tok,
        out_shape=jax.ShapeDtypeStruct((T, D_MODEL), F32),
        compiler_params=pltpu.CompilerParams(dimension_semantics=("parallel",),
                                             vmem_limit_bytes=VMEM_LIMIT),
        name="mix_ffn_ln",
    )(x1, y, o, *weights)


_Z0, _XBC0, _CQ0, _CKV0, _DT0, _KR0, _WIN_COLS = 0, 1024, 2560, 3072, 3584, 3712, 3840


def _rope_tail(rs, cc, ss):
    return rs * cc + pltpu.roll(rs, 64, 1) * ss


def _in_proj_units(x_ref, cc_ref, ss_ref, win_ref, qg_ref, wuq_ref, kvg_ref, wukv_ref, *refs, absorbed):
    if absorbed:
        z_ref, xbc_ref, dt_ref, lat_ref, kr_ref, q_ref = refs
    else:
        z_ref, xbc_ref, dt_ref, lat_ref, kr_ref, q_ref, k_ref, v_ref = refs
    wide = 2 * LANES
    kv_w = QK_NOPE + V_HEAD
    xb = x_ref[...].astype(BF16)

    def proj(c0, c1):
        return _dot(xb, win_ref[:, c0:c1])

    for c in range(0, SSD_D_INNER, wide):
        z_ref[:, c:c + wide] = proj(_Z0 + c, _Z0 + c + wide)
        yield
    dk = proj(_DT0, _WIN_COLS)
    dt_ref[...] = dk[:, :LANES]
    yield
    cq = _rms_norm(proj(_CQ0, _CKV0), qg_ref[...]).astype(BF16)
    yield
    for c in range(0, SSD_CONV_DIM, wide):
        xbc_ref[:, c:c + wide] = proj(_XBC0 + c, _XBC0 + c + wide)
        yield
    for h in range(MLA_HEADS):
        qh = _dot(cq, wuq_ref[:, h * QK_PAD:(h + 1) * QK_PAD])
        q_nope = qh[:, :LANES].astype(BF16)
        if absorbed:
            q_nope = _dot_nt(q_nope, wukv_ref[:, h * kv_w:h * kv_w + QK_NOPE]).astype(BF16)
        q_ref[h, :, :q_nope.shape[1]] = q_nope
        q_ref[h, :, q_nope.shape[1]:] = _rope_tail(qh[:, LANES:], cc_ref[...], ss_ref[...]).astype(BF16)
        yield
    lat = _rms_norm(proj(_CKV0, _DT0), kvg_ref[...])
    lat_ref[...] = lat
    kr_tail = _rope_tail(dk[:, LANES:], cc_ref[...], ss_ref[...])
    kr_ref[...] = kr_tail[:, :QK_ROPE]
    yield
    if absorbed:
        return
    latb = lat.astype(BF16)
    kr_tail = kr_tail.astype(BF16)
    for h in range(MLA_HEADS):
        kv = _dot(latb, wukv_ref[:, h * kv_w:(h + 1) * kv_w])
        k_ref[h, :, :LANES] = kv[:, :QK_NOPE].astype(BF16)
        k_ref[h, :, LANES:] = kr_tail
        v_ref[:, h * V_HEAD:(h + 1) * V_HEAD] = kv[:, QK_NOPE:].astype(BF16)
        yield


def _in_proj_kernel(*refs, absorbed):
    for _ in _in_proj_units(*refs, absorbed=absorbed):
        pass


def _in_proj(x1, cc, ss, win, qg, wuq, kvg, wukv, *, tm, seq_len, absorbed):
    T = x1.shape[0]
    nt = T // tm
    tok = lambda w: pl.BlockSpec((tm, w), lambda i: (i, 0))
    if cc.shape[0] == T:
        tab = pl.BlockSpec((tm, LANES), lambda i: (i, 0))
    else:
        per_seq = seq_len // tm
        tab = pl.BlockSpec((tm, LANES), lambda i: (i % per_seq, 0))
    q_width = KV_LORA + LANES if absorbed else QK_PAD
    head_tok = lambda w: pl.BlockSpec((MLA_HEADS, tm, w), lambda i: (0, i, 0))
    head_shape = lambda w: jax.ShapeDtypeStruct((MLA_HEADS, T, w), BF16)
    out_shape = [
        jax.ShapeDtypeStruct((T, SSD_D_INNER), F32),
        jax.ShapeDtypeStruct((T, SSD_CONV_DIM), F32),
        jax.ShapeDtypeStruct((T, LANES), F32),
        jax.ShapeDtypeStruct((T, KV_LORA), F32),
        jax.ShapeDtypeStruct((T, QK_ROPE), F32),
        head_shape(q_width),
    ]
    out_specs = [tok(SSD_D_INNER), tok(SSD_CONV_DIM), tok(LANES), tok(KV_LORA), tok(QK_ROPE),
                 head_tok(q_width)]
    weights = [win, qg, wuq, kvg, wukv]
    if not absorbed:
        out_shape += [head_shape(QK_PAD), jax.ShapeDtypeStruct((T, MLA_WIDTH), BF16)]
        out_specs += [head_tok(QK_PAD), tok(MLA_WIDTH)]
    return pl.pallas_call(
        functools.partial(_in_proj_kernel, absorbed=absorbed),
        grid=(nt,),
        in_specs=[tok(D_MODEL), tab, tab] + [_resident(w.shape) for w in weights],
        out_specs=out_specs,
        out_shape=out_shape,
        compiler_params=pltpu.CompilerParams(dimension_semantics=("parallel",),
                                             vmem_limit_bytes=VMEM_LIMIT),
        name="in_proj",
    )(x1, cc, ss, *weights)


def _split3(x):
    hi = x.astype(BF16)
    r1 = x - hi.astype(F32)
    mid = r1.astype(BF16)
    lo = (r1 - mid.astype(F32)).astype(BF16)
    return hi, mid, lo


def _softplus(x):
    return jnp.maximum(x, 0.0) + jnp.log1p(jnp.exp(-jnp.abs(x)))


def _ssd_block(out, xbc, hist, z, dt_raw, h_in, lb, cw_ref, cb_ref, dtb_ref, alog_ref, dsk_ref, ng_ref):
    Q = SSD_Q
    if lb < Q:
        xbc = jnp.concatenate([xbc, jnp.zeros((Q - lb, SSD_CONV_DIM), F32)], axis=0)
        dt_raw = jnp.concatenate([dt_raw, jnp.zeros((Q - lb, LANES), F32)], axis=0)
    xw = jnp.concatenate([hist, xbc], axis=0)
    conv = xw * cw_ref[0:1, :]
    for k in range(1, CONV_W):
        conv = pltpu.roll(conv, 1, 0) + xw * cw_ref[k:k + 1, :]
    conv = conv[8:, :] + cb_ref[...]
    xc = _silu(conv)
    xs = xc[:, :SSD_D_INNER]
    yield

    row_id = lax.broadcasted_iota(jnp.int32, (Q, LANES), 0)
    dt_col = jnp.where(row_id < lb, _softplus(dt_raw + dtb_ref[...]), 0.0)
    da_col = dt_col * (-jnp.exp(alog_ref[...]))
    ii = lax.broadcasted_iota(jnp.int32, (Q, Q), 0)
    jj = lax.broadcasted_iota(jnp.int32, (Q, Q), 1)
    causal = ii >= jj
    tri = causal.astype(BF16)
    a_col = sum(_dot(tri, p) for p in _split3(da_col)) * LOG2E
    a_row = a_col.T
    dt_row = dt_col.T
    e_col = jnp.exp2(a_col)
    a_last = a_row[:, Q - 1:Q]
    w_row = dt_row * jnp.exp2(a_last - a_row)
    e_last = jnp.exp2(jnp.broadcast_to(a_last, (LANES, LANES)))

    xs_t = xs.T
    yield
    lane = lax.broadcasted_iota(jnp.int32, (Q, LANES), 1)
    lo_half = lane < SSD_HEAD_DIM
    y_parts = []
    h_out = []
    for g in range(SSD_GROUPS):
        b_g = xc[:, SSD_D_INNER + g * SSD_STATE:SSD_D_INNER + (g + 1) * SSD_STATE].astype(BF16)
        c0 = SSD_D_INNER + SSD_GROUPS * SSD_STATE + g * SSD_STATE
        c_g = xc[:, c0:c0 + SSD_STATE].astype(BF16)
        cb = _dot_nt(c_g, b_g)
        h0 = g * SSD_HEADS_PER_GROUP
        hp = h_in[h0:h0 + SSD_HEADS_PER_GROUP].reshape(SSD_HEADS_PER_GROUP * SSD_HEAD_DIM, SSD_STATE)
        y_off = _dot_nt(c_g, hp.astype(BF16))
        lhs = []
        for hh in range(SSD_HEADS_PER_GROUP):
            h = h0 + hh
            lhs.append(xs_t[h * SSD_HEAD_DIM:(h + 1) * SSD_HEAD_DIM, :] * w_row[h:h + 1, :])
        st = _dot(jnp.concatenate(lhs, axis=0).astype(BF16), b_g)
        for hh in range(SSD_HEADS_PER_GROUP):
            h = h0 + hh
            h_out.append(e_last[h:h + 1, :] * h_in[h]
                         + st[hh * SSD_HEAD_DIM:(hh + 1) * SSD_HEAD_DIM, :])
        yield
        for pr in range(SSD_HEADS_PER_GROUP // 2):
            ha = h0 + 2 * pr
            ms = []
            for h in (ha, ha + 1):
                seg = a_col[:, h:h + 1] - a_row[h:h + 1, :]
                m = jnp.exp2(jnp.where(causal, seg, -jnp.inf)) * cb * dt_row[h:h + 1, :]
                ms.append(m.astype(BF16))
            xp = xs[:, ha * SSD_HEAD_DIM:(ha + 2) * SSD_HEAD_DIM]
            rhs = jnp.concatenate([jnp.where(lo_half, xp, 0.0), jnp.where(lo_half, 0.0, xp)],
                                  axis=0).astype(BF16)
            y_d = _dot(jnp.concatenate(ms, axis=1), rhs)
            yo = y_off[:, 2 * pr * SSD_HEAD_DIM:(2 * pr + 2) * SSD_HEAD_DIM]
            dec = jnp.where(lo_half, e_col[:, ha:ha + 1], e_col[:, ha + 1:ha + 2])
            y_parts.append(y_d + yo * dec)
            if pr % 2 == 1:
                yield
    y = jnp.concatenate(y_parts, axis=1) + dsk_ref[...] * xs
    y = y[:lb] * _silu(z)
    half = SSD_D_INNER // SSD_GROUPS
    ng = ng_ref[...]
    y = jnp.concatenate([_rms_norm(y[:, g * half:(g + 1) * half], ng[:, g * half:(g + 1) * half])
                         for g in range(SSD_GROUPS)], axis=1)
    out["y"] = y
    out["h"] = jnp.stack(h_out)


def _ssd_kernel(*refs, lb, zero_init):
    if zero_init:
        xbc_ref, z_ref, dt_ref, *params, y_ref, nconv_ref, hnew_ref, tail_ref, h_ref = refs
        cprev_ref = h0_ref = None
    else:
        (xbc_ref, z_ref, dt_ref, cprev_ref, h0_ref, *params,
         y_ref, nconv_ref, hnew_ref, tail_ref, h_ref) = refs
    j = pl.program_id(1)

    @pl.when(j == 0)
    def _():
        tail_ref[...] = jnp.zeros(tail_ref.shape, F32)
        if zero_init:
            h_ref[...] = jnp.zeros(h_ref.shape, F32)
        else:
            tail_ref[8 - (CONV_W - 1):, :] = cprev_ref[0]
            h_ref[...] = h0_ref[0]

    out = {}
    for _ in _ssd_block(out, xbc_ref[...], tail_ref[...], z_ref[...], dt_ref[...], h_ref[...], lb, *params):
        pass
    y_ref[...] = out["y"]
    h_ref[...] = out["h"]
    tail_ref[...] = xbc_ref[lb - 8:lb, :]

    @pl.when(j == pl.num_programs(1) - 1)
    def _():
        nconv_ref[0] = xbc_ref[lb - (CONV_W - 1):lb, :]
        hnew_ref[0] = h_ref[...]


def _in_proj_ssd_kernel(x_ref, cc_ref, ss_ref, win_w, qg, wuq, kvg, wukv, cw, cb, dtb, alog, dsk, ng,
                        lat_ref, kr_ref, q_ref, k_ref, v_ref, y_ref, nconv_ref, hnew_ref,
                        zs_ref, xs_ref, dts_ref, tail_ref, h_ref, *, tiles_per_seq):
    i = pl.program_id(0)
    slot = i % 2
    tm = x_ref.shape[0]

    @pl.when(i == 0)
    def _():
        zs_ref[1] = jnp.zeros(zs_ref.shape[1:], F32)
        xs_ref[1] = jnp.zeros(xs_ref.shape[1:], F32)
        dts_ref[1] = jnp.zeros(dts_ref.shape[1:], F32)
        tail_ref[...] = jnp.zeros(tail_ref.shape, F32)
        h_ref[...] = jnp.zeros(h_ref.shape, F32)

    proj_units = _in_proj_units(x_ref, cc_ref, ss_ref, win_w, qg, wuq, kvg, wukv,
                                zs_ref.at[slot], xs_ref.at[slot], dts_ref.at[slot],
                                lat_ref, kr_ref, q_ref, k_ref, v_ref, absorbed=False)

    prev = 1 - slot
    fresh = (i - 1) % tiles_per_seq == 0
    hist = jnp.where(fresh, 0.0, tail_ref[...])
    h = jnp.where(fresh, 0.0, h_ref[...])
    for sb in range(tm // SSD_Q):
        r0 = sb * SSD_Q
        if sb > 0:
            hist = xs_ref[prev, r0 - 8:r0, :]
        out = {}
        for _ in _ssd_block(out, xs_ref[prev, r0:r0 + SSD_Q, :], hist, zs_ref[prev, r0:r0 + SSD_Q, :],
                            dts_ref[prev, r0:r0 + SSD_Q, :], h, SSD_Q, cw, cb, dtb, alog, dsk, ng):
            next(proj_units, None)
        y_ref[r0:r0 + SSD_Q, :] = out["y"]
        h = out["h"]
    for _ in proj_units:
        pass
    tail_ref[...] = xs_ref[prev, tm - 8:tm, :]
    h_ref[...] = h
    nconv_ref[0] = xs_ref[prev, tm - (CONV_W - 1):tm, :]
    hnew_ref[0] = h


def _in_proj_ssd(x1, cc, ss, win, qg, wuq, kvg, wukv, cw, cb, dtb, alog, dsk, ng, *, tm, seq_len):
    T = x1.shape[0]
    nt = T // tm
    per_seq = seq_len // tm
    nb = T // seq_len
    cur = lambda i: jnp.minimum(i, nt - 1)
    old = lambda i: jnp.maximum(i - 1, 0)
    tok = lambda w: pl.BlockSpec((tm, w), lambda i: (cur(i), 0))
    tab = pl.BlockSpec((tm, LANES), lambda i: (cur(i) % per_seq, 0))
    head_tok = pl.BlockSpec((MLA_HEADS, tm, QK_PAD), lambda i: (0, cur(i), 0))
    head_shape = jax.ShapeDtypeStruct((MLA_HEADS, T, QK_PAD), BF16)
    weights = [win, qg, wuq, kvg, wukv, cw, cb, dtb, alog, dsk, ng]
    return pl.pallas_call(
        functools.partial(_in_proj_ssd_kernel, tiles_per_seq=per_seq),
        grid=(nt + 1,),
        in_specs=[tok(D_MODEL), tab, tab] + [_resident(w.shape) for w in weights],
        out_specs=[tok(KV_LORA), tok(QK_ROPE), head_tok, head_tok, tok(MLA_WIDTH),
                   pl.BlockSpec((tm, SSD_D_INNER), lambda i: (old(i), 0)),
                   pl.BlockSpec((1, CONV_W - 1, SSD_CONV_DIM), lambda i: (old(i) // per_seq, 0, 0)),
                   pl.BlockSpec((1, SSD_HEADS, SSD_HEAD_DIM, SSD_STATE),
                                lambda i: (old(i) // per_seq, 0, 0, 0))],
        out_shape=[jax.ShapeDtypeStruct((T, KV_LORA), F32),
                   jax.ShapeDtypeStruct((T, QK_ROPE), F32),
                   head_shape, head_shape,
                   jax.ShapeDtypeStruct((T, MLA_WIDTH), BF16),
                   jax.ShapeDtypeStruct((T, SSD_D_INNER), F32),
                   jax.ShapeDtypeStruct((nb, CONV_W - 1, SSD_CONV_DIM), F32),
                   jax.ShapeDtypeStruct((nb, SSD_HEADS, SSD_HEAD_DIM, SSD_STATE), F32)],
        scratch_shapes=[pltpu.VMEM((2, tm, SSD_D_INNER), F32),
                        pltpu.VMEM((2, tm, SSD_CONV_DIM), F32),
                        pltpu.VMEM((2, tm, LANES), F32),
                        pltpu.VMEM((8, SSD_CONV_DIM), F32),
                        pltpu.VMEM((SSD_HEADS, SSD_HEAD_DIM, SSD_STATE), F32)],
        compiler_params=pltpu.CompilerParams(dimension_semantics=("arbitrary",),
                                             vmem_limit_bytes=VMEM_LIMIT),
        name="in_proj_ssd",
    )(x1, cc, ss, *weights)


def _ssd(xbc, z, dt, cprev, h0, cw, cb, dtb, alog, dsk, ng, *, nb, seq_len):
    lb = min(SSD_Q, seq_len)
    nblk = seq_len // lb
    zero_init = cprev is None
    tokb = lambda w: pl.BlockSpec((lb, w), lambda b, j: (b * nblk + j, 0))
    in_specs = [tokb(SSD_CONV_DIM), tokb(SSD_D_INNER), tokb(LANES)]
    args = [xbc, z, dt]
    if not zero_init:
        in_specs += [pl.BlockSpec((1, CONV_W - 1, SSD_CONV_DIM), lambda b, j: (b, 0, 0)),
                     pl.BlockSpec((1, SSD_HEADS, SSD_HEAD_DIM, SSD_STATE), lambda b, j: (b, 0, 0, 0))]
        args += [cprev, h0]
    params = [cw, cb, dtb, alog, dsk, ng]
    in_specs += [pl.BlockSpec(p.shape, lambda b, j: (0, 0)) for p in params]
    args += params
    return pl.pallas_call(
        functools.partial(_ssd_kernel, lb=lb, zero_init=zero_init),
        grid=(nb, nblk),
        in_specs=in_specs,
        out_specs=(tokb(SSD_D_INNER),
                   pl.BlockSpec((1, CONV_W - 1, SSD_CONV_DIM), lambda b, j: (b, 0, 0)),
                   pl.BlockSpec((1, SSD_HEADS, SSD_HEAD_DIM, SSD_STATE), lambda b, j: (b, 0, 0, 0))),
        out_shape=(jax.ShapeDtypeStruct((nb * seq_len, SSD_D_INNER), F32),
                   jax.ShapeDtypeStruct((nb, CONV_W - 1, SSD_CONV_DIM), F32),
                   jax.ShapeDtypeStruct((nb, SSD_HEADS, SSD_HEAD_DIM, SSD_STATE), F32)),
        scratch_shapes=[pltpu.VMEM((8, SSD_CONV_DIM), F32),
                        pltpu.VMEM((SSD_HEADS, SSD_HEAD_DIM, SSD_STATE), F32)],
        compiler_params=pltpu.CompilerParams(dimension_semantics=("arbitrary", "arbitrary"),
                                             vmem_limit_bytes=VMEM_LIMIT),
        name="ssd",
    )(*args)


def _lane_tile(x, width):
    return jnp.concatenate([x] * (width // LANES), axis=1)


def _attn_prompt_kernel(q_ref, k_ref, v_ref, o_ref, vx_ref, m_ref, acc_ref, *, seq_len, tq, tk, nh):
    ri = lax.broadcasted_iota(jnp.int32, (tk, tk), 0)
    ci = lax.broadcasted_iota(jnp.int32, (tk, tk), 1)
    diag_mask = (ci // CHUNK) <= (ri // CHUNK)

    def head_steps(hh):
        cols = slice(hh * V_HEAD, (hh + 1) * V_HEAD)
        vx_ref[hh, :, :V_HEAD] = v_ref[:, cols]
        vx_ref[hh, :, V_HEAD:] = jnp.ones((seq_len, LANES), BF16)

        def step(r0, k0, row0, masked):
            s = _dot_nt(q_ref[hh, r0 + row0:r0 + tq, :], k_ref[hh, k0:k0 + tk, :])
            if masked:
                top = jnp.where(diag_mask, s[:tk], NEG_BIG)
                s = top if tq - row0 == tk else jnp.concatenate([top, s[tk:]], axis=0)
            m_prev = m_ref[hh, row0:, :]
            m_next = jnp.maximum(m_prev, jnp.max(s, axis=-1, keepdims=True))
            alpha = jnp.exp2(m_prev - m_next)
            p = jnp.exp2(s - _lane_tile(m_next, tk))
            m_ref[hh, row0:, :] = m_next
            acc_ref[hh, row0:, :] = (_lane_tile(alpha, V_HEAD + LANES) * acc_ref[hh, row0:, :]
                                     + _dot(p.astype(BF16), vx_ref[hh, k0:k0 + tk, :]))

        for qb in range(seq_len // tq):
            r0 = qb * tq
            m_ref[hh] = jnp.full(m_ref.shape[1:], NEG_BIG, F32)
            acc_ref[hh] = jnp.zeros(acc_ref.shape[1:], F32)
            for j in range(r0 // tk):
                step(r0, j * tk, 0, False)
                yield
            for d in range(tq // tk):
                step(r0, r0 + d * tk, d * tk, True)
                yield
            o_ref[r0:r0 + tq, cols] = acc_ref[hh, :, :V_HEAD] / acc_ref[hh, :, V_HEAD:]

    active = [head_steps(hh) for hh in range(nh)]
    while active:
        active = [g for g in active if next(g, True) is None]


def _attn_prompt(q, k, v, *, nb, seq_len):
    tq = min(ATT_TQ, seq_len)
    tk = min(ATT_TK, tq)
    nh = ATT_HEADS_PER_STEP
    scratch = [pltpu.VMEM((nh, seq_len, V_HEAD + LANES), BF16), pltpu.VMEM((nh, tq, LANES), F32),
               pltpu.VMEM((nh, tq, V_HEAD + LANES), F32)]
    return pl.pallas_call(
        functools.partial(_attn_prompt_kernel, seq_len=seq_len, tq=tq, tk=tk, nh=nh),
        grid=(nb, MLA_HEADS // nh),
        in_specs=[pl.BlockSpec((nh, seq_len, QK_PAD), lambda b, h: (h, b, 0)),
                  pl.BlockSpec((nh, seq_len, QK_PAD), lambda b, h: (h, b, 0)),
                  pl.BlockSpec((seq_len, nh * V_HEAD), lambda b, h: (b, h))],
        out_specs=pl.BlockSpec((seq_len, nh * V_HEAD), lambda b, h: (b, h)),
        out_shape=jax.ShapeDtypeStruct((nb * seq_len, MLA_WIDTH), F32),
        scratch_shapes=scratch,
        compiler_params=pltpu.CompilerParams(dimension_semantics=("parallel", "parallel"),
                                             vmem_limit_bytes=VMEM_LIMIT),
        name="attn_prompt",
    )(q, k, v)


def _attn_sample_kernel(q_ref, latc_ref, krc_ref, latn_ref, krn_ref, o_ref, *, past_len, new_len):
    rows = MLA_HEADS * new_len
    qa = jnp.concatenate([q_ref[h] for h in range(MLA_HEADS)], axis=0)
    q_lat = qa[:, :KV_LORA]
    q_rope = qa[:, KV_LORA:KV_LORA + QK_ROPE]
    latc = latc_ref[...].astype(BF16)
    latn = latn_ref[...].astype(BF16)
    sc = _dot_nt(q_lat, latc) + _dot_nt(q_rope, krc_ref[...].astype(BF16))
    sn = _dot_nt(q_lat, latn) + _dot_nt(q_rope, krn_ref[...].astype(BF16))
    qpos = past_len + lax.broadcasted_iota(jnp.int32, (rows, 1), 0) % new_len
    kpos_c = lax.broadcasted_iota(jnp.int32, (1, past_len), 1)
    kpos_n = past_len + lax.broadcasted_iota(jnp.int32, (1, new_len), 1)
    sc = jnp.where((kpos_c // CHUNK) <= (qpos // CHUNK), sc, NEG_BIG)
    sn = jnp.where((kpos_n // CHUNK) <= (qpos // CHUNK), sn, NEG_BIG)
    m = jnp.maximum(jnp.max(sc, axis=-1, keepdims=True), jnp.max(sn, axis=-1, keepdims=True))
    pc = jnp.exp2(sc - m)
    pn = jnp.exp2(sn - m)
    l = jnp.sum(pc, axis=-1, keepdims=True) + jnp.sum(pn, axis=-1, keepdims=True)
    o = (_dot(pc.astype(BF16), latc) + _dot(pn.astype(BF16), latn)) / l
    for h in range(MLA_HEADS):
        o_ref[h] = o[h * new_len:(h + 1) * new_len].astype(BF16)


def _attn_sample(qa, lat_c, kr_c, lat_n, kr_n, *, nb, past_len, new_len):
    return pl.pallas_call(
        functools.partial(_attn_sample_kernel, past_len=past_len, new_len=new_len),
        grid=(nb,),
        in_specs=[pl.BlockSpec((MLA_HEADS, new_len, KV_LORA + LANES), lambda b: (0, b, 0)),
                  pl.BlockSpec((past_len, KV_LORA), lambda b: (b, 0)),
                  pl.BlockSpec((past_len, QK_ROPE), lambda b: (b, 0)),
                  pl.BlockSpec((new_len, KV_LORA), lambda b: (b, 0)),
                  pl.BlockSpec((new_len, QK_ROPE), lambda b: (b, 0))],
        out_specs=pl.BlockSpec((MLA_HEADS, new_len, KV_LORA), lambda b: (0, b, 0)),
        out_shape=jax.ShapeDtypeStruct((MLA_HEADS, nb * new_len, KV_LORA), BF16),
        compiler_params=pltpu.CompilerParams(dimension_semantics=("parallel",),
                                             vmem_limit_bytes=VMEM_LIMIT),
        name="attn_sample",
    )(qa, lat_c, kr_c, lat_n, kr_n)


def _rope_tables(pos):
    inv = ROPE_THETA ** (-jnp.arange(0, QK_ROPE, 2, dtype=F32) / QK_ROPE)
    ang = pos.astype(F32)[:, None] * inv[None, :]
    cos, sin = jnp.cos(ang), jnp.sin(ang)
    pad = jnp.zeros((pos.shape[0], LANES - QK_ROPE), F32)
    return (jnp.concatenate([cos, cos, pad], axis=1), jnp.concatenate([-sin, sin, pad], axis=1))


def _pack_weights(w_in, conv_w, conv_b, dt_bias, a_log, d_skip, ssd_norm_g, q_norm_g, w_uq,
                  kv_norm_g, w_ukv, w_out):
    swap = np.concatenate([np.arange(QK_ROPE // 2, QK_ROPE), np.arange(QK_ROPE // 2)])
    i0 = SSD_D_INNER
    i1 = i0 + SSD_CONV_DIM
    i2 = i1 + SSD_HEADS
    i3 = i2 + Q_LORA
    i4 = i3 + KV_LORA
    w_dt = jnp.pad(w_in[:, i1:i2], ((0, 0), (0, LANES - SSD_HEADS)))
    w_kr = w_in[:, i4:]
    win = jnp.concatenate([w_in[:, :i0], w_in[:, i0:i1], w_in[:, i2:i3], w_in[:, i3:i4], w_dt,
                           w_kr, w_kr[:, swap]], axis=1)
    uq = (w_uq * (SM_SCALE * LOG2E)).reshape(Q_LORA, MLA_HEADS, QK_NOPE + QK_ROPE)
    wuq = jnp.concatenate([uq, uq[:, :, QK_NOPE:][:, :, swap]], axis=2)
    wuq = wuq.reshape(Q_LORA, MLA_HEADS * QK_PAD)
    lane_pad = lambda v: jnp.pad(v, (0, LANES - SSD_HEADS)).reshape(1, LANES)
    return dict(
        win=win, wuq=wuq, wukv=w_ukv, wo=w_out,
        qg=q_norm_g.reshape(1, Q_LORA), kvg=kv_norm_g.reshape(1, KV_LORA),
        cw=conv_w, cb=conv_b.reshape(1, SSD_CONV_DIM),
        dtb=lane_pad(dt_bias), alog=lane_pad(a_log),
        dsk=jnp.repeat(d_skip, SSD_HEAD_DIM).reshape(1, SSD_D_INNER),
        ng=ssd_norm_g.reshape(1, SSD_D_INNER),
    )


def _stream(x1, nb, L, conv_prev, h0, lat_prev, kr_prev, pos0, w, f2, lns):
    tm = min(TOKEN_TILE, nb * L)
    _, (g2, b2), (g3, b3) = lns
    prompt = lat_prev is None
    cc, ss = _rope_tables(pos0 + jnp.arange(L))
    if L % tm != 0:
        cc, ss = jnp.tile(cc, (nb, 1)), jnp.tile(ss, (nb, 1))
    proj_w = (w["win"], w["qg"], w["wuq"], w["kvg"], w["wukv"])
    ssd_w = (w["cw"], w["cb"], w["dtb"], w["alog"], w["dsk"], w["ng"])
    if prompt:
        assert L % tm == 0 and tm % SSD_Q == 0
        lat, kr, q, kcat, v, y, nconv, hnew = _in_proj_ssd(x1, cc, ss, *proj_w, *ssd_w, tm=tm, seq_len=L)
        o = _attn_prompt(q, kcat, v, nb=nb, seq_len=L)
    else:
        z, xbc, dt, lat, kr, q = _in_proj(x1, cc, ss, *proj_w, tm=tm, seq_len=L, absorbed=True)
        y, nconv, hnew = _ssd(xbc, z, dt, conv_prev, h0, *ssd_w, nb=nb, seq_len=L)
        past = lat_prev.shape[1]
        o = _attn_sample(q, lat_prev.reshape(nb * past, KV_LORA), kr_prev.reshape(nb * past, QK_ROPE),
                         lat, kr, nb=nb, past_len=past, new_len=L)
    out = _mix_ffn_ln(x1, y, o, None if prompt else w["wukv"], w["wo"], g2, b2, *f2, g3, b3, tm)
    return (out.reshape(nb, L, D_MODEL), lat.reshape(nb, L, KV_LORA), kr.reshape(nb, L, QK_ROPE),
            nconv, hnew)


def kernel(x_prompt, x_sample, cache_latent, cache_k_rope, state_conv, state_ssm, ln1_g, ln1_b, ffn1_w_gate, ffn1_w_up, ffn1_w_down, w_in, conv_w, conv_b, dt_bias, a_log, d_skip, ssd_norm_g, q_norm_g, w_uq, kv_norm_g, w_ukv, w_out, ln2_g, ln2_b, ffn2_w_gate, ffn2_w_up, ffn2_w_down, ln3_g, ln3_b):
    assert w_in.shape[0] == DEPTH == 1
    l = 0
    w = _pack_weights(w_in[l], conv_w[l], conv_b[l], dt_bias[l], a_log[l], d_skip[l],
                      ssd_norm_g[l], q_norm_g[l], w_uq[l], kv_norm_g[l], w_ukv[l], w_out[l])
    f1 = (ffn1_w_gate[l].astype(BF16), ffn1_w_up[l].astype(BF16), ffn1_w_down[l].astype(BF16))
    row = lambda v: v[l].reshape(1, D_MODEL)
    lns = ((row(ln1_g), row(ln1_b)), (row(ln2_g), row(ln2_b)), (row(ln3_g), row(ln3_b)))
    (g1, b1) = lns[0]

    nbp, lp, _ = x_prompt.shape
    nbs, ls, _ = x_sample.shape
    later = ("win", "wuq", "wukv", "wo")
    x1_p, cast = _ffn_ln(x_prompt.reshape(nbp * lp, D_MODEL), *f1, g1, b1, min(TOKEN_TILE, nbp * lp),
                         casts=[ffn2_w_gate[l], ffn2_w_up[l], ffn2_w_down[l]] + [w[k] for k in later])
    f2 = cast[:3]
    w.update(zip(later, cast[3:]))
    x1_s, _ = _ffn_ln(x_sample.reshape(nbs * ls, D_MODEL), *f1, g1, b1, min(TOKEN_TILE, nbs * ls))

    yp, lat_p, kr_p, conv_p, ssm_p = _stream(x1_p, nbp, lp, None, None, None, None, 0, w, f2, lns)
    past = cache_latent.shape[2]
    ys, lat_s, kr_s, conv_s, ssm_s = _stream(x1_s, nbs, ls, state_conv[l], state_ssm[l], cache_latent[l],
                                             cache_k_rope[l], past, w, f2, lns)
    st = lambda a: a[None]
    return (yp, ys, st(lat_p), st(kr_p), st(conv_p), st(ssm_p),
            st(lat_s), st(kr_s), st(conv_s), st(ssm_s))
```

```python
import functools

import numpy as np
import jax
import jax.numpy as jnp
from jax import lax
from jax.experimental import pallas as pl
from jax.experimental.pallas import tpu as pltpu

F32 = jnp.float32
BF16 = jnp.bfloat16

D_MODEL = 1024
D_FF = 2816
CHUNK = 64
SSD_HEADS = 16
SSD_HEAD_DIM = 64
SSD_D_INNER = SSD_HEADS * SSD_HEAD_DIM
SSD_GROUPS = 2
SSD_HEADS_PER_GROUP = SSD_HEADS // SSD_GROUPS
SSD_STATE = 128
CONV_W = 4
SSD_CONV_DIM = SSD_D_INNER + 2 * SSD_GROUPS * SSD_STATE
MLA_HEADS = 8
Q_LORA = 512
KV_LORA = 512
QK_NOPE = 128
QK_ROPE = 64
V_HEAD = 128
MLA_WIDTH = MLA_HEADS * V_HEAD
ROPE_THETA = 10000.0
DEPTH = 1
ALPHA = (2 * DEPTH) ** 0.25
EPS = 1e-5
SM_SCALE = (QK_NOPE + QK_ROPE) ** -0.5
LOG2E = 1.4426950408889634
NEG_BIG = -1e30

LANES = 128
BF16_SUBLANES = 16
QK_PAD = 2 * LANES
SSD_Q = 128
FF_CHUNK = 256
TOKEN_TILE = 512
LN_ROW_BLOCKS = 8
SIDE_LAG = 2
ATT_TQ = 1024
ATT_TK = 256
ATT_HEADS_PER_STEP = 2
VMEM_LIMIT = 56 * 1024 * 1024

_NT = (((1,), (1,)), ((), ()))


def _resident(shape):
    nd = len(shape)
    return pl.BlockSpec(shape, lambda *_: (0,) * nd, pipeline_mode=pl.Buffered(1))


def _dot(a, b):
    return jnp.dot(a, b, preferred_element_type=F32)


def _dot_nt(a, b):
    return lax.dot_general(a, b, _NT, preferred_element_type=F32)


def _sigmoid(x):
    return 1.0 / (1.0 + jnp.exp2(x * (-LOG2E)))


def _silu(x):
    return x * _sigmoid(x)


def _layer_norm(y, g, b):
    mu = jnp.mean(y, axis=-1, keepdims=True)
    d = y - mu
    var = jnp.mean(d * d, axis=-1, keepdims=True)
    return d * lax.rsqrt(var + EPS) * g + b


def _rms_norm(x, g):
    return x * lax.rsqrt(jnp.mean(x * x, axis=-1, keepdims=True) + EPS) * g


def _zero_after(values):
    t = values[0:8]
    for r in range(8, values.shape[0], 8):
        t = t + values[r:r + 8]
    u = t[:, :LANES]
    for c in range(LANES, values.shape[1], LANES):
        u = u + t[:, c:c + LANES]
    return pltpu.bitcast((pltpu.bitcast(u, jnp.uint32) >> 16) >> 16, F32)


def _ffn_residual(x, wg_ref, wu_ref, wd_ref, side=()):
    side = iter(side)
    rows = x.shape[0]
    xb = x.astype(BF16)
    acc = jnp.zeros((rows, D_MODEL), F32)
    due = {}
    for c in range(D_FF // FF_CHUNK):
        cs = slice(c * FF_CHUNK, (c + 1) * FF_CHUNK)
        h = _silu(_dot(xb, wg_ref[:, cs])) * _dot(xb, wu_ref[:, cs])
        for v in due.pop(c, ()):
            h = h + jnp.tile(_zero_after(v), (rows // 8, FF_CHUNK // LANES))
        acc = acc + _dot(h.astype(BF16), wd_ref[cs, :])
        produced = next(side, None)
        if produced is not None and c + SIDE_LAG < D_FF // FF_CHUNK:
            due[c + SIDE_LAG] = produced
    for _ in side:
        pass
    return ALPHA * x + 0.5 * acc


def _layer_norm_rows(src, g_ref, b_ref, dst_ref):
    nblk = LN_ROW_BLOCKS if dst_ref.shape[0] % (8 * LN_ROW_BLOCKS) == 0 else 1
    rows = dst_ref.shape[0] // nblk
    for r in range(nblk):
        sl = slice(r * rows, (r + 1) * rows)
        dst_ref[sl, :] = out = _layer_norm(src[sl, :], g_ref[...], b_ref[...])
        yield (out,)


def _run(gen):
    for _ in gen:
        pass


def _ffn_ln_kernel(x_ref, wg_ref, wu_ref, wd_ref, g_ref, b_ref, *refs):
    n_cast = (len(refs) - 2) // 2
    cast_in, o_ref, cast_out, pre_ref = refs[:n_cast], refs[n_cast], refs[n_cast + 1:-1], refs[-1]
    i = pl.program_id(0)
    nt = pl.num_programs(0) - 1

    def norm_prev():
        return _layer_norm_rows(pre_ref, g_ref, b_ref, o_ref)

    @pl.when(i == 0)
    def _():
        pre_ref[...] = jnp.zeros(pre_ref.shape, F32)

    @pl.when(i < nt)
    def _():
        pre = _ffn_residual(x_ref[...], wg_ref, wu_ref, wd_ref, norm_prev())
        pre_ref[...] = pre
        for src, dst in zip(cast_in, cast_out):
            dst[...] = src[...].astype(BF16)

    @pl.when(i == nt)
    def _():
        _run(norm_prev())


def _cast_row_block(rows, steps):
    units = rows // BF16_SUBLANES
    nblk = max(d for d in range(1, min(units, steps) + 1) if units % d == 0)
    return rows // nblk, nblk


def _ffn_ln(x, wg, wu, wd, g, b, tm, casts=()):
    T = x.shape[0]
    nt = T // tm
    cast_in, cast_out = [], []
    for a in casts:
        rows, cols = a.shape[-2:]
        rpb, nblk = _cast_row_block(rows, nt)
        row_blk = lambda i, last=nblk - 1: jnp.minimum(i, last)
        cast_out.append(pl.BlockSpec((rpb, cols), lambda i, f=row_blk: (f(i), 0)))
        cast_in.append(cast_out[-1] if a.ndim == 2 else
                       pl.BlockSpec((None, rpb, cols), lambda i, f=row_blk: (0, f(i), 0)))
    out = pl.pallas_call(
        _ffn_ln_kernel,
        grid=(nt + 1,),
        in_specs=[pl.BlockSpec((tm, D_MODEL), lambda i: (jnp.minimum(i, nt - 1), 0)),
                  _resident(wg.shape), _resident(wu.shape), _resident(wd.shape),
                  _resident(g.shape), _resident(b.shape)] + cast_in,
        out_specs=[pl.BlockSpec((tm, D_MODEL), lambda i: (jnp.maximum(i - 1, 0), 0))] + cast_out,
        out_shape=[jax.ShapeDtypeStruct((T, D_MODEL), F32)]
        + [jax.ShapeDtypeStruct(a.shape[-2:], BF16) for a in casts],
        scratch_shapes=[pltpu.VMEM((tm, D_MODEL), F32)],
        compiler_params=pltpu.CompilerParams(dimension_semantics=("arbitrary",),
                                             vmem_limit_bytes=VMEM_LIMIT),
        name="ffn_ln",
    )(x, wg, wu, wd, g, b, *casts)
    return out[0], out[1:]


def _mix_ffn_ln_kernel(x1_ref, y_ref, o_ref, *refs, absorbed):
    if absorbed:
        wukv_ref, *refs = refs
    wo_ref, g2_ref, b2_ref, wg_ref, wu_ref, wd_ref, g3_ref, b3_ref, out_ref, pre3_ref = refs
    i = pl.program_id(0)
    nt = pl.num_programs(0) - 1

    def mix_residual():
        if absorbed:
            kv_w = QK_NOPE + V_HEAD
            o = jnp.concatenate([_dot(o_ref[h], wukv_ref[:, h * kv_w + QK_NOPE:(h + 1) * kv_w])
                                 for h in range(MLA_HEADS)], axis=1).astype(BF16)
        else:
            o = o_ref[...]
        mix = _dot(y_ref[...], wo_ref[:SSD_D_INNER, :]) + _dot(o, wo_ref[SSD_D_INNER:, :])
        return ALPHA * x1_ref[...] + mix

    def norm_prev():
        return _layer_norm_rows(pre3_ref, g3_ref, b3_ref, out_ref)

    @pl.when(i == 0)
    def _():
        pre3_ref[...] = jnp.zeros(pre3_ref.shape, F32)

    @pl.when(i < nt)
    def _():
        x2 = _layer_norm(mix_residual(), g2_ref[...], b2_ref[...])
        pre3 = _ffn_residual(x2, wg_ref, wu_ref, wd_ref, norm_prev())
        pre3_ref[...] = pre3

    @pl.when(i == nt)
    def _():
        _run(norm_prev())


def _mix_ffn_ln(x1, y, o, wukv, wo, g2, b2, wg, wu, wd, g3, b3, tm):
    T = x1.shape[0]
    nt = T // tm
    cur = lambda i: jnp.minimum(i, nt - 1)
    tok = pl.BlockSpec((tm, D_MODEL), lambda i: (cur(i), 0))
    absorbed = wukv is not None
    o_spec = pl.BlockSpec((MLA_HEADS, tm, KV_LORA), lambda i: (0, cur(i), 0)) if absorbed else tok
    weights = ([wukv] if absorbed else []) + [wo, g2, b2, wg, wu, wd, g3, b3]
    return pl.pallas_call(
        functools.partial(_mix_ffn_ln_kernel, absorbed=absorbed),
        grid=(nt + 1,),
        in_specs=[tok, tok, o_spec] + [_resident(w.shape) for w in weights],
        out_specs=pl.BlockSpec((tm, D_MODEL), lambda i: (jnp.maximum(i - 1, 0), 0)),
        out_shape=jax.ShapeDtypeStruct((T, D_MODEL), F32),
        scratch_shapes=[pltpu.VMEM((tm, D_MODEL), F32)],
        compiler_params=pltpu.CompilerParams(dimension_semantics=("arbitrary",),
                                             vmem_limit_bytes=VMEM_LIMIT),
        name="mix_ffn_ln",
    )(x1, y, o, *weights)


_Z0, _XBC0, _CQ0, _CKV0, _DT0, _KR0, _WIN_COLS = 0, 1024, 2560, 3072, 3584, 3712, 3840


def _rope_tail(rs, cc, ss):
    return rs * cc + pltpu.roll(rs, 64, 1) * ss


def _in_proj_units(x_ref, cc_ref, ss_ref, win_ref, qg_ref, wuq_ref, kvg_ref, wukv_ref, *refs, absorbed):
    if absorbed:
        z_ref, xbc_ref, dt_ref, lat_ref, kr_ref, q_ref = refs
    else:
        z_ref, xbc_ref, dt_ref, lat_ref, kr_ref, q_ref, k_ref, v_ref = refs
    wide = 2 * LANES
    kv_w = QK_NOPE + V_HEAD
    xb = x_ref[...].astype(BF16)

    def proj(c0, c1):
        return _dot(xb, win_ref[:, c0:c1])

    for c in range(0, SSD_D_INNER, wide):
        z_ref[:, c:c + wide] = proj(_Z0 + c, _Z0 + c + wide)
        yield
    dk = proj(_DT0, _WIN_COLS)
    dt_ref[...] = dk[:, :LANES]
    yield
    cq = _rms_norm(proj(_CQ0, _CKV0), qg_ref[...]).astype(BF16)
    yield
    for c in range(0, SSD_CONV_DIM, wide):
        xbc_ref[:, c:c + wide] = proj(_XBC0 + c, _XBC0 + c + wide)
        yield
    for h in range(MLA_HEADS):
        qh = _dot(cq, wuq_ref[:, h * QK_PAD:(h + 1) * QK_PAD])
        q_nope = qh[:, :LANES].astype(BF16)
        if absorbed:
            q_nope = _dot_nt(q_nope, wukv_ref[:, h * kv_w:h * kv_w + QK_NOPE]).astype(BF16)
        q_ref[h, :, :q_nope.shape[1]] = q_nope
        q_ref[h, :, q_nope.shape[1]:] = _rope_tail(qh[:, LANES:], cc_ref[...], ss_ref[...]).astype(BF16)
        yield
    lat = _rms_norm(proj(_CKV0, _DT0), kvg_ref[...])
    lat_ref[...] = lat
    kr_tail = _rope_tail(dk[:, LANES:], cc_ref[...], ss_ref[...])
    kr_ref[...] = kr_tail[:, :QK_ROPE]
    yield
    if absorbed:
        return
    latb = lat.astype(BF16)
    kr_tail = kr_tail.astype(BF16)
    for h in range(MLA_HEADS):
        kv = _dot(latb, wukv_ref[:, h * kv_w:(h + 1) * kv_w])
        k_ref[h, :, :LANES] = kv[:, :QK_NOPE].astype(BF16)
        k_ref[h, :, LANES:] = kr_tail
        v_ref[:, h * V_HEAD:(h + 1) * V_HEAD] = kv[:, QK_NOPE:].astype(BF16)
        yield


def _in_proj_kernel(*refs, absorbed):
    for _ in _in_proj_units(*refs, absorbed=absorbed):
        pass


def _in_proj(x1, cc, ss, win, qg, wuq, kvg, wukv, *, tm, seq_len, absorbed):
    T = x1.shape[0]
    nt = T // tm
    tok = lambda w: pl.BlockSpec((tm, w), lambda i: (i, 0))
    if cc.shape[0] == T:
        tab = pl.BlockSpec((tm, LANES), lambda i: (i, 0))
    else:
        per_seq = seq_len // tm
        tab = pl.BlockSpec((tm, LANES), lambda i: (i % per_seq, 0))
    q_width = KV_LORA + LANES if absorbed else QK_PAD
    head_tok = lambda w: pl.BlockSpec((MLA_HEADS, tm, w), lambda i: (0, i, 0))
    head_shape = lambda w: jax.ShapeDtypeStruct((MLA_HEADS, T, w), BF16)
    out_shape = [
        jax.ShapeDtypeStruct((T, SSD_D_INNER), F32),
        jax.ShapeDtypeStruct((T, SSD_CONV_DIM), F32),
        jax.ShapeDtypeStruct((T, LANES), F32),
        jax.ShapeDtypeStruct((T, KV_LORA), F32),
        jax.ShapeDtypeStruct((T, QK_ROPE), F32),
        head_shape(q_width),
    ]
    out_specs = [tok(SSD_D_INNER), tok(SSD_CONV_DIM), tok(LANES), tok(KV_LORA), tok(QK_ROPE),
                 head_tok(q_width)]
    weights = [win, qg, wuq, kvg, wukv]
    if not absorbed:
        out_shape += [head_shape(QK_PAD), jax.ShapeDtypeStruct((T, MLA_WIDTH), BF16)]
        out_specs += [head_tok(QK_PAD), tok(MLA_WIDTH)]
    return pl.pallas_call(
        functools.partial(_in_proj_kernel, absorbed=absorbed),
        grid=(nt,),
        in_specs=[tok(D_MODEL), tab, tab] + [_resident(w.shape) for w in weights],
        out_specs=out_specs,
        out_shape=out_shape,
        compiler_params=pltpu.CompilerParams(dimension_semantics=("parallel",),
                                             vmem_limit_bytes=VMEM_LIMIT),
        name="in_proj",
    )(x1, cc, ss, *weights)


def _split3(x):
    hi = x.astype(BF16)
    r1 = x - hi.astype(F32)
    mid = r1.astype(BF16)
    lo = (r1 - mid.astype(F32)).astype(BF16)
    return hi, mid, lo


def _softplus(x):
    return jnp.maximum(x, 0.0) + jnp.log1p(jnp.exp(-jnp.abs(x)))


def _ssd_block(out, xbc, hist, z, dt_raw, h_in, lb, cw_ref, cb_ref, dtb_ref, alog_ref, dsk_ref, ng_ref):
    Q = SSD_Q
    if lb < Q:
        xbc = jnp.concatenate([xbc, jnp.zeros((Q - lb, SSD_CONV_DIM), F32)], axis=0)
        dt_raw = jnp.concatenate([dt_raw, jnp.zeros((Q - lb, LANES), F32)], axis=0)
    xw = jnp.concatenate([hist, xbc], axis=0)
    conv = xw * cw_ref[0:1, :]
    for k in range(1, CONV_W):
        conv = pltpu.roll(conv, 1, 0) + xw * cw_ref[k:k + 1, :]
    conv = conv[8:, :] + cb_ref[...]
    xc = _silu(conv)
    xs = xc[:, :SSD_D_INNER]
    yield

    row_id = lax.broadcasted_iota(jnp.int32, (Q, LANES), 0)
    dt_col = jnp.where(row_id < lb, _softplus(dt_raw + dtb_ref[...]), 0.0)
    da_col = dt_col * (-jnp.exp(alog_ref[...]))
    ii = lax.broadcasted_iota(jnp.int32, (Q, Q), 0)
    jj = lax.broadcasted_iota(jnp.int32, (Q, Q), 1)
    causal = ii >= jj
    tri = causal.astype(BF16)
    a_col = sum(_dot(tri, p) for p in _split3(da_col)) * LOG2E
    a_row = a_col.T
    dt_row = dt_col.T
    e_col = jnp.exp2(a_col)
    a_last = a_row[:, Q - 1:Q]
    w_row = dt_row * jnp.exp2(a_last - a_row)
    e_last = jnp.exp2(jnp.broadcast_to(a_last, (LANES, LANES)))

    xs_t = xs.T
    yield
    lane = lax.broadcasted_iota(jnp.int32, (Q, LANES), 1)
    lo_half = lane < SSD_HEAD_DIM
    y_parts = []
    h_out = []
    for g in range(SSD_GROUPS):
        b_g = xc[:, SSD_D_INNER + g * SSD_STATE:SSD_D_INNER + (g + 1) * SSD_STATE].astype(BF16)
        c0 = SSD_D_INNER + SSD_GROUPS * SSD_STATE + g * SSD_STATE
        c_g = xc[:, c0:c0 + SSD_STATE].astype(BF16)
        cb = _dot_nt(c_g, b_g)
        h0 = g * SSD_HEADS_PER_GROUP
        hp = h_in[h0:h0 + SSD_HEADS_PER_GROUP].reshape(SSD_HEADS_PER_GROUP * SSD_HEAD_DIM, SSD_STATE)
        y_off = _dot_nt(c_g, hp.astype(BF16))
        lhs = []
        for hh in range(SSD_HEADS_PER_GROUP):
            h = h0 + hh
            lhs.append(xs_t[h * SSD_HEAD_DIM:(h + 1) * SSD_HEAD_DIM, :] * w_row[h:h + 1, :])
        st = _dot(jnp.concatenate(lhs, axis=0).astype(BF16), b_g)
        for hh in range(SSD_HEADS_PER_GROUP):
            h = h0 + hh
            h_out.append(e_last[h:h + 1, :] * h_in[h]
                         + st[hh * SSD_HEAD_DIM:(hh + 1) * SSD_HEAD_DIM, :])
        yield
        for pr in range(SSD_HEADS_PER_GROUP // 2):
            ha = h0 + 2 * pr
            ms = []
            for h in (ha, ha + 1):
                seg = a_col[:, h:h + 1] - a_row[h:h + 1, :]
                m = jnp.exp2(jnp.where(causal, seg, -jnp.inf)) * cb * dt_row[h:h + 1, :]
                ms.append(m.astype(BF16))
            xp = xs[:, ha * SSD_HEAD_DIM:(ha + 2) * SSD_HEAD_DIM]
            rhs = jnp.concatenate([jnp.where(lo_half, xp, 0.0), jnp.where(lo_half, 0.0, xp)],
                                  axis=0).astype(BF16)
            y_d = _dot(jnp.concatenate(ms, axis=1), rhs)
            yo = y_off[:, 2 * pr * SSD_HEAD_DIM:(2 * pr + 2) * SSD_HEAD_DIM]
            dec = jnp.where(lo_half, e_col[:, ha:ha + 1], e_col[:, ha + 1:ha + 2])
            y_parts.append(y_d + yo * dec)
            if pr % 2 == 1:
                yield
    y = jnp.concatenate(y_parts, axis=1) + dsk_ref[...] * xs
    y = y[:lb] * _silu(z)
    half = SSD_D_INNER // SSD_GROUPS
    ng = ng_ref[...]
    y = jnp.concatenate([_rms_norm(y[:, g * half:(g + 1) * half], ng[:, g * half:(g + 1) * half])
                         for g in range(SSD_GROUPS)], axis=1)
    out["y"] = y
    out["h"] = jnp.stack(h_out)


def _ssd_kernel(*refs, lb, zero_init):
    if zero_init:
        xbc_ref, z_ref, dt_ref, *params, y_ref, nconv_ref, hnew_ref, tail_ref, h_ref = refs
        cprev_ref = h0_ref = None
    else:
        (xbc_ref, z_ref, dt_ref, cprev_ref, h0_ref, *params,
         y_ref, nconv_ref, hnew_ref, tail_ref, h_ref) = refs
    j = pl.program_id(1)

    @pl.when(j == 0)
    def _():
        tail_ref[...] = jnp.zeros(tail_ref.shape, F32)
        if zero_init:
            h_ref[...] = jnp.zeros(h_ref.shape, F32)
        else:
            tail_ref[8 - (CONV_W - 1):, :] = cprev_ref[0]
            h_ref[...] = h0_ref[0]

    out = {}
    for _ in _ssd_block(out, xbc_ref[...], tail_ref[...], z_ref[...], dt_ref[...], h_ref[...], lb, *params):
        pass
    y_ref[...] = out["y"].astype(BF16)
    h_ref[...] = out["h"]
    tail_ref[...] = xbc_ref[lb - 8:lb, :]

    @pl.when(j == pl.num_programs(1) - 1)
    def _():
        nconv_ref[0] = xbc_ref[lb - (CONV_W - 1):lb, :]
        hnew_ref[0] = h_ref[...]


def _in_proj_ssd_kernel(x_ref, cc_ref, ss_ref, win_w, qg, wuq, kvg, wukv, cw, cb, dtb, alog, dsk, ng,
                        lat_ref, kr_ref, q_ref, k_ref, v_ref, y_ref, nconv_ref, hnew_ref,
                        zs_ref, xs_ref, dts_ref, tail_ref, h_ref, *, tiles_per_seq):
    i = pl.program_id(0)
    slot = i % 2
    tm = x_ref.shape[0]

    @pl.when(i == 0)
    def _():
        zs_ref[1] = jnp.zeros(zs_ref.shape[1:], F32)
        xs_ref[1] = jnp.zeros(xs_ref.shape[1:], F32)
        dts_ref[1] = jnp.zeros(dts_ref.shape[1:], F32)
        tail_ref[...] = jnp.zeros(tail_ref.shape, F32)
        h_ref[...] = jnp.zeros(h_ref.shape, F32)

    proj_units = _in_proj_units(x_ref, cc_ref, ss_ref, win_w, qg, wuq, kvg, wukv,
                                zs_ref.at[slot], xs_ref.at[slot], dts_ref.at[slot],
                                lat_ref, kr_ref, q_ref, k_ref, v_ref, absorbed=False)

    prev = 1 - slot
    fresh = (i - 1) % tiles_per_seq == 0
    hist = jnp.where(fresh, 0.0, tail_ref[...])
    h = jnp.where(fresh, 0.0, h_ref[...])
    for sb in range(tm // SSD_Q):
        r0 = sb * SSD_Q
        if sb > 0:
            hist = xs_ref[prev, r0 - 8:r0, :]
        out = {}
        for _ in _ssd_block(out, xs_ref[prev, r0:r0 + SSD_Q, :], hist, zs_ref[prev, r0:r0 + SSD_Q, :],
                            dts_ref[prev, r0:r0 + SSD_Q, :], h, SSD_Q, cw, cb, dtb, alog, dsk, ng):
            next(proj_units, None)
        y_ref[r0:r0 + SSD_Q, :] = out["y"].astype(BF16)
        h = out["h"]
    for _ in proj_units:
        pass
    tail_ref[...] = xs_ref[prev, tm - 8:tm, :]
    h_ref[...] = h
    nconv_ref[0] = xs_ref[prev, tm - (CONV_W - 1):tm, :]
    hnew_ref[0] = h


def _in_proj_ssd(x1, cc, ss, win, qg, wuq, kvg, wukv, cw, cb, dtb, alog, dsk, ng, *, tm, seq_len):
    T = x1.shape[0]
    nt = T // tm
    per_seq = seq_len // tm
    nb = T // seq_len
    cur = lambda i: jnp.minimum(i, nt - 1)
    old = lambda i: jnp.maximum(i - 1, 0)
    tok = lambda w: pl.BlockSpec((tm, w), lambda i: (cur(i), 0))
    tab = pl.BlockSpec((tm, LANES), lambda i: (cur(i) % per_seq, 0))
    head_tok = pl.BlockSpec((MLA_HEADS, tm, QK_PAD), lambda i: (0, cur(i), 0))
    head_shape = jax.ShapeDtypeStruct((MLA_HEADS, T, QK_PAD), BF16)
    weights = [win, qg, wuq, kvg, wukv, cw, cb, dtb, alog, dsk, ng]
    return pl.pallas_call(
        functools.partial(_in_proj_ssd_kernel, tiles_per_seq=per_seq),
        grid=(nt + 1,),
        in_specs=[tok(D_MODEL), tab, tab] + [_resident(w.shape) for w in weights],
        out_specs=[tok(KV_LORA), tok(QK_ROPE), head_tok, head_tok, tok(MLA_WIDTH),
                   pl.BlockSpec((tm, SSD_D_INNER), lambda i: (old(i), 0)),
                   pl.BlockSpec((1, CONV_W - 1, SSD_CONV_DIM), lambda i: (old(i) // per_seq, 0, 0)),
                   pl.BlockSpec((1, SSD_HEADS, SSD_HEAD_DIM, SSD_STATE),
                                lambda i: (old(i) // per_seq, 0, 0, 0))],
        out_shape=[jax.ShapeDtypeStruct((T, KV_LORA), F32),
                   jax.ShapeDtypeStruct((T, QK_ROPE), F32),
                   head_shape, head_shape,
                   jax.ShapeDtypeStruct((T, MLA_WIDTH), BF16),
                   jax.ShapeDtypeStruct((T, SSD_D_INNER), BF16),
                   jax.ShapeDtypeStruct((nb, CONV_W - 1, SSD_CONV_DIM), F32),
                   jax.ShapeDtypeStruct((nb, SSD_HEADS, SSD_HEAD_DIM, SSD_STATE), F32)],
        scratch_shapes=[pltpu.VMEM((2, tm, SSD_D_INNER), F32),
                        pltpu.VMEM((2, tm, SSD_CONV_DIM), F32),
                        pltpu.VMEM((2, tm, LANES), F32),
                        pltpu.VMEM((8, SSD_CONV_DIM), F32),
                        pltpu.VMEM((SSD_HEADS, SSD_HEAD_DIM, SSD_STATE), F32)],
        compiler_params=pltpu.CompilerParams(dimension_semantics=("arbitrary",),
                                             vmem_limit_bytes=VMEM_LIMIT),
        name="in_proj_ssd",
    )(x1, cc, ss, *weights)


def _ssd(xbc, z, dt, cprev, h0, cw, cb, dtb, alog, dsk, ng, *, nb, seq_len):
    lb = min(SSD_Q, seq_len)
    nblk = seq_len // lb
    zero_init = cprev is None
    tokb = lambda w: pl.BlockSpec((lb, w), lambda b, j: (b * nblk + j, 0))
    in_specs = [tokb(SSD_CONV_DIM), tokb(SSD_D_INNER), tokb(LANES)]
    args = [xbc, z, dt]
    if not zero_init:
        in_specs += [pl.BlockSpec((1, CONV_W - 1, SSD_CONV_DIM), lambda b, j: (b, 0, 0)),
                     pl.BlockSpec((1, SSD_HEADS, SSD_HEAD_DIM, SSD_STATE), lambda b, j: (b, 0, 0, 0))]
        args += [cprev, h0]
    params = [cw, cb, dtb, alog, dsk, ng]
    in_specs += [pl.BlockSpec(p.shape, lambda b, j: (0, 0)) for p in params]
    args += params
    return pl.pallas_call(
        functools.partial(_ssd_kernel, lb=lb, zero_init=zero_init),
        grid=(nb, nblk),
        in_specs=in_specs,
        out_specs=(tokb(SSD_D_INNER),
                   pl.BlockSpec((1, CONV_W - 1, SSD_CONV_DIM), lambda b, j: (b, 0, 0)),
                   pl.BlockSpec((1, SSD_HEADS, SSD_HEAD_DIM, SSD_STATE), lambda b, j: (b, 0, 0, 0))),
        out_shape=(jax.ShapeDtypeStruct((nb * seq_len, SSD_D_INNER), BF16),
                   jax.ShapeDtypeStruct((nb, CONV_W - 1, SSD_CONV_DIM), F32),
                   jax.ShapeDtypeStruct((nb, SSD_HEADS, SSD_HEAD_DIM, SSD_STATE), F32)),
        scratch_shapes=[pltpu.VMEM((8, SSD_CONV_DIM), F32),
                        pltpu.VMEM((SSD_HEADS, SSD_HEAD_DIM, SSD_STATE), F32)],
        compiler_params=pltpu.CompilerParams(dimension_semantics=("arbitrary", "arbitrary"),
                                             vmem_limit_bytes=VMEM_LIMIT),
        name="ssd",
    )(*args)


def _lane_tile(x, width):
    return jnp.concatenate([x] * (width // LANES), axis=1)


def _attn_prompt_kernel(q_ref, k_ref, v_ref, o_ref, vx_ref, m_ref, acc_ref, *, seq_len, tq, tk, nh):
    ri = lax.broadcasted_iota(jnp.int32, (tk, tk), 0)
    ci = lax.broadcasted_iota(jnp.int32, (tk, tk), 1)
    diag_mask = (ci // CHUNK) <= (ri // CHUNK)

    def head_steps(hh):
        cols = slice(hh * V_HEAD, (hh + 1) * V_HEAD)
        vx_ref[hh, :, :V_HEAD] = v_ref[:, cols]
        vx_ref[hh, :, V_HEAD:] = jnp.ones((seq_len, LANES), BF16)

        def step(r0, k0, row0, masked):
            s = _dot_nt(q_ref[hh, r0 + row0:r0 + tq, :], k_ref[hh, k0:k0 + tk, :])
            if masked:
                top = jnp.where(diag_mask, s[:tk], NEG_BIG)
                s = top if tq - row0 == tk else jnp.concatenate([top, s[tk:]], axis=0)
            m_prev = m_ref[hh, row0:, :]
            m_next = jnp.maximum(m_prev, jnp.max(s, axis=-1, keepdims=True))
            alpha = jnp.exp2(m_prev - m_next)
            p = jnp.exp2(s - _lane_tile(m_next, tk))
            m_ref[hh, row0:, :] = m_next
            acc_ref[hh, row0:, :] = (_lane_tile(alpha, V_HEAD + LANES) * acc_ref[hh, row0:, :]
                                     + _dot(p.astype(BF16), vx_ref[hh, k0:k0 + tk, :]))

        for qb in range(seq_len // tq):
            r0 = qb * tq
            m_ref[hh] = jnp.full(m_ref.shape[1:], NEG_BIG, F32)
            acc_ref[hh] = jnp.zeros(acc_ref.shape[1:], F32)
            for j in range(r0 // tk):
                step(r0, j * tk, 0, False)
                yield
            for d in range(tq // tk):
                step(r0, r0 + d * tk, d * tk, True)
                yield
            o_ref[r0:r0 + tq, cols] = (acc_ref[hh, :, :V_HEAD] / acc_ref[hh, :, V_HEAD:]).astype(BF16)

    active = [head_steps(hh) for hh in range(nh)]
    while active:
        active = [g for g in active if next(g, True) is None]


def _attn_prompt(q, k, v, *, nb, seq_len):
    tq = min(ATT_TQ, seq_len)
    tk = min(ATT_TK, tq)
    nh = ATT_HEADS_PER_STEP
    scratch = [pltpu.VMEM((nh, seq_len, V_HEAD + LANES), BF16), pltpu.VMEM((nh, tq, LANES), F32),
               pltpu.VMEM((nh, tq, V_HEAD + LANES), F32)]
    return pl.pallas_call(
        functools.partial(_attn_prompt_kernel, seq_len=seq_len, tq=tq, tk=tk, nh=nh),
        grid=(nb, MLA_HEADS // nh),
        in_specs=[pl.BlockSpec((nh, seq_len, QK_PAD), lambda b, h: (h, b, 0)),
                  pl.BlockSpec((nh, seq_len, QK_PAD), lambda b, h: (h, b, 0)),
                  pl.BlockSpec((seq_len, nh * V_HEAD), lambda b, h: (b, h))],
        out_specs=pl.BlockSpec((seq_len, nh * V_HEAD), lambda b, h: (b, h)),
        out_shape=jax.ShapeDtypeStruct((nb * seq_len, MLA_WIDTH), BF16),
        scratch_shapes=scratch,
        compiler_params=pltpu.CompilerParams(dimension_semantics=("parallel", "parallel"),
                                             vmem_limit_bytes=VMEM_LIMIT),
        name="attn_prompt",
    )(q, k, v)


def _attn_sample_kernel(q_ref, latc_ref, krc_ref, latn_ref, krn_ref, o_ref, *, past_len, new_len):
    rows = MLA_HEADS * new_len
    qa = jnp.concatenate([q_ref[h] for h in range(MLA_HEADS)], axis=0)
    q_lat = qa[:, :KV_LORA]
    q_rope = qa[:, KV_LORA:KV_LORA + QK_ROPE]
    latc = latc_ref[...].astype(BF16)
    latn = latn_ref[...].astype(BF16)
    sc = _dot_nt(q_lat, latc) + _dot_nt(q_rope, krc_ref[...].astype(BF16))
    sn = _dot_nt(q_lat, latn) + _dot_nt(q_rope, krn_ref[...].astype(BF16))
    qpos = past_len + lax.broadcasted_iota(jnp.int32, (rows, 1), 0) % new_len
    kpos_c = lax.broadcasted_iota(jnp.int32, (1, past_len), 1)
    kpos_n = past_len + lax.broadcasted_iota(jnp.int32, (1, new_len), 1)
    sc = jnp.where((kpos_c // CHUNK) <= (qpos // CHUNK), sc, NEG_BIG)
    sn = jnp.where((kpos_n // CHUNK) <= (qpos // CHUNK), sn, NEG_BIG)
    m = jnp.maximum(jnp.max(sc, axis=-1, keepdims=True), jnp.max(sn, axis=-1, keepdims=True))
    pc = jnp.exp2(sc - m)
    pn = jnp.exp2(sn - m)
    l = jnp.sum(pc, axis=-1, keepdims=True) + jnp.sum(pn, axis=-1, keepdims=True)
    o = (_dot(pc.astype(BF16), latc) + _dot(pn.astype(BF16), latn)) / l
    for h in range(MLA_HEADS):
        o_ref[h] = o[h * new_len:(h + 1) * new_len].astype(BF16)


def _attn_sample(qa, lat_c, kr_c, lat_n, kr_n, *, nb, past_len, new_len):
    return pl.pallas_call(
        functools.partial(_attn_sample_kernel, past_len=past_len, new_len=new_len),
        grid=(nb,),
        in_specs=[pl.BlockSpec((MLA_HEADS, new_len, KV_LORA + LANES), lambda b: (0, b, 0)),
                  pl.BlockSpec((past_len, KV_LORA), lambda b: (b, 0)),
                  pl.BlockSpec((past_len, QK_ROPE), lambda b: (b, 0)),
                  pl.BlockSpec((new_len, KV_LORA), lambda b: (b, 0)),
                  pl.BlockSpec((new_len, QK_ROPE), lambda b: (b, 0))],
        out_specs=pl.BlockSpec((MLA_HEADS, new_len, KV_LORA), lambda b: (0, b, 0)),
        out_shape=jax.ShapeDtypeStruct((MLA_HEADS, nb * new_len, KV_LORA), BF16),
        compiler_params=pltpu.CompilerParams(dimension_semantics=("parallel",),
                                             vmem_limit_bytes=VMEM_LIMIT),
        name="attn_sample",
    )(qa, lat_c, kr_c, lat_n, kr_n)


def _rope_tables(pos):
    inv = ROPE_THETA ** (-jnp.arange(0, QK_ROPE, 2, dtype=F32) / QK_ROPE)
    ang = pos.astype(F32)[:, None] * inv[None, :]
    cos, sin = jnp.cos(ang), jnp.sin(ang)
    pad = jnp.zeros((pos.shape[0], LANES - QK_ROPE), F32)
    return (jnp.concatenate([cos, cos, pad], axis=1), jnp.concatenate([-sin, sin, pad], axis=1))


def _pack_weights(w_in, conv_w, conv_b, dt_bias, a_log, d_skip, ssd_norm_g, q_norm_g, w_uq,
                  kv_norm_g, w_ukv, w_out):
    swap = np.concatenate([np.arange(QK_ROPE // 2, QK_ROPE), np.arange(QK_ROPE // 2)])
    i0 = SSD_D_INNER
    i1 = i0 + SSD_CONV_DIM
    i2 = i1 + SSD_HEADS
    i3 = i2 + Q_LORA
    i4 = i3 + KV_LORA
    w_dt = jnp.pad(w_in[:, i1:i2], ((0, 0), (0, LANES - SSD_HEADS)))
    w_kr = w_in[:, i4:]
    win = jnp.concatenate([w_in[:, :i0], w_in[:, i0:i1], w_in[:, i2:i3], w_in[:, i3:i4], w_dt,
                           w_kr, w_kr[:, swap]], axis=1)
    uq = (w_uq * (SM_SCALE * LOG2E)).reshape(Q_LORA, MLA_HEADS, QK_NOPE + QK_ROPE)
    wuq = jnp.concatenate([uq, uq[:, :, QK_NOPE:][:, :, swap]], axis=2)
    wuq = wuq.reshape(Q_LORA, MLA_HEADS * QK_PAD)
    lane_pad = lambda v: jnp.pad(v, (0, LANES - SSD_HEADS)).reshape(1, LANES)
    return dict(
        win=win, wuq=wuq,
        qg=q_norm_g.reshape(1, Q_LORA), kvg=kv_norm_g.reshape(1, KV_LORA),
        cw=conv_w, cb=conv_b.reshape(1, SSD_CONV_DIM),
        dtb=lane_pad(dt_bias), alog=lane_pad(a_log),
        dsk=jnp.repeat(d_skip, SSD_HEAD_DIM).reshape(1, SSD_D_INNER),
        ng=ssd_norm_g.reshape(1, SSD_D_INNER),
    )


def _stream(x1, nb, L, conv_prev, h0, lat_prev, kr_prev, pos0, w, f2, lns):
    tm = min(TOKEN_TILE, nb * L)
    _, (g2, b2), (g3, b3) = lns
    prompt = lat_prev is None
    cc, ss = _rope_tables(pos0 + jnp.arange(L))
    if L % tm != 0:
        cc, ss = jnp.tile(cc, (nb, 1)), jnp.tile(ss, (nb, 1))
    proj_w = (w["win"], w["qg"], w["wuq"], w["kvg"], w["wukv"])
    ssd_w = (w["cw"], w["cb"], w["dtb"], w["alog"], w["dsk"], w["ng"])
    if prompt:
        assert L % tm == 0 and tm % SSD_Q == 0
        lat, kr, q, kcat, v, y, nconv, hnew = _in_proj_ssd(x1, cc, ss, *proj_w, *ssd_w, tm=tm, seq_len=L)
        o = _attn_prompt(q, kcat, v, nb=nb, seq_len=L)
    else:
        z, xbc, dt, lat, kr, q = _in_proj(x1, cc, ss, *proj_w, tm=tm, seq_len=L, absorbed=True)
        y, nconv, hnew = _ssd(xbc, z, dt, conv_prev, h0, *ssd_w, nb=nb, seq_len=L)
        past = lat_prev.shape[1]
        o = _attn_sample(q, lat_prev.reshape(nb * past, KV_LORA), kr_prev.reshape(nb * past, QK_ROPE),
                         lat, kr, nb=nb, past_len=past, new_len=L)
    out = _mix_ffn_ln(x1, y, o, None if prompt else w["wukv"], w["wo"], g2, b2, *f2, g3, b3, tm)
    return (out.reshape(nb, L, D_MODEL), lat.reshape(nb, L, KV_LORA), kr.reshape(nb, L, QK_ROPE),
            nconv, hnew)


def kernel(x_prompt, x_sample, cache_latent, cache_k_rope, state_conv, state_ssm, ln1_g, ln1_b, ffn1_w_gate, ffn1_w_up, ffn1_w_down, w_in, conv_w, conv_b, dt_bias, a_log, d_skip, ssd_norm_g, q_norm_g, w_uq, kv_norm_g, w_ukv, w_out, ln2_g, ln2_b, ffn2_w_gate, ffn2_w_up, ffn2_w_down, ln3_g, ln3_b):
    assert w_in.shape[0] == DEPTH == 1
    l = 0
    w = _pack_weights(w_in[l], conv_w[l], conv_b[l], dt_bias[l], a_log[l], d_skip[l],
                      ssd_norm_g[l], q_norm_g[l], w_uq[l], kv_norm_g[l], w_ukv[l], w_out[l])
    f1 = (ffn1_w_gate[l].astype(BF16), ffn1_w_up[l].astype(BF16), ffn1_w_down[l].astype(BF16))
    row = lambda v: v[l].reshape(1, D_MODEL)
    lns = ((row(ln1_g), row(ln1_b)), (row(ln2_g), row(ln2_b)), (row(ln3_g), row(ln3_b)))
    (g1, b1) = lns[0]

    nbp, lp, _ = x_prompt.shape
    nbs, ls, _ = x_sample.shape
    x1_p, cast = _ffn_ln(x_prompt.reshape(nbp * lp, D_MODEL), *f1, g1, b1, min(TOKEN_TILE, nbp * lp),
                         casts=[ffn2_w_gate, ffn2_w_up, ffn2_w_down, w["win"], w["wuq"], w_ukv, w_out])
    f2 = cast[:3]
    w.update(zip(("win", "wuq", "wukv", "wo"), cast[3:]))
    x1_s, _ = _ffn_ln(x_sample.reshape(nbs * ls, D_MODEL), *f1, g1, b1, min(TOKEN_TILE, nbs * ls))

    yp, lat_p, kr_p, conv_p, ssm_p = _stream(x1_p, nbp, lp, None, None, None, None, 0, w, f2, lns)
    past = cache_latent.shape[2]
    ys, lat_s, kr_s, conv_s, ssm_s = _stream(x1_s, nbs, ls, state_conv[l], state_ssm[l], cache_latent[l],
                                             cache_k_rope[l], past, w, f2, lns)
    st = lambda a: a[None]
    return (yp, ys, st(lat_p), st(kr_p), st(conv_p), st(ssm_p),
            st(lat_s), st(kr_s), st(conv_s), st(ssm_s))
```

```python
import functools

import jax
import jax.numpy as jnp
from jax import lax
from jax.experimental import pallas as pl
from jax.experimental.pallas import tpu as pltpu

F32 = jnp.float32
BF16 = jnp.bfloat16

D_MODEL = 1024
D_FF = 2816
CHUNK = 64
SSD_HEADS = 16
SSD_HEAD_DIM = 64
SSD_D_INNER = SSD_HEADS * SSD_HEAD_DIM
SSD_GROUPS = 2
SSD_HEADS_PER_GROUP = SSD_HEADS // SSD_GROUPS
SSD_STATE = 128
CONV_W = 4
SSD_CONV_DIM = SSD_D_INNER + 2 * SSD_GROUPS * SSD_STATE
MLA_HEADS = 8
Q_LORA = 512
KV_LORA = 512
QK_NOPE = 128
QK_ROPE = 64
V_HEAD = 128
MLA_WIDTH = MLA_HEADS * V_HEAD
ROPE_THETA = 10000.0
DEPTH = 1
ALPHA = (2 * DEPTH) ** 0.25
EPS = 1e-5
SM_SCALE = (QK_NOPE + QK_ROPE) ** -0.5
LOG2E = 1.4426950408889634
NEG_BIG = -1e30

LANES = 128
BF16_SUBLANES = 16
QK_PAD = 2 * LANES
SSD_Q = 128
FF_CHUNK = 256
TOKEN_TILE = 512
LN_ROW_BLOCKS = 8
SIDE_LAG = 2
CAST_STEPS = 16
ATT_TQ = 1024
ATT_TK = 256
ATT_HEADS_PER_STEP = 2
VMEM_LIMIT = 56 * 1024 * 1024

_NT = (((1,), (1,)), ((), ()))


def _resident(shape):
    nd = len(shape)
    return pl.BlockSpec(shape, lambda *_: (0,) * nd, pipeline_mode=pl.Buffered(1))


def _dot(a, b):
    return jnp.dot(a, b, preferred_element_type=F32)


def _dot_nt(a, b):
    return lax.dot_general(a, b, _NT, preferred_element_type=F32)


def _sigmoid(x):
    return 1.0 / (1.0 + jnp.exp2(x * (-LOG2E)))


def _silu(x):
    return x * _sigmoid(x)


def _layer_norm(y, g, b):
    mu = jnp.mean(y, axis=-1, keepdims=True)
    d = y - mu
    var = jnp.mean(d * d, axis=-1, keepdims=True)
    return d * lax.rsqrt(var + EPS) * g + b


def _rms_norm(x, g):
    return x * lax.rsqrt(jnp.mean(x * x, axis=-1, keepdims=True) + EPS) * g


def _zero_after(values):
    t = values[0:8]
    for r in range(8, values.shape[0], 8):
        t = t + values[r:r + 8]
    u = t[:, :LANES]
    for c in range(LANES, values.shape[1], LANES):
        u = u + t[:, c:c + LANES]
    return pltpu.bitcast((pltpu.bitcast(u, jnp.uint32) >> 16) >> 16, F32)


def _ffn_residual(x, wg_ref, wu_ref, wd_ref, side=()):
    side = iter(side)
    rows = x.shape[0]
    xb = x.astype(BF16)
    acc = jnp.zeros((rows, D_MODEL), F32)
    due = {}
    for c in range(D_FF // FF_CHUNK):
        cs = slice(c * FF_CHUNK, (c + 1) * FF_CHUNK)
        h = _silu(_dot(xb, wg_ref[:, cs])) * _dot(xb, wu_ref[:, cs])
        for v in due.pop(c, ()):
            h = h + jnp.tile(_zero_after(v), (rows // 8, FF_CHUNK // LANES))
        acc = acc + _dot(h.astype(BF16), wd_ref[cs, :])
        produced = next(side, None)
        if produced is not None and c + SIDE_LAG < D_FF // FF_CHUNK:
            due[c + SIDE_LAG] = produced
    for _ in side:
        pass
    return ALPHA * x + 0.5 * acc


def _layer_norm_rows(src, g_ref, b_ref, dst_ref):
    nblk = LN_ROW_BLOCKS if dst_ref.shape[0] % (8 * LN_ROW_BLOCKS) == 0 else 1
    rows = dst_ref.shape[0] // nblk
    for r in range(nblk):
        sl = slice(r * rows, (r + 1) * rows)
        dst_ref[sl, :] = out = _layer_norm(src[sl, :], g_ref[...], b_ref[...])
        yield (out,)


def _run(gen):
    for _ in gen:
        pass


def _ffn_ln_kernel(x_ref, wg_ref, wu_ref, wd_ref, g_ref, b_ref, *refs):
    n_cast = (len(refs) - 2) // 2
    cast_in, o_ref, cast_out, pre_ref = refs[:n_cast], refs[n_cast], refs[n_cast + 1:-1], refs[-1]
    i = pl.program_id(0)
    nt = pl.num_programs(0) - 1

    def norm_prev():
        return _layer_norm_rows(pre_ref, g_ref, b_ref, o_ref)

    @pl.when(i == 0)
    def _():
        pre_ref[...] = jnp.zeros(pre_ref.shape, F32)

    @pl.when(i < nt)
    def _():
        pre = _ffn_residual(x_ref[...], wg_ref, wu_ref, wd_ref, norm_prev())
        pre_ref[...] = pre
        for src, dst in zip(cast_in, cast_out):
            dst[...] = src[...].astype(BF16)

    @pl.when(i == nt)
    def _():
        _run(norm_prev())


def _cast_row_block(rows, steps):
    units = rows // BF16_SUBLANES
    nblk = max(d for d in range(1, min(units, steps) + 1) if units % d == 0)
    return rows // nblk, nblk


def _cast_specs(arrays, steps):
    cast_in, cast_out = [], []
    for a in arrays:
        rows, cols = a.shape[-2:]
        rpb, nblk = _cast_row_block(rows, steps)
        row_blk = lambda i, last=nblk - 1: jnp.minimum(i, last)
        cast_out.append(pl.BlockSpec((rpb, cols), lambda i, f=row_blk: (f(i), 0)))
        cast_in.append(cast_out[-1] if a.ndim == 2 else
                       pl.BlockSpec((None, rpb, cols), lambda i, f=row_blk: (0, f(i), 0)))
    return cast_in, cast_out


def _cast_kernel(*refs):
    n = len(refs) // 2
    for src, dst in zip(refs[:n], refs[n:]):
        dst[...] = src[...].astype(BF16)


def _cast_bf16(arrays):
    cast_in, cast_out = _cast_specs(arrays, CAST_STEPS)
    return pl.pallas_call(
        _cast_kernel,
        grid=(CAST_STEPS,),
        in_specs=cast_in,
        out_specs=cast_out,
        out_shape=[jax.ShapeDtypeStruct(a.shape[-2:], BF16) for a in arrays],
        compiler_params=pltpu.CompilerParams(dimension_semantics=("arbitrary",),
                                             vmem_limit_bytes=VMEM_LIMIT),
        name="cast_bf16",
    )(*arrays)


def _ffn_ln(x, wg, wu, wd, g, b, tm, casts=()):
    T = x.shape[0]
    nt = T // tm
    cast_in, cast_out = _cast_specs(casts, nt)
    out = pl.pallas_call(
        _ffn_ln_kernel,
        grid=(nt + 1,),
        in_specs=[pl.BlockSpec((tm, D_MODEL), lambda i: (jnp.minimum(i, nt - 1), 0)),
                  _resident(wg.shape), _resident(wu.shape), _resident(wd.shape),
                  _resident(g.shape), _resident(b.shape)] + cast_in,
        out_specs=[pl.BlockSpec((tm, D_MODEL), lambda i: (jnp.maximum(i - 1, 0), 0))] + cast_out,
        out_shape=[jax.ShapeDtypeStruct((T, D_MODEL), F32)]
        + [jax.ShapeDtypeStruct(a.shape[-2:], BF16) for a in casts],
        scratch_shapes=[pltpu.VMEM((tm, D_MODEL), F32)],
        compiler_params=pltpu.CompilerParams(dimension_semantics=("arbitrary",),
                                             vmem_limit_bytes=VMEM_LIMIT),
        name="ffn_ln",
    )(x, wg, wu, wd, g, b, *casts)
    return out[0], out[1:]


def _mix_ffn_ln_kernel(x1_ref, y_ref, o_ref, *refs, absorbed):
    if absorbed:
        wukv_ref, *refs = refs
    wo_ref, g2_ref, b2_ref, wg_ref, wu_ref, wd_ref, g3_ref, b3_ref, out_ref, pre3_ref = refs
    i = pl.program_id(0)
    nt = pl.num_programs(0) - 1

    def mix_residual():
        if absorbed:
            kv_w = QK_NOPE + V_HEAD
            o = jnp.concatenate([_dot(o_ref[h], wukv_ref[:, h * kv_w + QK_NOPE:(h + 1) * kv_w])
                                 for h in range(MLA_HEADS)], axis=1).astype(BF16)
        else:
            o = o_ref[...]
        mix = _dot(y_ref[...], wo_ref[:SSD_D_INNER, :]) + _dot(o, wo_ref[SSD_D_INNER:, :])
        return ALPHA * x1_ref[...] + mix

    def norm_prev():
        return _layer_norm_rows(pre3_ref, g3_ref, b3_ref, out_ref)

    @pl.when(i == 0)
    def _():
        pre3_ref[...] = jnp.zeros(pre3_ref.shape, F32)

    @pl.when(i < nt)
    def _():
        x2 = _layer_norm(mix_residual(), g2_ref[...], b2_ref[...])
        pre3 = _ffn_residual(x2, wg_ref, wu_ref, wd_ref, norm_prev())
        pre3_ref[...] = pre3

    @pl.when(i == nt)
    def _():
        _run(norm_prev())


def _mix_ffn_ln(x1, y, o, wukv, wo, g2, b2, wg, wu, wd, g3, b3, tm):
    T = x1.shape[0]
    nt = T // tm
    cur = lambda i: jnp.minimum(i, nt - 1)
    tok = pl.BlockSpec((tm, D_MODEL), lambda i: (cur(i), 0))
    absorbed = wukv is not None
    o_spec = pl.BlockSpec((MLA_HEADS, tm, KV_LORA), lambda i: (0, cur(i), 0)) if absorbed else tok
    weights = ([wukv] if absorbed else []) + [wo, g2, b2, wg, wu, wd, g3, b3]
    return pl.pallas_call(
        functools.partial(_mix_ffn_ln_kernel, absorbed=absorbed),
        grid=(nt + 1,),
        in_specs=[tok, tok, o_spec] + [_resident(w.shape) for w in weights],
        out_specs=pl.BlockSpec((tm, D_MODEL), lambda i: (jnp.maximum(i - 1, 0), 0)),
        out_shape=jax.ShapeDtypeStruct((T, D_MODEL), F32),
        scratch_shapes=[pltpu.VMEM((tm, D_MODEL), F32)],
        compiler_params=pltpu.CompilerParams(dimension_semantics=("arbitrary",),
                                             vmem_limit_bytes=VMEM_LIMIT),
        name="mix_ffn_ln",
    )(x1, y, o, *weights)


_Z0, _XBC0 = 0, SSD_D_INNER
_CQ0, _CKV0, _DT0, _KR0, _TAIL_COLS = 0, 512, 1024, 1152, 1280


def _rope_tail(rs, cc, ss):
    return rs * cc + pltpu.roll(rs, 64, 1) * ss


def _in_proj_units(x_ref, cc_ref, ss_ref, win_ref, wtail_ref, qg_ref, wuq_ref, kvg_ref, wukv_ref, *refs,
                   absorbed):
    if absorbed:
        z_ref, xbc_ref, dt_ref, lat_ref, kr_ref, q_ref = refs
    else:
        z_ref, xbc_ref, dt_ref, lat_ref, kr_ref, q_ref, k_ref, v_ref = refs
    wide = 2 * LANES
    kv_w = QK_NOPE + V_HEAD
    xb = x_ref[...].astype(BF16)

    def proj(c0, c1):
        return _dot(xb, win_ref[:, c0:c1])

    def proj_tail(c0, c1):
        return _dot(xb, wtail_ref[:, c0:c1])

    for c in range(0, SSD_D_INNER, wide):
        z_ref[:, c:c + wide] = proj(_Z0 + c, _Z0 + c + wide)
        yield
    dk = proj_tail(_DT0, _TAIL_COLS)
    dt_ref[...] = dk[:, :LANES]
    yield
    cq = _rms_norm(proj_tail(_CQ0, _CKV0), qg_ref[...]).astype(BF16)
    yield
    for c in range(0, SSD_CONV_DIM, wide):
        xbc_ref[:, c:c + wide] = proj(_XBC0 + c, _XBC0 + c + wide)
        yield
    for h in range(MLA_HEADS):
        qh = _dot(cq, wuq_ref[:, h * QK_PAD:(h + 1) * QK_PAD])
        q_nope = qh[:, :LANES].astype(BF16)
        if absorbed:
            q_nope = _dot_nt(q_nope, wukv_ref[:, h * kv_w:h * kv_w + QK_NOPE]).astype(BF16)
        q_ref[h, :, :q_nope.shape[1]] = q_nope
        q_ref[h, :, q_nope.shape[1]:] = _rope_tail(qh[:, LANES:], cc_ref[...], ss_ref[...]).astype(BF16)
        yield
    lat = _rms_norm(proj_tail(_CKV0, _DT0), kvg_ref[...])
    lat_ref[...] = lat
    kr_tail = _rope_tail(dk[:, LANES:], cc_ref[...], ss_ref[...])
    kr_ref[...] = kr_tail[:, :QK_ROPE]
    yield
    if absorbed:
        return
    latb = lat.astype(BF16)
    kr_tail = kr_tail.astype(BF16)
    for h in range(MLA_HEADS):
        kv = _dot(latb, wukv_ref[:, h * kv_w:(h + 1) * kv_w])
        k_ref[h, :, :LANES] = kv[:, :QK_NOPE].astype(BF16)
        k_ref[h, :, LANES:] = kr_tail
        v_ref[:, h * V_HEAD:(h + 1) * V_HEAD] = kv[:, QK_NOPE:].astype(BF16)
        yield


def _in_proj_kernel(*refs, absorbed):
    for _ in _in_proj_units(*refs, absorbed=absorbed):
        pass


def _in_proj(x1, cc, ss, win, wtail, qg, wuq, kvg, wukv, *, tm, seq_len, absorbed):
    T = x1.shape[0]
    nt = T // tm
    tok = lambda w: pl.BlockSpec((tm, w), lambda i: (i, 0))
    if cc.shape[0] == T:
        tab = pl.BlockSpec((tm, LANES), lambda i: (i, 0))
    else:
        per_seq = seq_len // tm
        tab = pl.BlockSpec((tm, LANES), lambda i: (i % per_seq, 0))
    q_width = KV_LORA + LANES if absorbed else QK_PAD
    head_tok = lambda w: pl.BlockSpec((MLA_HEADS, tm, w), lambda i: (0, i, 0))
    head_shape = lambda w: jax.ShapeDtypeStruct((MLA_HEADS, T, w), BF16)
    out_shape = [
        jax.ShapeDtypeStruct((T, SSD_D_INNER), F32),
        jax.ShapeDtypeStruct((T, SSD_CONV_DIM), F32),
        jax.ShapeDtypeStruct((T, LANES), F32),
        jax.ShapeDtypeStruct((T, KV_LORA), F32),
        jax.ShapeDtypeStruct((T, QK_ROPE), F32),
        head_shape(q_width),
    ]
    out_specs = [tok(SSD_D_INNER), tok(SSD_CONV_DIM), tok(LANES), tok(KV_LORA), tok(QK_ROPE),
                 head_tok(q_width)]
    weights = [win, wtail, qg, wuq, kvg, wukv]
    if not absorbed:
        out_shape += [head_shape(QK_PAD), jax.ShapeDtypeStruct((T, MLA_WIDTH), BF16)]
        out_specs += [head_tok(QK_PAD), tok(MLA_WIDTH)]
    return pl.pallas_call(
        functools.partial(_in_proj_kernel, absorbed=absorbed),
        grid=(nt,),
        in_specs=[tok(D_MODEL), tab, tab] + [_resident(w.shape) for w in weights],
        out_specs=out_specs,
        out_shape=out_shape,
        compiler_params=pltpu.CompilerParams(dimension_semantics=("parallel",),
                                             vmem_limit_bytes=VMEM_LIMIT),
        name="in_proj",
    )(x1, cc, ss, *weights)


def _split3(x):
    hi = x.astype(BF16)
    r1 = x - hi.astype(F32)
    mid = r1.astype(BF16)
    lo = (r1 - mid.astype(F32)).astype(BF16)
    return hi, mid, lo


def _softplus(x):
    return jnp.maximum(x, 0.0) + jnp.log1p(jnp.exp(-jnp.abs(x)))


def _ssd_block(out, xbc, hist, z, dt_raw, h_in, lb, cw_ref, cb_ref, dtb_ref, alog_ref, dsk_ref, ng_ref):
    Q = SSD_Q
    if lb < Q:
        xbc = jnp.concatenate([xbc, jnp.zeros((Q - lb, SSD_CONV_DIM), F32)], axis=0)
        dt_raw = jnp.concatenate([dt_raw, jnp.zeros((Q - lb, LANES), F32)], axis=0)
    xw = jnp.concatenate([hist, xbc], axis=0)
    conv = xw * cw_ref[0:1, :]
    for k in range(1, CONV_W):
        conv = pltpu.roll(conv, 1, 0) + xw * cw_ref[k:k + 1, :]
    conv = conv[8:, :] + cb_ref[...]
    xc = _silu(conv)
    xs = xc[:, :SSD_D_INNER]
    yield

    row_id = lax.broadcasted_iota(jnp.int32, (Q, LANES), 0)
    dt_col = jnp.where(row_id < lb, _softplus(dt_raw + dtb_ref[...]), 0.0)
    da_col = dt_col * (-jnp.exp(alog_ref[...]))
    ii = lax.broadcasted_iota(jnp.int32, (Q, Q), 0)
    jj = lax.broadcasted_iota(jnp.int32, (Q, Q), 1)
    causal = ii >= jj
    tri = causal.astype(BF16)
    a_col = sum(_dot(tri, p) for p in _split3(da_col)) * LOG2E
    a_row = a_col.T
    dt_row = dt_col.T
    e_col = jnp.exp2(a_col)
    a_last = a_row[:, Q - 1:Q]
    w_row = dt_row * jnp.exp2(a_last - a_row)
    e_last = jnp.exp2(jnp.broadcast_to(a_last, (LANES, LANES)))

    xs_t = xs.T
    yield
    lane = lax.broadcasted_iota(jnp.int32, (Q, LANES), 1)
    lo_half = lane < SSD_HEAD_DIM
    y_parts = []
    h_out = []
    for g in range(SSD_GROUPS):
        b_g = xc[:, SSD_D_INNER + g * SSD_STATE:SSD_D_INNER + (g + 1) * SSD_STATE].astype(BF16)
        c0 = SSD_D_INNER + SSD_GROUPS * SSD_STATE + g * SSD_STATE
        c_g = xc[:, c0:c0 + SSD_STATE].astype(BF16)
        cb = _dot_nt(c_g, b_g)
        h0 = g * SSD_HEADS_PER_GROUP
        hp = h_in[h0:h0 + SSD_HEADS_PER_GROUP].reshape(SSD_HEADS_PER_GROUP * SSD_HEAD_DIM, SSD_STATE)
        y_off = _dot_nt(c_g, hp.astype(BF16))
        lhs = []
        for hh in range(SSD_HEADS_PER_GROUP):
            h = h0 + hh
            lhs.append(xs_t[h * SSD_HEAD_DIM:(h + 1) * SSD_HEAD_DIM, :] * w_row[h:h + 1, :])
        st = _dot(jnp.concatenate(lhs, axis=0).astype(BF16), b_g)
        for hh in range(SSD_HEADS_PER_GROUP):
            h = h0 + hh
            h_out.append(e_last[h:h + 1, :] * h_in[h]
                         + st[hh * SSD_HEAD_DIM:(hh + 1) * SSD_HEAD_DIM, :])
        yield
        for pr in range(SSD_HEADS_PER_GROUP // 2):
            ha = h0 + 2 * pr
            ms = []
            for h in (ha, ha + 1):
                seg = a_col[:, h:h + 1] - a_row[h:h + 1, :]
                m = jnp.exp2(jnp.where(causal, seg, -jnp.inf)) * cb * dt_row[h:h + 1, :]
                ms.append(m.astype(BF16))
            xp = xs[:, ha * SSD_HEAD_DIM:(ha + 2) * SSD_HEAD_DIM]
            rhs = jnp.concatenate([jnp.where(lo_half, xp, 0.0), jnp.where(lo_half, 0.0, xp)],
                                  axis=0).astype(BF16)
            y_d = _dot(jnp.concatenate(ms, axis=1), rhs)
            yo = y_off[:, 2 * pr * SSD_HEAD_DIM:(2 * pr + 2) * SSD_HEAD_DIM]
            dec = jnp.where(lo_half, e_col[:, ha:ha + 1], e_col[:, ha + 1:ha + 2])
            y_parts.append(y_d + yo * dec)
            if pr % 2 == 1:
                yield
    y = jnp.concatenate(y_parts, axis=1) + dsk_ref[...] * xs
    y = y[:lb] * _silu(z)
    half = SSD_D_INNER // SSD_GROUPS
    ng = ng_ref[...]
    y = jnp.concatenate([_rms_norm(y[:, g * half:(g + 1) * half], ng[:, g * half:(g + 1) * half])
                         for g in range(SSD_GROUPS)], axis=1)
    out["y"] = y
    out["h"] = jnp.stack(h_out)


def _ssd_kernel(*refs, lb, zero_init):
    if zero_init:
        xbc_ref, z_ref, dt_ref, *params, y_ref, nconv_ref, hnew_ref, tail_ref, h_ref = refs
        cprev_ref = h0_ref = None
    else:
        (xbc_ref, z_ref, dt_ref, cprev_ref, h0_ref, *params,
         y_ref, nconv_ref, hnew_ref, tail_ref, h_ref) = refs
    j = pl.program_id(1)

    @pl.when(j == 0)
    def _():
        tail_ref[...] = jnp.zeros(tail_ref.shape, F32)
        if zero_init:
            h_ref[...] = jnp.zeros(h_ref.shape, F32)
        else:
            tail_ref[8 - (CONV_W - 1):, :] = cprev_ref[0]
            h_ref[...] = h0_ref[0]

    out = {}
    for _ in _ssd_block(out, xbc_ref[...], tail_ref[...], z_ref[...], dt_ref[...], h_ref[...], lb, *params):
        pass
    y_ref[...] = out["y"].astype(BF16)
    h_ref[...] = out["h"]
    tail_ref[...] = xbc_ref[lb - 8:lb, :]

    @pl.when(j == pl.num_programs(1) - 1)
    def _():
        nconv_ref[0] = xbc_ref[lb - (CONV_W - 1):lb, :]
        hnew_ref[0] = h_ref[...]


def _in_proj_ssd_kernel(x_ref, cc_ref, ss_ref, win_w, wtail, qg, wuq, kvg, wukv, cw, cb, dtb, alog, dsk, ng,
                        lat_ref, kr_ref, q_ref, k_ref, v_ref, y_ref, nconv_ref, hnew_ref,
                        zs_ref, xs_ref, dts_ref, tail_ref, h_ref, *, tiles_per_seq):
    i = pl.program_id(0)
    slot = i % 2
    tm = x_ref.shape[0]

    @pl.when(i == 0)
    def _():
        zs_ref[1] = jnp.zeros(zs_ref.shape[1:], F32)
        xs_ref[1] = jnp.zeros(xs_ref.shape[1:], F32)
        dts_ref[1] = jnp.zeros(dts_ref.shape[1:], F32)
        tail_ref[...] = jnp.zeros(tail_ref.shape, F32)
        h_ref[...] = jnp.zeros(h_ref.shape, F32)

    proj_units = _in_proj_units(x_ref, cc_ref, ss_ref, win_w, wtail, qg, wuq, kvg, wukv,
                                zs_ref.at[slot], xs_ref.at[slot], dts_ref.at[slot],
                                lat_ref, kr_ref, q_ref, k_ref, v_ref, absorbed=False)

    prev = 1 - slot
    fresh = (i - 1) % tiles_per_seq == 0
    hist = jnp.where(fresh, 0.0, tail_ref[...])
    h = jnp.where(fresh, 0.0, h_ref[...])
    for sb in range(tm // SSD_Q):
        r0 = sb * SSD_Q
        if sb > 0:
            hist = xs_ref[prev, r0 - 8:r0, :]
        out = {}
        for _ in _ssd_block(out, xs_ref[prev, r0:r0 + SSD_Q, :], hist, zs_ref[prev, r0:r0 + SSD_Q, :],
                            dts_ref[prev, r0:r0 + SSD_Q, :], h, SSD_Q, cw, cb, dtb, alog, dsk, ng):
            next(proj_units, None)
        y_ref[r0:r0 + SSD_Q, :] = out["y"].astype(BF16)
        h = out["h"]
    for _ in proj_units:
        pass
    tail_ref[...] = xs_ref[prev, tm - 8:tm, :]
    h_ref[...] = h
    nconv_ref[0] = xs_ref[prev, tm - (CONV_W - 1):tm, :]
    hnew_ref[0] = h


def _in_proj_ssd(x1, cc, ss, win, wtail, qg, wuq, kvg, wukv, cw, cb, dtb, alog, dsk, ng, *, tm, seq_len):
    T = x1.shape[0]
    nt = T // tm
    per_seq = seq_len // tm
    nb = T // seq_len
    cur = lambda i: jnp.minimum(i, nt - 1)
    old = lambda i: jnp.maximum(i - 1, 0)
    tok = lambda w: pl.BlockSpec((tm, w), lambda i: (cur(i), 0))
    tab = pl.BlockSpec((tm, LANES), lambda i: (cur(i) % per_seq, 0))
    head_tok = pl.BlockSpec((MLA_HEADS, tm, QK_PAD), lambda i: (0, cur(i), 0))
    head_shape = jax.ShapeDtypeStruct((MLA_HEADS, T, QK_PAD), BF16)
    weights = [win, wtail, qg, wuq, kvg, wukv, cw, cb, dtb, alog, dsk, ng]
    return pl.pallas_call(
        functools.partial(_in_proj_ssd_kernel, tiles_per_seq=per_seq),
        grid=(nt + 1,),
        in_specs=[tok(D_MODEL), tab, tab] + [_resident(w.shape) for w in weights],
        out_specs=[tok(KV_LORA),
                   pl.BlockSpec((None, tm, QK_ROPE), lambda i: (cur(i) // per_seq, cur(i) % per_seq, 0)),
                   head_tok, head_tok, tok(MLA_WIDTH),
                   pl.BlockSpec((tm, SSD_D_INNER), lambda i: (old(i), 0)),
                   pl.BlockSpec((1, CONV_W - 1, SSD_CONV_DIM), lambda i: (old(i) // per_seq, 0, 0)),
                   pl.BlockSpec((1, SSD_HEADS, SSD_HEAD_DIM, SSD_STATE),
                                lambda i: (old(i) // per_seq, 0, 0, 0))],
        out_shape=[jax.ShapeDtypeStruct((T, KV_LORA), F32),
                   jax.ShapeDtypeStruct((nb, seq_len, QK_ROPE), F32),
                   head_shape, head_shape,
                   jax.ShapeDtypeStruct((T, MLA_WIDTH), BF16),
                   jax.ShapeDtypeStruct((T, SSD_D_INNER), BF16),
                   jax.ShapeDtypeStruct((nb, CONV_W - 1, SSD_CONV_DIM), F32),
                   jax.ShapeDtypeStruct((nb, SSD_HEADS, SSD_HEAD_DIM, SSD_STATE), F32)],
        scratch_shapes=[pltpu.VMEM((2, tm, SSD_D_INNER), F32),
                        pltpu.VMEM((2, tm, SSD_CONV_DIM), F32),
                        pltpu.VMEM((2, tm, LANES), F32),
                        pltpu.VMEM((8, SSD_CONV_DIM), F32),
                        pltpu.VMEM((SSD_HEADS, SSD_HEAD_DIM, SSD_STATE), F32)],
        compiler_params=pltpu.CompilerParams(dimension_semantics=("arbitrary",),
                                             vmem_limit_bytes=VMEM_LIMIT),
        name="in_proj_ssd",
    )(x1, cc, ss, *weights)


def _ssd(xbc, z, dt, cprev, h0, cw, cb, dtb, alog, dsk, ng, *, nb, seq_len):
    lb = min(SSD_Q, seq_len)
    nblk = seq_len // lb
    zero_init = cprev is None
    tokb = lambda w: pl.BlockSpec((lb, w), lambda b, j: (b * nblk + j, 0))
    in_specs = [tokb(SSD_CONV_DIM), tokb(SSD_D_INNER), tokb(LANES)]
    args = [xbc, z, dt]
    if not zero_init:
        in_specs += [pl.BlockSpec((1, CONV_W - 1, SSD_CONV_DIM), lambda b, j: (b, 0, 0)),
                     pl.BlockSpec((1, SSD_HEADS, SSD_HEAD_DIM, SSD_STATE), lambda b, j: (b, 0, 0, 0))]
        args += [cprev, h0]
    params = [cw, cb, dtb, alog, dsk, ng]
    in_specs += [pl.BlockSpec(p.shape, lambda b, j: (0, 0)) for p in params]
    args += params
    return pl.pallas_call(
        functools.partial(_ssd_kernel, lb=lb, zero_init=zero_init),
        grid=(nb, nblk),
        in_specs=in_specs,
        out_specs=(tokb(SSD_D_INNER),
                   pl.BlockSpec((1, CONV_W - 1, SSD_CONV_DIM), lambda b, j: (b, 0, 0)),
                   pl.BlockSpec((1, SSD_HEADS, SSD_HEAD_DIM, SSD_STATE), lambda b, j: (b, 0, 0, 0))),
        out_shape=(jax.ShapeDtypeStruct((nb * seq_len, SSD_D_INNER), BF16),
                   jax.ShapeDtypeStruct((nb, CONV_W - 1, SSD_CONV_DIM), F32),
                   jax.ShapeDtypeStruct((nb, SSD_HEADS, SSD_HEAD_DIM, SSD_STATE), F32)),
        scratch_shapes=[pltpu.VMEM((8, SSD_CONV_DIM), F32),
                        pltpu.VMEM((SSD_HEADS, SSD_HEAD_DIM, SSD_STATE), F32)],
        compiler_params=pltpu.CompilerParams(dimension_semantics=("arbitrary", "arbitrary"),
                                             vmem_limit_bytes=VMEM_LIMIT),
        name="ssd",
    )(*args)


def _lane_tile(x, width):
    return jnp.concatenate([x] * (width // LANES), axis=1)


def _attn_prompt_kernel(q_ref, k_ref, v_ref, o_ref, vx_ref, m_ref, acc_ref, *, seq_len, tq, tk, nh):
    ri = lax.broadcasted_iota(jnp.int32, (tk, tk), 0)
    ci = lax.broadcasted_iota(jnp.int32, (tk, tk), 1)
    diag_mask = (ci // CHUNK) <= (ri // CHUNK)

    def head_steps(hh):
        cols = slice(hh * V_HEAD, (hh + 1) * V_HEAD)
        vx_ref[hh, :, :V_HEAD] = v_ref[:, cols]
        vx_ref[hh, :, V_HEAD:] = jnp.ones((seq_len, LANES), BF16)

        def step(r0, k0, row0, masked):
            s = _dot_nt(q_ref[hh, r0 + row0:r0 + tq, :], k_ref[hh, k0:k0 + tk, :])
            if masked:
                top = jnp.where(diag_mask, s[:tk], NEG_BIG)
                s = top if tq - row0 == tk else jnp.concatenate([top, s[tk:]], axis=0)
            m_prev = m_ref[hh, row0:, :]
            m_next = jnp.maximum(m_prev, jnp.max(s, axis=-1, keepdims=True))
            alpha = jnp.exp2(m_prev - m_next)
            p = jnp.exp2(s - _lane_tile(m_next, tk))
            m_ref[hh, row0:, :] = m_next
            acc_ref[hh, row0:, :] = (_lane_tile(alpha, V_HEAD + LANES) * acc_ref[hh, row0:, :]
                                     + _dot(p.astype(BF16), vx_ref[hh, k0:k0 + tk, :]))

        for qb in range(seq_len // tq):
            r0 = qb * tq
            m_ref[hh] = jnp.full(m_ref.shape[1:], NEG_BIG, F32)
            acc_ref[hh] = jnp.zeros(acc_ref.shape[1:], F32)
            for j in range(r0 // tk):
                step(r0, j * tk, 0, False)
                yield
            for d in range(tq // tk):
                step(r0, r0 + d * tk, d * tk, True)
                yield
            o_ref[r0:r0 + tq, cols] = (acc_ref[hh, :, :V_HEAD] / acc_ref[hh, :, V_HEAD:]).astype(BF16)

    active = [head_steps(hh) for hh in range(nh)]
    while active:
        active = [g for g in active if next(g, True) is None]


def _attn_prompt(q, k, v, *, nb, seq_len):
    tq = min(ATT_TQ, seq_len)
    tk = min(ATT_TK, tq)
    nh = ATT_HEADS_PER_STEP
    scratch = [pltpu.VMEM((nh, seq_len, V_HEAD + LANES), BF16), pltpu.VMEM((nh, tq, LANES), F32),
               pltpu.VMEM((nh, tq, V_HEAD + LANES), F32)]
    return pl.pallas_call(
        functools.partial(_attn_prompt_kernel, seq_len=seq_len, tq=tq, tk=tk, nh=nh),
        grid=(nb, MLA_HEADS // nh),
        in_specs=[pl.BlockSpec((nh, seq_len, QK_PAD), lambda b, h: (h, b, 0)),
                  pl.BlockSpec((nh, seq_len, QK_PAD), lambda b, h: (h, b, 0)),
                  pl.BlockSpec((seq_len, nh * V_HEAD), lambda b, h: (b, h))],
        out_specs=pl.BlockSpec((seq_len, nh * V_HEAD), lambda b, h: (b, h)),
        out_shape=jax.ShapeDtypeStruct((nb * seq_len, MLA_WIDTH), BF16),
        scratch_shapes=scratch,
        compiler_params=pltpu.CompilerParams(dimension_semantics=("parallel", "parallel"),
                                             vmem_limit_bytes=VMEM_LIMIT),
        name="attn_prompt",
    )(q, k, v)


def _attn_sample_kernel(q_ref, latc_ref, krc_ref, latn_ref, krn_ref, o_ref, *, past_len, new_len):
    rows = MLA_HEADS * new_len
    qa = jnp.concatenate([q_ref[h] for h in range(MLA_HEADS)], axis=0)
    q_lat = qa[:, :KV_LORA]
    q_rope = qa[:, KV_LORA:KV_LORA + QK_ROPE]
    latc = latc_ref[...].astype(BF16)
    latn = latn_ref[...].astype(BF16)
    sc = _dot_nt(q_lat, latc) + _dot_nt(q_rope, krc_ref[...].astype(BF16))
    sn = _dot_nt(q_lat, latn) + _dot_nt(q_rope, krn_ref[...].astype(BF16))
    qpos = past_len + lax.broadcasted_iota(jnp.int32, (rows, 1), 0) % new_len
    kpos_c = lax.broadcasted_iota(jnp.int32, (1, past_len), 1)
    kpos_n = past_len + lax.broadcasted_iota(jnp.int32, (1, new_len), 1)
    sc = jnp.where((kpos_c // CHUNK) <= (qpos // CHUNK), sc, NEG_BIG)
    sn = jnp.where((kpos_n // CHUNK) <= (qpos // CHUNK), sn, NEG_BIG)
    m = jnp.maximum(jnp.max(sc, axis=-1, keepdims=True), jnp.max(sn, axis=-1, keepdims=True))
    pc = jnp.exp2(sc - m)
    pn = jnp.exp2(sn - m)
    l = jnp.sum(pc, axis=-1, keepdims=True) + jnp.sum(pn, axis=-1, keepdims=True)
    o = (_dot(pc.astype(BF16), latc) + _dot(pn.astype(BF16), latn)) / l
    for h in range(MLA_HEADS):
        o_ref[h] = o[h * new_len:(h + 1) * new_len].astype(BF16)


def _attn_sample(qa, lat_c, kr_c, lat_n, kr_n, *, nb, past_len, new_len):
    return pl.pallas_call(
        functools.partial(_attn_sample_kernel, past_len=past_len, new_len=new_len),
        grid=(nb,),
        in_specs=[pl.BlockSpec((MLA_HEADS, new_len, KV_LORA + LANES), lambda b: (0, b, 0)),
                  pl.BlockSpec((None, past_len, KV_LORA), lambda b: (b, 0, 0)),
                  pl.BlockSpec((None, past_len, QK_ROPE), lambda b: (b, 0, 0)),
                  pl.BlockSpec((new_len, KV_LORA), lambda b: (b, 0)),
                  pl.BlockSpec((new_len, QK_ROPE), lambda b: (b, 0))],
        out_specs=pl.BlockSpec((MLA_HEADS, new_len, KV_LORA), lambda b: (0, b, 0)),
        out_shape=jax.ShapeDtypeStruct((MLA_HEADS, nb * new_len, KV_LORA), BF16),
        compiler_params=pltpu.CompilerParams(dimension_semantics=("parallel",),
                                             vmem_limit_bytes=VMEM_LIMIT),
        name="attn_sample",
    )(qa, lat_c, kr_c, lat_n, kr_n)


def _rope_tables(pos):
    inv = ROPE_THETA ** (-jnp.arange(0, QK_ROPE, 2, dtype=F32) / QK_ROPE)
    ang = pos.astype(F32)[:, None] * inv[None, :]
    cos, sin = jnp.cos(ang), jnp.sin(ang)
    pad = jnp.zeros((pos.shape[0], LANES - QK_ROPE), F32)
    return (jnp.concatenate([cos, cos, pad], axis=1), jnp.concatenate([-sin, sin, pad], axis=1))


def _pack_weights(w_in, conv_w, conv_b, dt_bias, a_log, d_skip, ssd_norm_g, q_norm_g, w_uq,
                  kv_norm_g, w_ukv, w_out):
    i0 = SSD_D_INNER
    i1 = i0 + SSD_CONV_DIM
    i2 = i1 + SSD_HEADS
    i3 = i2 + Q_LORA
    i4 = i3 + KV_LORA
    w_dt = jnp.pad(w_in[:, i1:i2], ((0, 0), (0, LANES - SSD_HEADS)))
    w_kr = w_in[:, i4:]
    wtail = jnp.concatenate([w_in[:, i2:i3], w_in[:, i3:i4], w_dt, w_kr,
                             w_kr[:, QK_ROPE // 2:], w_kr[:, :QK_ROPE // 2]], axis=1)
    uq = (w_uq * (SM_SCALE * LOG2E)).reshape(Q_LORA, MLA_HEADS, QK_NOPE + QK_ROPE)
    rope = uq[:, :, QK_NOPE:]
    wuq = jnp.concatenate([uq, rope[:, :, QK_ROPE // 2:], rope[:, :, :QK_ROPE // 2]], axis=2)
    wuq = wuq.reshape(Q_LORA, MLA_HEADS * QK_PAD)
    lane_pad = lambda v: jnp.pad(v, (0, LANES - SSD_HEADS)).reshape(1, LANES)
    return dict(
        wtail=wtail, wuq=wuq,
        qg=q_norm_g.reshape(1, Q_LORA), kvg=kv_norm_g.reshape(1, KV_LORA),
        cw=conv_w, cb=conv_b.reshape(1, SSD_CONV_DIM),
        dtb=lane_pad(dt_bias), alog=lane_pad(a_log),
        dsk=jnp.repeat(d_skip, SSD_HEAD_DIM).reshape(1, SSD_D_INNER),
        ng=ssd_norm_g.reshape(1, SSD_D_INNER),
    )


def _stream(x1, nb, L, conv_prev, h0, lat_prev, kr_prev, pos0, w, f2, lns):
    tm = min(TOKEN_TILE, nb * L)
    _, (g2, b2), (g3, b3) = lns
    prompt = lat_prev is None
    cc, ss = _rope_tables(pos0 + jnp.arange(L))
    if L % tm != 0:
        cc, ss = jnp.tile(cc, (nb, 1)), jnp.tile(ss, (nb, 1))
    proj_w = (w["win"], w["wtail"], w["qg"], w["wuq"], w["kvg"], w["wukv"])
    ssd_w = (w["cw"], w["cb"], w["dtb"], w["alog"], w["dsk"], w["ng"])
    if prompt:
        assert L % tm == 0 and tm % SSD_Q == 0
        lat, kr, q, kcat, v, y, nconv, hnew = _in_proj_ssd(x1, cc, ss, *proj_w, *ssd_w, tm=tm, seq_len=L)
        o = _attn_prompt(q, kcat, v, nb=nb, seq_len=L)
    else:
        z, xbc, dt, lat, kr, q = _in_proj(x1, cc, ss, *proj_w, tm=tm, seq_len=L, absorbed=True)
        y, nconv, hnew = _ssd(xbc, z, dt, conv_prev, h0, *ssd_w, nb=nb, seq_len=L)
        past = lat_prev.shape[1]
        o = _attn_sample(q, lat_prev, kr_prev, lat, kr, nb=nb, past_len=past, new_len=L)
    out = _mix_ffn_ln(x1, y, o, None if prompt else w["wukv"], w["wo"], g2, b2, *f2, g3, b3, tm)
    return (out.reshape(nb, L, D_MODEL), lat.reshape(nb, L, KV_LORA), kr.reshape(nb, L, QK_ROPE),
            nconv, hnew)


def kernel(x_prompt, x_sample, cache_latent, cache_k_rope, state_conv, state_ssm, ln1_g, ln1_b, ffn1_w_gate, ffn1_w_up, ffn1_w_down, w_in, conv_w, conv_b, dt_bias, a_log, d_skip, ssd_norm_g, q_norm_g, w_uq, kv_norm_g, w_ukv, w_out, ln2_g, ln2_b, ffn2_w_gate, ffn2_w_up, ffn2_w_down, ln3_g, ln3_b):
    assert w_in.shape[0] == DEPTH == 1
    l = 0
    w = _pack_weights(w_in[l], conv_w[l], conv_b[l], dt_bias[l], a_log[l], d_skip[l],
                      ssd_norm_g[l], q_norm_g[l], w_uq[l], kv_norm_g[l], w_ukv[l], w_out[l])
    f1 = _cast_bf16([ffn1_w_gate, ffn1_w_up, ffn1_w_down])
    row = lambda v: v[l].reshape(1, D_MODEL)
    lns = ((row(ln1_g), row(ln1_b)), (row(ln2_g), row(ln2_b)), (row(ln3_g), row(ln3_b)))
    (g1, b1) = lns[0]

    nbp, lp, _ = x_prompt.shape
    nbs, ls, _ = x_sample.shape
    x1_p, cast = _ffn_ln(x_prompt.reshape(nbp * lp, D_MODEL), *f1, g1, b1, min(TOKEN_TILE, nbp * lp),
                         casts=[ffn2_w_gate, ffn2_w_up, ffn2_w_down, w_in, w["wtail"], w["wuq"], w_ukv, w_out])
    f2 = cast[:3]
    w.update(zip(("win", "wtail", "wuq", "wukv", "wo"), cast[3:]))
    x1_s, _ = _ffn_ln(x_sample.reshape(nbs * ls, D_MODEL), *f1, g1, b1, min(TOKEN_TILE, nbs * ls))

    yp, lat_p, kr_p, conv_p, ssm_p = _stream(x1_p, nbp, lp, None, None, None, None, 0, w, f2, lns)
    past = cache_latent.shape[2]
    ys, lat_s, kr_s, conv_s, ssm_s = _stream(x1_s, nbs, ls, state_conv[l], state_ssm[l], cache_latent[l],
                                             cache_k_rope[l], past, w, f2, lns)
    st = lambda a: a[None]
    return (yp, ys, st(lat_p), st(kr_p), st(conv_p), st(ssm_p),
            st(lat_s), st(kr_s), st(conv_s), st(ssm_s))
```

```python
import functools

import jax
import jax.numpy as jnp
from jax import lax
from jax.experimental import pallas as pl
from jax.experimental.pallas import tpu as pltpu

F32 = jnp.float32
BF16 = jnp.bfloat16

D_MODEL = 1024
D_FF = 2816
CHUNK = 64
SSD_HEADS = 16
SSD_HEAD_DIM = 64
SSD_D_INNER = SSD_HEADS * SSD_HEAD_DIM
SSD_GROUPS = 2
SSD_HEADS_PER_GROUP = SSD_HEADS // SSD_GROUPS
SSD_STATE = 128
CONV_W = 4
SSD_CONV_DIM = SSD_D_INNER + 2 * SSD_GROUPS * SSD_STATE
MLA_HEADS = 8
Q_LORA = 512
KV_LORA = 512
QK_NOPE = 128
QK_ROPE = 64
V_HEAD = 128
MLA_WIDTH = MLA_HEADS * V_HEAD
ROPE_THETA = 10000.0
DEPTH = 1
ALPHA = (2 * DEPTH) ** 0.25
EPS = 1e-5
SM_SCALE = (QK_NOPE + QK_ROPE) ** -0.5
LOG2E = 1.4426950408889634
NEG_BIG = -1e30

LANES = 128
BF16_SUBLANES = 16
QK_PAD = 2 * LANES
SSD_Q = 128
FF_CHUNK = 256
TOKEN_TILE = 512
LN_ROW_BLOCKS = 8
SIDE_LAG = 2
CAST_STEPS = 16
SSD_SEQS_PER_STEP = 4
ATT_TQ = 1024
ATT_TK = 256
ATT_HEADS_PER_STEP = 4
VMEM_LIMIT = 56 * 1024 * 1024

_NT = (((1,), (1,)), ((), ()))


def _resident(shape):
    nd = len(shape)
    return pl.BlockSpec(shape, lambda *_: (0,) * nd, pipeline_mode=pl.Buffered(1))


def _dot(a, b):
    return jnp.dot(a, b, preferred_element_type=F32)


def _dot_nt(a, b):
    return lax.dot_general(a, b, _NT, preferred_element_type=F32)


def _sigmoid(x):
    return 1.0 / (1.0 + jnp.exp2(x * (-LOG2E)))


def _silu(x):
    return x * _sigmoid(x)


def _layer_norm(y, g, b):
    mu = jnp.mean(y, axis=-1, keepdims=True)
    d = y - mu
    var = jnp.mean(d * d, axis=-1, keepdims=True)
    return d * lax.rsqrt(var + EPS) * g + b


def _rms_norm(x, g):
    return x * lax.rsqrt(jnp.mean(x * x, axis=-1, keepdims=True) + EPS) * g


def _zero_after(values):
    t = values[0:8]
    for r in range(8, values.shape[0], 8):
        t = t + values[r:r + 8]
    u = t[:, :LANES]
    for c in range(LANES, values.shape[1], LANES):
        u = u + t[:, c:c + LANES]
    return pltpu.bitcast((pltpu.bitcast(u, jnp.uint32) >> 16) >> 16, F32)


def _ffn_residual(x, wg_ref, wu_ref, wd_ref, side=()):
    side = iter(side)
    rows = x.shape[0]
    xb = x.astype(BF16)
    acc = jnp.zeros((rows, D_MODEL), F32)
    due = {}
    for c in range(D_FF // FF_CHUNK):
        cs = slice(c * FF_CHUNK, (c + 1) * FF_CHUNK)
        h = _silu(_dot(xb, wg_ref[:, cs])) * _dot(xb, wu_ref[:, cs])
        for v in due.pop(c, ()):
            h = h + jnp.tile(_zero_after(v), (rows // 8, FF_CHUNK // LANES))
        acc = acc + _dot(h.astype(BF16), wd_ref[cs, :])
        produced = next(side, None)
        if produced is not None and c + SIDE_LAG < D_FF // FF_CHUNK:
            due[c + SIDE_LAG] = produced
    for _ in side:
        pass
    return ALPHA * x + 0.5 * acc


def _layer_norm_rows(src, g_ref, b_ref, dst_ref):
    nblk = LN_ROW_BLOCKS if dst_ref.shape[0] % (8 * LN_ROW_BLOCKS) == 0 else 1
    rows = dst_ref.shape[0] // nblk
    for r in range(nblk):
        sl = slice(r * rows, (r + 1) * rows)
        dst_ref[sl, :] = out = _layer_norm(src[sl, :], g_ref[...], b_ref[...])
        yield (out,)


def _run(gen):
    for _ in gen:
        pass


def _ffn_ln_kernel(x_ref, wg_ref, wu_ref, wd_ref, g_ref, b_ref, *refs):
    n_cast = (len(refs) - 2) // 2
    cast_in, o_ref, cast_out, pre_ref = refs[:n_cast], refs[n_cast], refs[n_cast + 1:-1], refs[-1]
    i = pl.program_id(0)
    nt = pl.num_programs(0) - 1

    def norm_prev():
        return _layer_norm_rows(pre_ref, g_ref, b_ref, o_ref)

    @pl.when(i == 0)
    def _():
        pre_ref[...] = jnp.zeros(pre_ref.shape, F32)

    @pl.when(i < nt)
    def _():
        pre = _ffn_residual(x_ref[...], wg_ref, wu_ref, wd_ref, norm_prev())
        pre_ref[...] = pre
        for src, dst in zip(cast_in, cast_out):
            dst[...] = src[...].astype(BF16)

    @pl.when(i == nt)
    def _():
        _run(norm_prev())


def _cast_row_block(rows, steps):
    units = rows // BF16_SUBLANES
    nblk = max(d for d in range(1, min(units, steps) + 1) if units % d == 0)
    return rows // nblk, nblk


def _cast_specs(arrays, steps):
    cast_in, cast_out = [], []
    for a in arrays:
        rows, cols = a.shape[-2:]
        rpb, nblk = _cast_row_block(rows, steps)
        row_blk = lambda i, last=nblk - 1: jnp.minimum(i, last)
        cast_out.append(pl.BlockSpec((rpb, cols), lambda i, f=row_blk: (f(i), 0)))
        cast_in.append(cast_out[-1] if a.ndim == 2 else
                       pl.BlockSpec((None, rpb, cols), lambda i, f=row_blk: (0, f(i), 0)))
    return cast_in, cast_out


def _cast_kernel(*refs):
    n = len(refs) // 2
    for src, dst in zip(refs[:n], refs[n:]):
        dst[...] = src[...].astype(BF16)


def _cast_bf16(arrays):
    cast_in, cast_out = _cast_specs(arrays, CAST_STEPS)
    return pl.pallas_call(
        _cast_kernel,
        grid=(CAST_STEPS,),
        in_specs=cast_in,
        out_specs=cast_out,
        out_shape=[jax.ShapeDtypeStruct(a.shape[-2:], BF16) for a in arrays],
        compiler_params=pltpu.CompilerParams(dimension_semantics=("arbitrary",),
                                             vmem_limit_bytes=VMEM_LIMIT),
        name="cast_bf16",
    )(*arrays)


def _ffn_ln(x, wg, wu, wd, g, b, tm, casts=()):
    T = x.shape[0]
    nt = T // tm
    cast_in, cast_out = _cast_specs(casts, nt)
    out = pl.pallas_call(
        _ffn_ln_kernel,
        grid=(nt + 1,),
        in_specs=[pl.BlockSpec((tm, D_MODEL), lambda i: (jnp.minimum(i, nt - 1), 0)),
                  _resident(wg.shape), _resident(wu.shape), _resident(wd.shape),
                  _resident(g.shape), _resident(b.shape)] + cast_in,
        out_specs=[pl.BlockSpec((tm, D_MODEL), lambda i: (jnp.maximum(i - 1, 0), 0))] + cast_out,
        out_shape=[jax.ShapeDtypeStruct((T, D_MODEL), F32)]
        + [jax.ShapeDtypeStruct(a.shape[-2:], BF16) for a in casts],
        scratch_shapes=[pltpu.VMEM((tm, D_MODEL), F32)],
        compiler_params=pltpu.CompilerParams(dimension_semantics=("arbitrary",),
                                             vmem_limit_bytes=VMEM_LIMIT),
        name="ffn_ln",
    )(x, wg, wu, wd, g, b, *casts)
    return out[0], out[1:]


def _mix_ffn_ln_kernel(x1_ref, y_ref, o_ref, *refs, absorbed):
    if absorbed:
        wukv_ref, *refs = refs
    wo_ref, g2_ref, b2_ref, wg_ref, wu_ref, wd_ref, g3_ref, b3_ref, out_ref, pre3_ref = refs
    i = pl.program_id(0)
    nt = pl.num_programs(0) - 1

    def mix_residual():
        if absorbed:
            kv_w = QK_NOPE + V_HEAD
            o = jnp.concatenate([_dot(o_ref[h], wukv_ref[:, h * kv_w + QK_NOPE:(h + 1) * kv_w])
                                 for h in range(MLA_HEADS)], axis=1).astype(BF16)
        else:
            o = o_ref[...]
        mix = _dot(y_ref[...], wo_ref[:SSD_D_INNER, :]) + _dot(o, wo_ref[SSD_D_INNER:, :])
        return ALPHA * x1_ref[...] + mix

    def norm_prev():
        return _layer_norm_rows(pre3_ref, g3_ref, b3_ref, out_ref)

    @pl.when(i == 0)
    def _():
        pre3_ref[...] = jnp.zeros(pre3_ref.shape, F32)

    @pl.when(i < nt)
    def _():
        x2 = _layer_norm(mix_residual(), g2_ref[...], b2_ref[...])
        pre3 = _ffn_residual(x2, wg_ref, wu_ref, wd_ref, norm_prev())
        pre3_ref[...] = pre3

    @pl.when(i == nt)
    def _():
        _run(norm_prev())


def _mix_ffn_ln(x1, y, o, wukv, wo, g2, b2, wg, wu, wd, g3, b3, tm):
    T = x1.shape[0]
    nt = T // tm
    cur = lambda i: jnp.minimum(i, nt - 1)
    tok = pl.BlockSpec((tm, D_MODEL), lambda i: (cur(i), 0))
    absorbed = wukv is not None
    o_spec = pl.BlockSpec((MLA_HEADS, tm, KV_LORA), lambda i: (0, cur(i), 0)) if absorbed else tok
    weights = ([wukv] if absorbed else []) + [wo, g2, b2, wg, wu, wd, g3, b3]
    return pl.pallas_call(
        functools.partial(_mix_ffn_ln_kernel, absorbed=absorbed),
        grid=(nt + 1,),
        in_specs=[tok, tok, o_spec] + [_resident(w.shape) for w in weights],
        out_specs=pl.BlockSpec((tm, D_MODEL), lambda i: (jnp.maximum(i - 1, 0), 0)),
        out_shape=jax.ShapeDtypeStruct((T, D_MODEL), F32),
        scratch_shapes=[pltpu.VMEM((tm, D_MODEL), F32)],
        compiler_params=pltpu.CompilerParams(dimension_semantics=("arbitrary",),
                                             vmem_limit_bytes=VMEM_LIMIT),
        name="mix_ffn_ln",
    )(x1, y, o, *weights)


_Z0, _XBC0 = 0, SSD_D_INNER
_CQ0, _CKV0, _DT0, _KR0, _TAIL_COLS = 0, 512, 1024, 1152, 1280


def _rope_tail(rs, cc, ss):
    return rs * cc + pltpu.roll(rs, 64, 1) * ss


def _in_proj_units(x_ref, cc_ref, ss_ref, win_ref, wtail_ref, qg_ref, wuq_ref, kvg_ref, wukv_ref, *refs,
                   absorbed):
    if absorbed:
        z_ref, xbc_ref, dt_ref, lat_ref, kr_ref, q_ref = refs
    else:
        z_ref, xbc_ref, dt_ref, lat_ref, kr_ref, q_ref, k_ref, v_ref = refs
    wide = 2 * LANES
    kv_w = QK_NOPE + V_HEAD
    xb = x_ref[...].astype(BF16)

    def proj(c0, c1):
        return _dot(xb, win_ref[:, c0:c1])

    def proj_tail(c0, c1):
        return _dot(xb, wtail_ref[:, c0:c1])

    for c in range(0, SSD_D_INNER, wide):
        z_ref[:, c:c + wide] = proj(_Z0 + c, _Z0 + c + wide)
        yield
    dk = proj_tail(_DT0, _TAIL_COLS)
    dt_ref[...] = dk[:, :LANES]
    yield
    cq = _rms_norm(proj_tail(_CQ0, _CKV0), qg_ref[...]).astype(BF16)
    yield
    for c in range(0, SSD_CONV_DIM, wide):
        xbc_ref[:, c:c + wide] = proj(_XBC0 + c, _XBC0 + c + wide)
        yield
    for h in range(MLA_HEADS):
        qh = _dot(cq, wuq_ref[:, h * QK_PAD:(h + 1) * QK_PAD])
        q_nope = qh[:, :LANES].astype(BF16)
        if absorbed:
            q_nope = _dot_nt(q_nope, wukv_ref[:, h * kv_w:h * kv_w + QK_NOPE]).astype(BF16)
        q_ref[h, :, :q_nope.shape[1]] = q_nope
        q_ref[h, :, q_nope.shape[1]:] = _rope_tail(qh[:, LANES:], cc_ref[...], ss_ref[...]).astype(BF16)
        yield
    lat = _rms_norm(proj_tail(_CKV0, _DT0), kvg_ref[...])
    lat_ref[...] = lat
    kr_tail = _rope_tail(dk[:, LANES:], cc_ref[...], ss_ref[...])
    kr_ref[...] = kr_tail[:, :QK_ROPE]
    yield
    if absorbed:
        return
    latb = lat.astype(BF16)
    kr_tail = kr_tail.astype(BF16)
    for h in range(MLA_HEADS):
        kv = _dot(latb, wukv_ref[:, h * kv_w:(h + 1) * kv_w])
        k_ref[h, :, :LANES] = kv[:, :QK_NOPE].astype(BF16)
        k_ref[h, :, LANES:] = kr_tail
        v_ref[:, h * V_HEAD:(h + 1) * V_HEAD] = kv[:, QK_NOPE:].astype(BF16)
        yield


def _in_proj_kernel(*refs, absorbed):
    for _ in _in_proj_units(*refs, absorbed=absorbed):
        pass


def _in_proj(x1, cc, ss, win, wtail, qg, wuq, kvg, wukv, *, tm, seq_len, absorbed):
    T = x1.shape[0]
    nt = T // tm
    tok = lambda w: pl.BlockSpec((tm, w), lambda i: (i, 0))
    if cc.shape[0] == T:
        tab = pl.BlockSpec((tm, LANES), lambda i: (i, 0))
    else:
        per_seq = seq_len // tm
        tab = pl.BlockSpec((tm, LANES), lambda i: (i % per_seq, 0))
    q_width = KV_LORA + LANES if absorbed else QK_PAD
    head_tok = lambda w: pl.BlockSpec((MLA_HEADS, tm, w), lambda i: (0, i, 0))
    head_shape = lambda w: jax.ShapeDtypeStruct((MLA_HEADS, T, w), BF16)
    out_shape = [
        jax.ShapeDtypeStruct((T, SSD_D_INNER), F32),
        jax.ShapeDtypeStruct((T, SSD_CONV_DIM), F32),
        jax.ShapeDtypeStruct((T, LANES), F32),
        jax.ShapeDtypeStruct((T, KV_LORA), F32),
        jax.ShapeDtypeStruct((T, QK_ROPE), F32),
        head_shape(q_width),
    ]
    out_specs = [tok(SSD_D_INNER), tok(SSD_CONV_DIM), tok(LANES), tok(KV_LORA), tok(QK_ROPE),
                 head_tok(q_width)]
    weights = [win, wtail, qg, wuq, kvg, wukv]
    if not absorbed:
        out_shape += [head_shape(QK_PAD), jax.ShapeDtypeStruct((T, MLA_WIDTH), BF16)]
        out_specs += [head_tok(QK_PAD), tok(MLA_WIDTH)]
    return pl.pallas_call(
        functools.partial(_in_proj_kernel, absorbed=absorbed),
        grid=(nt,),
        in_specs=[tok(D_MODEL), tab, tab] + [_resident(w.shape) for w in weights],
        out_specs=out_specs,
        out_shape=out_shape,
        compiler_params=pltpu.CompilerParams(dimension_semantics=("parallel",),
                                             vmem_limit_bytes=VMEM_LIMIT),
        name="in_proj",
    )(x1, cc, ss, *weights)


def _split3(x):
    hi = x.astype(BF16)
    r1 = x - hi.astype(F32)
    mid = r1.astype(BF16)
    lo = (r1 - mid.astype(F32)).astype(BF16)
    return hi, mid, lo


def _softplus(x):
    return jnp.maximum(x, 0.0) + jnp.log1p(jnp.exp(-jnp.abs(x)))


def _ssd_block(out, xbc, hist, z, dt_raw, h_in, lb, cw_ref, cb_ref, dtb_ref, alog_ref, dsk_ref, ng_ref):
    Q = SSD_Q
    if lb < Q:
        xbc = jnp.concatenate([xbc, jnp.zeros((Q - lb, SSD_CONV_DIM), F32)], axis=0)
        dt_raw = jnp.concatenate([dt_raw, jnp.zeros((Q - lb, LANES), F32)], axis=0)
    xw = jnp.concatenate([hist, xbc], axis=0)
    conv = xw * cw_ref[0:1, :]
    for k in range(1, CONV_W):
        conv = pltpu.roll(conv, 1, 0) + xw * cw_ref[k:k + 1, :]
    conv = conv[8:, :] + cb_ref[...]
    xc = _silu(conv)
    xs = xc[:, :SSD_D_INNER]
    yield

    row_id = lax.broadcasted_iota(jnp.int32, (Q, LANES), 0)
    dt_col = jnp.where(row_id < lb, _softplus(dt_raw + dtb_ref[...]), 0.0)
    da_col = dt_col * (-jnp.exp(alog_ref[...]))
    ii = lax.broadcasted_iota(jnp.int32, (Q, Q), 0)
    jj = lax.broadcasted_iota(jnp.int32, (Q, Q), 1)
    causal = ii >= jj
    tri = causal.astype(BF16)
    a_col = sum(_dot(tri, p) for p in _split3(da_col)) * LOG2E
    a_row = a_col.T
    dt_row = dt_col.T
    e_col = jnp.exp2(a_col)
    a_last = a_row[:, Q - 1:Q]
    w_row = dt_row * jnp.exp2(a_last - a_row)
    e_last = jnp.exp2(jnp.broadcast_to(a_last, (LANES, LANES)))

    xs_t = xs.T
    yield
    lane = lax.broadcasted_iota(jnp.int32, (Q, LANES), 1)
    lo_half = lane < SSD_HEAD_DIM
    y_parts = []
    h_out = []
    for g in range(SSD_GROUPS):
        b_g = xc[:, SSD_D_INNER + g * SSD_STATE:SSD_D_INNER + (g + 1) * SSD_STATE].astype(BF16)
        c0 = SSD_D_INNER + SSD_GROUPS * SSD_STATE + g * SSD_STATE
        c_g = xc[:, c0:c0 + SSD_STATE].astype(BF16)
        cb = _dot_nt(c_g, b_g)
        h0 = g * SSD_HEADS_PER_GROUP
        hp = h_in[h0:h0 + SSD_HEADS_PER_GROUP].reshape(SSD_HEADS_PER_GROUP * SSD_HEAD_DIM, SSD_STATE)
        y_off = _dot_nt(c_g, hp.astype(BF16))
        lhs = []
        for hh in range(SSD_HEADS_PER_GROUP):
            h = h0 + hh
            lhs.append(xs_t[h * SSD_HEAD_DIM:(h + 1) * SSD_HEAD_DIM, :] * w_row[h:h + 1, :])
        st = _dot(jnp.concatenate(lhs, axis=0).astype(BF16), b_g)
        for hh in range(SSD_HEADS_PER_GROUP):
            h = h0 + hh
            h_out.append(e_last[h:h + 1, :] * h_in[h]
                         + st[hh * SSD_HEAD_DIM:(hh + 1) * SSD_HEAD_DIM, :])
        yield
        for pr in range(SSD_HEADS_PER_GROUP // 2):
            ha = h0 + 2 * pr
            ms = []
            for h in (ha, ha + 1):
                seg = a_col[:, h:h + 1] - a_row[h:h + 1, :]
                m = jnp.exp2(jnp.where(causal, seg, -jnp.inf)) * cb * dt_row[h:h + 1, :]
                ms.append(m.astype(BF16))
            xp = xs[:, ha * SSD_HEAD_DIM:(ha + 2) * SSD_HEAD_DIM]
            rhs = jnp.concatenate([jnp.where(lo_half, xp, 0.0), jnp.where(lo_half, 0.0, xp)],
                                  axis=0).astype(BF16)
            y_d = _dot(jnp.concatenate(ms, axis=1), rhs)
            yo = y_off[:, 2 * pr * SSD_HEAD_DIM:(2 * pr + 2) * SSD_HEAD_DIM]
            dec = jnp.where(lo_half, e_col[:, ha:ha + 1], e_col[:, ha + 1:ha + 2])
            y_parts.append(y_d + yo * dec)
            if pr % 2 == 1:
                yield
    y = jnp.concatenate(y_parts, axis=1) + dsk_ref[...] * xs
    y = y[:lb] * _silu(z)
    half = SSD_D_INNER // SSD_GROUPS
    ng = ng_ref[...]
    y = jnp.concatenate([_rms_norm(y[:, g * half:(g + 1) * half], ng[:, g * half:(g + 1) * half])
                         for g in range(SSD_GROUPS)], axis=1)
    out["y"] = y
    out["h"] = jnp.stack(h_out)


def _ssd_short_kernel(xbc_ref, z_ref, dt_ref, cprev_ref, h0_ref, *refs, seq_len, seqs):
    *params, y_ref, nconv_ref, hnew_ref, tail_ref = refs

    def one(s):
        r0 = s * seq_len
        tail_ref[s] = jnp.zeros(tail_ref.shape[1:], F32)
        tail_ref[s, 8 - (CONV_W - 1):, :] = cprev_ref[s]
        out = {}
        yield from _ssd_block(out, xbc_ref[r0:r0 + seq_len, :], tail_ref[s], z_ref[r0:r0 + seq_len, :],
                              dt_ref[r0:r0 + seq_len, :], h0_ref[s], seq_len, *params)
        y_ref[r0:r0 + seq_len, :] = out["y"].astype(BF16)
        nconv_ref[s] = xbc_ref[r0 + seq_len - (CONV_W - 1):r0 + seq_len, :]
        hnew_ref[s] = out["h"]

    active = [one(s) for s in range(seqs)]
    while active:
        active = [g for g in active if next(g, True) is None]


def _in_proj_ssd_kernel(x_ref, cc_ref, ss_ref, win_w, wtail, qg, wuq, kvg, wukv, cw, cb, dtb, alog, dsk, ng,
                        lat_ref, kr_ref, q_ref, k_ref, v_ref, y_ref, nconv_ref, hnew_ref,
                        zs_ref, xs_ref, dts_ref, tail_ref, h_ref, *, tiles_per_seq):
    i = pl.program_id(0)
    slot = i % 2
    tm = x_ref.shape[0]

    @pl.when(i == 0)
    def _():
        zs_ref[1] = jnp.zeros(zs_ref.shape[1:], F32)
        xs_ref[1] = jnp.zeros(xs_ref.shape[1:], F32)
        dts_ref[1] = jnp.zeros(dts_ref.shape[1:], F32)
        tail_ref[...] = jnp.zeros(tail_ref.shape, F32)
        h_ref[...] = jnp.zeros(h_ref.shape, F32)

    proj_units = _in_proj_units(x_ref, cc_ref, ss_ref, win_w, wtail, qg, wuq, kvg, wukv,
                                zs_ref.at[slot], xs_ref.at[slot], dts_ref.at[slot],
                                lat_ref, kr_ref, q_ref, k_ref, v_ref, absorbed=False)

    prev = 1 - slot
    fresh = (i - 1) % tiles_per_seq == 0
    hist = jnp.where(fresh, 0.0, tail_ref[...])
    h = jnp.where(fresh, 0.0, h_ref[...])
    for sb in range(tm // SSD_Q):
        r0 = sb * SSD_Q
        if sb > 0:
            hist = xs_ref[prev, r0 - 8:r0, :]
        out = {}
        for _ in _ssd_block(out, xs_ref[prev, r0:r0 + SSD_Q, :], hist, zs_ref[prev, r0:r0 + SSD_Q, :],
                            dts_ref[prev, r0:r0 + SSD_Q, :], h, SSD_Q, cw, cb, dtb, alog, dsk, ng):
            next(proj_units, None)
        y_ref[r0:r0 + SSD_Q, :] = out["y"].astype(BF16)
        h = out["h"]
    for _ in proj_units:
        pass
    tail_ref[...] = xs_ref[prev, tm - 8:tm, :]
    h_ref[...] = h
    nconv_ref[0] = xs_ref[prev, tm - (CONV_W - 1):tm, :]
    hnew_ref[0] = h


def _in_proj_ssd(x1, cc, ss, win, wtail, qg, wuq, kvg, wukv, cw, cb, dtb, alog, dsk, ng, *, tm, seq_len):
    T = x1.shape[0]
    nt = T // tm
    per_seq = seq_len // tm
    nb = T // seq_len
    cur = lambda i: jnp.minimum(i, nt - 1)
    old = lambda i: jnp.maximum(i - 1, 0)
    tok = lambda w: pl.BlockSpec((tm, w), lambda i: (cur(i), 0))
    tab = pl.BlockSpec((tm, LANES), lambda i: (cur(i) % per_seq, 0))
    head_tok = pl.BlockSpec((MLA_HEADS, tm, QK_PAD), lambda i: (0, cur(i), 0))
    head_shape = jax.ShapeDtypeStruct((MLA_HEADS, T, QK_PAD), BF16)
    weights = [win, wtail, qg, wuq, kvg, wukv, cw, cb, dtb, alog, dsk, ng]
    return pl.pallas_call(
        functools.partial(_in_proj_ssd_kernel, tiles_per_seq=per_seq),
        grid=(nt + 1,),
        in_specs=[tok(D_MODEL), tab, tab] + [_resident(w.shape) for w in weights],
        out_specs=[tok(KV_LORA),
                   pl.BlockSpec((None, tm, QK_ROPE), lambda i: (cur(i) // per_seq, cur(i) % per_seq, 0)),
                   head_tok, head_tok, tok(MLA_WIDTH),
                   pl.BlockSpec((tm, SSD_D_INNER), lambda i: (old(i), 0)),
                   pl.BlockSpec((1, CONV_W - 1, SSD_CONV_DIM), lambda i: (old(i) // per_seq, 0, 0)),
                   pl.BlockSpec((1, SSD_HEADS, SSD_HEAD_DIM, SSD_STATE),
                                lambda i: (old(i) // per_seq, 0, 0, 0))],
        out_shape=[jax.ShapeDtypeStruct((T, KV_LORA), F32),
                   jax.ShapeDtypeStruct((nb, seq_len, QK_ROPE), F32),
                   head_shape, head_shape,
                   jax.ShapeDtypeStruct((T, MLA_WIDTH), BF16),
                   jax.ShapeDtypeStruct((T, SSD_D_INNER), BF16),
                   jax.ShapeDtypeStruct((nb, CONV_W - 1, SSD_CONV_DIM), F32),
                   jax.ShapeDtypeStruct((nb, SSD_HEADS, SSD_HEAD_DIM, SSD_STATE), F32)],
        scratch_shapes=[pltpu.VMEM((2, tm, SSD_D_INNER), F32),
                        pltpu.VMEM((2, tm, SSD_CONV_DIM), F32),
                        pltpu.VMEM((2, tm, LANES), F32),
                        pltpu.VMEM((8, SSD_CONV_DIM), F32),
                        pltpu.VMEM((SSD_HEADS, SSD_HEAD_DIM, SSD_STATE), F32)],
        compiler_params=pltpu.CompilerParams(dimension_semantics=("arbitrary",),
                                             vmem_limit_bytes=VMEM_LIMIT),
        name="in_proj_ssd",
    )(x1, cc, ss, *weights)


def _ssd_short(xbc, z, dt, cprev, h0, cw, cb, dtb, alog, dsk, ng, *, nb, seq_len):
    assert seq_len <= SSD_Q and seq_len % 8 == 0 and nb % SSD_SEQS_PER_STEP == 0
    seqs = SSD_SEQS_PER_STEP
    tokb = lambda w: pl.BlockSpec((seqs * seq_len, w), lambda b: (b, 0))
    conv_spec = pl.BlockSpec((seqs, CONV_W - 1, SSD_CONV_DIM), lambda b: (b, 0, 0))
    state_spec = pl.BlockSpec((seqs, SSD_HEADS, SSD_HEAD_DIM, SSD_STATE), lambda b: (b, 0, 0, 0))
    params = [cw, cb, dtb, alog, dsk, ng]
    return pl.pallas_call(
        functools.partial(_ssd_short_kernel, seq_len=seq_len, seqs=seqs),
        grid=(nb // seqs,),
        in_specs=[tokb(SSD_CONV_DIM), tokb(SSD_D_INNER), tokb(LANES), conv_spec, state_spec]
        + [pl.BlockSpec(p.shape, lambda b: (0, 0)) for p in params],
        out_specs=(tokb(SSD_D_INNER), conv_spec, state_spec),
        out_shape=(jax.ShapeDtypeStruct((nb * seq_len, SSD_D_INNER), BF16),
                   jax.ShapeDtypeStruct((nb, CONV_W - 1, SSD_CONV_DIM), F32),
                   jax.ShapeDtypeStruct((nb, SSD_HEADS, SSD_HEAD_DIM, SSD_STATE), F32)),
        scratch_shapes=[pltpu.VMEM((seqs, 8, SSD_CONV_DIM), F32)],
        compiler_params=pltpu.CompilerParams(dimension_semantics=("parallel",),
                                             vmem_limit_bytes=VMEM_LIMIT),
        name="ssd",
    )(xbc, z, dt, cprev, h0, *params)


def _lane_tile(x, width):
    return jnp.concatenate([x] * (width // LANES), axis=1)


def _attn_prompt_kernel(q_ref, k_ref, v_ref, o_ref, vx_ref, m_ref, acc_ref, *, seq_len, tq, tk, nh):
    ri = lax.broadcasted_iota(jnp.int32, (tk, tk), 0)
    ci = lax.broadcasted_iota(jnp.int32, (tk, tk), 1)
    diag_mask = (ci // CHUNK) <= (ri // CHUNK)

    def head_steps(hh):
        cols = slice(hh * V_HEAD, (hh + 1) * V_HEAD)
        vx_ref[hh, :, :V_HEAD] = v_ref[:, cols]
        vx_ref[hh, :, V_HEAD:] = jnp.ones((seq_len, LANES), BF16)

        def step(r0, k0, row0, masked):
            s = _dot_nt(q_ref[hh, r0 + row0:r0 + tq, :], k_ref[hh, k0:k0 + tk, :])
            if masked:
                top = jnp.where(diag_mask, s[:tk], NEG_BIG)
                s = top if tq - row0 == tk else jnp.concatenate([top, s[tk:]], axis=0)
            m_prev = m_ref[hh, row0:, :]
            m_next = jnp.maximum(m_prev, jnp.max(s, axis=-1, keepdims=True))
            alpha = jnp.exp2(m_prev - m_next)
            p = jnp.exp2(s - _lane_tile(m_next, tk))
            m_ref[hh, row0:, :] = m_next
            acc_ref[hh, row0:, :] = (_lane_tile(alpha, V_HEAD + LANES) * acc_ref[hh, row0:, :]
                                     + _dot(p.astype(BF16), vx_ref[hh, k0:k0 + tk, :]))

        for qb in range(seq_len // tq):
            r0 = qb * tq
            m_ref[hh] = jnp.full(m_ref.shape[1:], NEG_BIG, F32)
            acc_ref[hh] = jnp.zeros(acc_ref.shape[1:], F32)
            for j in range(r0 // tk):
                step(r0, j * tk, 0, False)
                yield
            for d in range(tq // tk):
                step(r0, r0 + d * tk, d * tk, True)
                yield
            o_ref[r0:r0 + tq, cols] = (acc_ref[hh, :, :V_HEAD] / acc_ref[hh, :, V_HEAD:]).astype(BF16)

    active = [head_steps(hh) for hh in range(nh)]
    while active:
        active = [g for g in active if next(g, True) is None]


def _attn_prompt(q, k, v, *, nb, seq_len):
    tq = min(ATT_TQ, seq_len)
    tk = min(ATT_TK, tq)
    nh = ATT_HEADS_PER_STEP
    scratch = [pltpu.VMEM((nh, seq_len, V_HEAD + LANES), BF16), pltpu.VMEM((nh, tq, LANES), F32),
               pltpu.VMEM((nh, tq, V_HEAD + LANES), F32)]
    return pl.pallas_call(
        functools.partial(_attn_prompt_kernel, seq_len=seq_len, tq=tq, tk=tk, nh=nh),
        grid=(nb, MLA_HEADS // nh),
        in_specs=[pl.BlockSpec((nh, seq_len, QK_PAD), lambda b, h: (h, b, 0)),
                  pl.BlockSpec((nh, seq_len, QK_PAD), lambda b, h: (h, b, 0)),
                  pl.BlockSpec((seq_len, nh * V_HEAD), lambda b, h: (b, h))],
        out_specs=pl.BlockSpec((seq_len, nh * V_HEAD), lambda b, h: (b, h)),
        out_shape=jax.ShapeDtypeStruct((nb * seq_len, MLA_WIDTH), BF16),
        scratch_shapes=scratch,
        compiler_params=pltpu.CompilerParams(dimension_semantics=("parallel", "parallel"),
                                             vmem_limit_bytes=VMEM_LIMIT),
        name="attn_prompt",
    )(q, k, v)


def _attn_sample_kernel(q_ref, latc_ref, krc_ref, latn_ref, krn_ref, o_ref, *, past_len, new_len):
    rows = MLA_HEADS * new_len
    qa = jnp.concatenate([q_ref[h] for h in range(MLA_HEADS)], axis=0)
    q_lat = qa[:, :KV_LORA]
    q_rope = qa[:, KV_LORA:KV_LORA + QK_ROPE]
    latc = latc_ref[...].astype(BF16)
    latn = latn_ref[...].astype(BF16)
    sc = _dot_nt(q_lat, latc) + _dot_nt(q_rope, krc_ref[...].astype(BF16))
    sn = _dot_nt(q_lat, latn) + _dot_nt(q_rope, krn_ref[...].astype(BF16))
    qpos = past_len + lax.broadcasted_iota(jnp.int32, (rows, 1), 0) % new_len
    kpos_c = lax.broadcasted_iota(jnp.int32, (1, past_len), 1)
    kpos_n = past_len + lax.broadcasted_iota(jnp.int32, (1, new_len), 1)
    sc = jnp.where((kpos_c // CHUNK) <= (qpos // CHUNK), sc, NEG_BIG)
    sn = jnp.where((kpos_n // CHUNK) <= (qpos // CHUNK), sn, NEG_BIG)
    m = jnp.maximum(jnp.max(sc, axis=-1, keepdims=True), jnp.max(sn, axis=-1, keepdims=True))
    pc = jnp.exp2(sc - m)
    pn = jnp.exp2(sn - m)
    l = jnp.sum(pc, axis=-1, keepdims=True) + jnp.sum(pn, axis=-1, keepdims=True)
    o = (_dot(pc.astype(BF16), latc) + _dot(pn.astype(BF16), latn)) / l
    for h in range(MLA_HEADS):
        o_ref[h] = o[h * new_len:(h + 1) * new_len].astype(BF16)


def _attn_sample(qa, lat_c, kr_c, lat_n, kr_n, *, nb, past_len, new_len):
    return pl.pallas_call(
        functools.partial(_attn_sample_kernel, past_len=past_len, new_len=new_len),
        grid=(nb,),
        in_specs=[pl.BlockSpec((MLA_HEADS, new_len, KV_LORA + LANES), lambda b: (0, b, 0)),
                  pl.BlockSpec((None, past_len, KV_LORA), lambda b: (b, 0, 0)),
                  pl.BlockSpec((None, past_len, QK_ROPE), lambda b: (b, 0, 0)),
                  pl.BlockSpec((new_len, KV_LORA), lambda b: (b, 0)),
                  pl.BlockSpec((new_len, QK_ROPE), lambda b: (b, 0))],
        out_specs=pl.BlockSpec((MLA_HEADS, new_len, KV_LORA), lambda b: (0, b, 0)),
        out_shape=jax.ShapeDtypeStruct((MLA_HEADS, nb * new_len, KV_LORA), BF16),
        compiler_params=pltpu.CompilerParams(dimension_semantics=("parallel",),
                                             vmem_limit_bytes=VMEM_LIMIT),
        name="attn_sample",
    )(qa, lat_c, kr_c, lat_n, kr_n)


def _rope_tables(pos):
    inv = ROPE_THETA ** (-jnp.arange(0, QK_ROPE, 2, dtype=F32) / QK_ROPE)
    ang = pos.astype(F32)[:, None] * inv[None, :]
    cos, sin = jnp.cos(ang), jnp.sin(ang)
    pad = jnp.zeros((pos.shape[0], LANES - QK_ROPE), F32)
    return (jnp.concatenate([cos, cos, pad], axis=1), jnp.concatenate([-sin, sin, pad], axis=1))


def _pack_weights(w_in, conv_w, conv_b, dt_bias, a_log, d_skip, ssd_norm_g, q_norm_g, w_uq,
                  kv_norm_g, w_ukv, w_out):
    i0 = SSD_D_INNER
    i1 = i0 + SSD_CONV_DIM
    i2 = i1 + SSD_HEADS
    i3 = i2 + Q_LORA
    i4 = i3 + KV_LORA
    w_dt = jnp.pad(w_in[:, i1:i2], ((0, 0), (0, LANES - SSD_HEADS)))
    w_kr = w_in[:, i4:]
    wtail = jnp.concatenate([w_in[:, i2:i3], w_in[:, i3:i4], w_dt, w_kr,
                             w_kr[:, QK_ROPE // 2:], w_kr[:, :QK_ROPE // 2]], axis=1)
    uq = (w_uq * (SM_SCALE * LOG2E)).reshape(Q_LORA, MLA_HEADS, QK_NOPE + QK_ROPE)
    rope = uq[:, :, QK_NOPE:]
    wuq = jnp.concatenate([uq, rope[:, :, QK_ROPE // 2:], rope[:, :, :QK_ROPE // 2]], axis=2)
    wuq = wuq.reshape(Q_LORA, MLA_HEADS * QK_PAD)
    lane_pad = lambda v: jnp.pad(v, (0, LANES - SSD_HEADS)).reshape(1, LANES)
    return dict(
        wtail=wtail, wuq=wuq,
        qg=q_norm_g.reshape(1, Q_LORA), kvg=kv_norm_g.reshape(1, KV_LORA),
        cw=conv_w, cb=conv_b.reshape(1, SSD_CONV_DIM),
        dtb=lane_pad(dt_bias), alog=lane_pad(a_log),
        dsk=jnp.repeat(d_skip, SSD_HEAD_DIM).reshape(1, SSD_D_INNER),
        ng=ssd_norm_g.reshape(1, SSD_D_INNER),
    )


def _stream(x1, nb, L, conv_prev, h0, lat_prev, kr_prev, pos0, w, f2, lns):
    tm = min(TOKEN_TILE, nb * L)
    _, (g2, b2), (g3, b3) = lns
    prompt = lat_prev is None
    cc, ss = _rope_tables(pos0 + jnp.arange(L))
    if L % tm != 0:
        cc, ss = jnp.tile(cc, (nb, 1)), jnp.tile(ss, (nb, 1))
    proj_w = (w["win"], w["wtail"], w["qg"], w["wuq"], w["kvg"], w["wukv"])
    ssd_w = (w["cw"], w["cb"], w["dtb"], w["alog"], w["dsk"], w["ng"])
    if prompt:
        assert L % tm == 0 and tm % SSD_Q == 0
        lat, kr, q, kcat, v, y, nconv, hnew = _in_proj_ssd(x1, cc, ss, *proj_w, *ssd_w, tm=tm, seq_len=L)
        o = _attn_prompt(q, kcat, v, nb=nb, seq_len=L)
    else:
        z, xbc, dt, lat, kr, q = _in_proj(x1, cc, ss, *proj_w, tm=tm, seq_len=L, absorbed=True)
        y, nconv, hnew = _ssd_short(xbc, z, dt, conv_prev, h0, *ssd_w, nb=nb, seq_len=L)
        past = lat_prev.shape[1]
        o = _attn_sample(q, lat_prev, kr_prev, lat, kr, nb=nb, past_len=past, new_len=L)
    out = _mix_ffn_ln(x1, y, o, None if prompt else w["wukv"], w["wo"], g2, b2, *f2, g3, b3, tm)
    return (out.reshape(nb, L, D_MODEL), lat.reshape(nb, L, KV_LORA), kr.reshape(nb, L, QK_ROPE),
            nconv, hnew)


def kernel(x_prompt, x_sample, cache_latent, cache_k_rope, state_conv, state_ssm, ln1_g, ln1_b, ffn1_w_gate, ffn1_w_up, ffn1_w_down, w_in, conv_w, conv_b, dt_bias, a_log, d_skip, ssd_norm_g, q_norm_g, w_uq, kv_norm_g, w_ukv, w_out, ln2_g, ln2_b, ffn2_w_gate, ffn2_w_up, ffn2_w_down, ln3_g, ln3_b):
    assert w_in.shape[0] == DEPTH == 1
    l = 0
    w = _pack_weights(w_in[l], conv_w[l], conv_b[l], dt_bias[l], a_log[l], d_skip[l],
                      ssd_norm_g[l], q_norm_g[l], w_uq[l], kv_norm_g[l], w_ukv[l], w_out[l])
    f1 = _cast_bf16([ffn1_w_gate, ffn1_w_up, ffn1_w_down])
    row = lambda v: v[l].reshape(1, D_MODEL)
    lns = ((row(ln1_g), row(ln1_b)), (row(ln2_g), row(ln2_b)), (row(ln3_g), row(ln3_b)))
    (g1, b1) = lns[0]

    nbp, lp, _ = x_prompt.shape
    nbs, ls, _ = x_sample.shape
    x1_p, cast = _ffn_ln(x_prompt.reshape(nbp * lp, D_MODEL), *f1, g1, b1, min(TOKEN_TILE, nbp * lp),
                         casts=[ffn2_w_gate, ffn2_w_up, ffn2_w_down, w_in, w["wtail"], w["wuq"], w_ukv, w_out])
    f2 = cast[:3]
    w.update(zip(("win", "wtail", "wuq", "wukv", "wo"), cast[3:]))
    x1_s, _ = _ffn_ln(x_sample.reshape(nbs * ls, D_MODEL), *f1, g1, b1, min(TOKEN_TILE, nbs * ls))

    yp, lat_p, kr_p, conv_p, ssm_p = _stream(x1_p, nbp, lp, None, None, None, None, 0, w, f2, lns)
    past = cache_latent.shape[2]
    ys, lat_s, kr_s, conv_s, ssm_s = _stream(x1_s, nbs, ls, state_conv[l], state_ssm[l], cache_latent[l],
                                             cache_k_rope[l], past, w, f2, lns)
    st = lambda a: a[None]
    return (yp, ys, st(lat_p), st(kr_p), st(conv_p), st(ssm_p),
            st(lat_s), st(kr_s), st(conv_s), st(ssm_s))
```

```python
import functools

import jax
import jax.numpy as jnp
from jax import lax
from jax.experimental import pallas as pl
from jax.experimental.pallas import tpu as pltpu

F32 = jnp.float32
BF16 = jnp.bfloat16

D_MODEL = 1024
D_FF = 2816
CHUNK = 64
SSD_HEADS = 16
SSD_HEAD_DIM = 64
SSD_D_INNER = SSD_HEADS * SSD_HEAD_DIM
SSD_GROUPS = 2
SSD_HEADS_PER_GROUP = SSD_HEADS // SSD_GROUPS
SSD_STATE = 128
CONV_W = 4
SSD_CONV_DIM = SSD_D_INNER + 2 * SSD_GROUPS * SSD_STATE
MLA_HEADS = 8
Q_LORA = 512
KV_LORA = 512
QK_NOPE = 128
QK_ROPE = 64
V_HEAD = 128
MLA_WIDTH = MLA_HEADS * V_HEAD
ROPE_THETA = 10000.0
DEPTH = 1
ALPHA = (2 * DEPTH) ** 0.25
EPS = 1e-5
SM_SCALE = (QK_NOPE + QK_ROPE) ** -0.5
LOG2E = 1.4426950408889634
NEG_BIG = -1e30

LANES = 128
BF16_SUBLANES = 16
QK_PAD = 2 * LANES
SSD_Q = 128
FF_CHUNK = 256
TOKEN_TILE = 512
LN_ROW_BLOCKS = 8
SIDE_LAG = 2
CAST_STEPS = 16
SSD_SEQS_PER_STEP = 4
ATT_TQ = 1024
ATT_TK = 256
ATT_HEADS_PER_STEP = 4
VMEM_LIMIT = 56 * 1024 * 1024

_NT = (((1,), (1,)), ((), ()))


def _resident(shape):
    nd = len(shape)
    return pl.BlockSpec(shape, lambda *_: (0,) * nd, pipeline_mode=pl.Buffered(1))


def _dot(a, b):
    return jnp.dot(a, b, preferred_element_type=F32)


def _dot_nt(a, b):
    return lax.dot_general(a, b, _NT, preferred_element_type=F32)


def _sigmoid(x):
    return 1.0 / (1.0 + jnp.exp2(x * (-LOG2E)))


def _silu(x):
    return x * _sigmoid(x)


def _layer_norm(y, g, b):
    mu = jnp.mean(y, axis=-1, keepdims=True)
    d = y - mu
    var = jnp.mean(d * d, axis=-1, keepdims=True)
    return d * lax.rsqrt(var + EPS) * g + b


def _rms_norm(x, g):
    return x * lax.rsqrt(jnp.mean(x * x, axis=-1, keepdims=True) + EPS) * g


def _zero_after(values):
    t = values[0:8]
    for r in range(8, values.shape[0], 8):
        t = t + values[r:r + 8]
    u = t[:, :LANES]
    for c in range(LANES, values.shape[1], LANES):
        u = u + t[:, c:c + LANES]
    return pltpu.bitcast((pltpu.bitcast(u, jnp.uint32) >> 16) >> 16, F32)


def _ffn_residual(x, wg_ref, wu_ref, wd_ref, side=()):
    side = iter(side)
    rows = x.shape[0]
    xb = x.astype(BF16)
    acc = jnp.zeros((rows, D_MODEL), F32)
    due = {}
    for c in range(D_FF // FF_CHUNK):
        cs = slice(c * FF_CHUNK, (c + 1) * FF_CHUNK)
        h = _silu(_dot(xb, wg_ref[:, cs])) * _dot(xb, wu_ref[:, cs])
        for v in due.pop(c, ()):
            h = h + jnp.tile(_zero_after(v), (rows // 8, FF_CHUNK // LANES))
        acc = acc + _dot(h.astype(BF16), wd_ref[cs, :])
        produced = next(side, None)
        if produced is not None and c + SIDE_LAG < D_FF // FF_CHUNK:
            due[c + SIDE_LAG] = produced
    for _ in side:
        pass
    return ALPHA * x + 0.5 * acc


def _layer_norm_rows(src, g_ref, b_ref, dst_ref):
    nblk = LN_ROW_BLOCKS if dst_ref.shape[0] % (8 * LN_ROW_BLOCKS) == 0 else 1
    rows = dst_ref.shape[0] // nblk
    for r in range(nblk):
        sl = slice(r * rows, (r + 1) * rows)
        dst_ref[sl, :] = out = _layer_norm(src[sl, :], g_ref[...], b_ref[...])
        yield (out,)


def _run(gen):
    for _ in gen:
        pass


def _ffn_ln_kernel(xl_ref, x_ref, wg_ref, wu_ref, wd_ref, g_ref, b_ref, *refs):
    n_cast = (len(refs) - 2) // 2
    cast_in, o_ref, cast_out, pre_ref = refs[:n_cast], refs[n_cast], refs[n_cast + 1:-1], refs[-1]
    i = pl.program_id(0)
    nt = pl.num_programs(0) - 1

    def norm_prev():
        return _layer_norm_rows(pre_ref, g_ref, b_ref, o_ref)

    @pl.when(i == 0)
    def _():
        pre_ref[...] = jnp.zeros(pre_ref.shape, F32)

    @pl.when(i < nt)
    def _():
        x = jnp.where(i == 0, xl_ref[...], x_ref[...])
        pre = _ffn_residual(x, wg_ref, wu_ref, wd_ref, norm_prev())
        pre_ref[...] = pre
        for src, dst in zip(cast_in, cast_out):
            dst[...] = src[...].astype(BF16)

    @pl.when(i == nt)
    def _():
        _run(norm_prev())


def _cast_row_block(rows, steps):
    units = rows // BF16_SUBLANES
    nblk = max(d for d in range(1, min(units, steps) + 1) if units % d == 0)
    return rows // nblk, nblk


def _cast_specs(arrays, steps):
    cast_in, cast_out = [], []
    for a in arrays:
        rows, cols = a.shape[-2:]
        rpb, nblk = _cast_row_block(rows, steps)
        row_blk = lambda i, last=nblk - 1: jnp.minimum(i, last)
        cast_out.append(pl.BlockSpec((rpb, cols), lambda i, f=row_blk: (f(i), 0)))
        cast_in.append(cast_out[-1] if a.ndim == 2 else
                       pl.BlockSpec((None, rpb, cols), lambda i, f=row_blk: (0, f(i), 0)))
    return cast_in, cast_out


def _cast_kernel(*refs):
    n = len(refs) // 2
    for src, dst in zip(refs[:n], refs[n:]):
        dst[...] = src[...].astype(BF16)


def _cast_bf16(arrays):
    cast_in, cast_out = _cast_specs(arrays, CAST_STEPS)
    return pl.pallas_call(
        _cast_kernel,
        grid=(CAST_STEPS,),
        in_specs=cast_in,
        out_specs=cast_out,
        out_shape=[jax.ShapeDtypeStruct(a.shape[-2:], BF16) for a in arrays],
        compiler_params=pltpu.CompilerParams(dimension_semantics=("arbitrary",),
                                             vmem_limit_bytes=VMEM_LIMIT),
        name="cast_bf16",
    )(*arrays)


def _ffn_ln(x_lead, x, wg, wu, wd, g, b, casts=()):
    tm = x_lead.shape[0]
    nt = x.shape[0] // tm
    m = nt + 1
    cast_in, cast_out = _cast_specs(casts, m)
    out = pl.pallas_call(
        _ffn_ln_kernel,
        grid=(m + 1,),
        in_specs=[pl.BlockSpec((tm, D_MODEL), lambda i: (0, 0)),
                  pl.BlockSpec((tm, D_MODEL), lambda i: (jnp.clip(i - 1, 0, nt - 1), 0)),
                  _resident(wg.shape), _resident(wu.shape), _resident(wd.shape),
                  _resident(g.shape), _resident(b.shape)] + cast_in,
        out_specs=[pl.BlockSpec((tm, D_MODEL), lambda i: (jnp.maximum(i - 1, 0), 0))] + cast_out,
        out_shape=[jax.ShapeDtypeStruct((m * tm, D_MODEL), F32)]
        + [jax.ShapeDtypeStruct(a.shape[-2:], BF16) for a in casts],
        scratch_shapes=[pltpu.VMEM((tm, D_MODEL), F32)],
        compiler_params=pltpu.CompilerParams(dimension_semantics=("arbitrary",),
                                             vmem_limit_bytes=VMEM_LIMIT),
        name="ffn_ln",
    )(x_lead, x, wg, wu, wd, g, b, *casts)
    return out[0], out[1:]


def _mix_ffn_ln_kernel(x1_ref, y_ref, o_ref, yl_ref, ol_ref, wukv_ref, wo_ref, g2_ref, b2_ref,
                       wg_ref, wu_ref, wd_ref, g3_ref, b3_ref, out_ref, outl_ref, pre3_ref, olead_ref):
    i = pl.program_id(0)
    m = pl.num_programs(0) - 1

    @pl.when(i == 0)
    def _():
        pre3_ref[...] = jnp.zeros(pre3_ref.shape, F32)
        kv_w = QK_NOPE + V_HEAD
        for h in range(MLA_HEADS):
            olead_ref[:, h * V_HEAD:(h + 1) * V_HEAD] = _dot(
                ol_ref[h], wukv_ref[:, h * kv_w + QK_NOPE:(h + 1) * kv_w]).astype(BF16)

    def norm_prev():
        return _layer_norm_rows(pre3_ref, g3_ref, b3_ref, out_ref)

    @pl.when(i < m)
    def _():
        lead = i == 0
        y = jnp.where(lead, yl_ref[...], y_ref[...])
        o = jnp.where(lead, olead_ref[...], o_ref[...])
        mix = _dot(y, wo_ref[:SSD_D_INNER, :]) + _dot(o, wo_ref[SSD_D_INNER:, :])
        x2 = _layer_norm(ALPHA * x1_ref[...] + mix, g2_ref[...], b2_ref[...])
        pre3 = _ffn_residual(x2, wg_ref, wu_ref, wd_ref, norm_prev())
        pre3_ref[...] = pre3

    @pl.when(i == m)
    def _():
        _run(norm_prev())

    @pl.when(i == 1)
    def _():
        outl_ref[...] = out_ref[...]


def _mix_ffn_ln(x1, y, o, y_lead, o_lead, wukv, wo, g2, b2, wg, wu, wd, g3, b3):
    tm = y_lead.shape[0]
    nt = y.shape[0] // tm
    m = nt + 1
    main = pl.BlockSpec((tm, D_MODEL), lambda i: (jnp.clip(i - 1, 0, nt - 1), 0))
    weights = [wukv, wo, g2, b2, wg, wu, wd, g3, b3]
    return pl.pallas_call(
        _mix_ffn_ln_kernel,
        grid=(m + 1,),
        in_specs=[pl.BlockSpec((tm, D_MODEL), lambda i: (jnp.minimum(i, m - 1), 0)), main, main,
                  pl.BlockSpec((tm, D_MODEL), lambda i: (0, 0)),
                  pl.BlockSpec((MLA_HEADS, tm, KV_LORA), lambda i: (0, 0, 0))]
        + [_resident(w.shape) for w in weights],
        out_specs=[pl.BlockSpec((tm, D_MODEL), lambda i: (jnp.maximum(i - 2, 0), 0)),
                   pl.BlockSpec((tm, D_MODEL), lambda i: (0, 0))],
        out_shape=[jax.ShapeDtypeStruct((nt * tm, D_MODEL), F32),
                   jax.ShapeDtypeStruct((tm, D_MODEL), F32)],
        scratch_shapes=[pltpu.VMEM((tm, D_MODEL), F32), pltpu.VMEM((tm, MLA_WIDTH), BF16)],
        compiler_params=pltpu.CompilerParams(dimension_semantics=("arbitrary",),
                                             vmem_limit_bytes=VMEM_LIMIT),
        name="mix_ffn_ln",
    )(x1, y, o, y_lead, o_lead, *weights)


_Z0, _XBC0 = 0, SSD_D_INNER
_CQ0, _CKV0, _DT0, _KR0, _TAIL_COLS = 0, 512, 1024, 1152, 1280


def _rope_tail(rs, cc, ss):
    return rs * cc + pltpu.roll(rs, 64, 1) * ss


def _in_proj_units(x_ref, cc_ref, ss_ref, win_ref, wtail_ref, qg_ref, wuq_ref, kvg_ref, wukv_ref, *refs,
                   absorbed):
    if absorbed:
        z_ref, xbc_ref, dt_ref, lat_ref, kr_ref, q_ref = refs
    else:
        z_ref, xbc_ref, dt_ref, lat_ref, kr_ref, q_ref, k_ref, v_ref = refs
    wide = 2 * LANES
    kv_w = QK_NOPE + V_HEAD
    xb = x_ref[...].astype(BF16)

    def proj(c0, c1):
        return _dot(xb, win_ref[:, c0:c1])

    def proj_tail(c0, c1):
        return _dot(xb, wtail_ref[:, c0:c1])

    for c in range(0, SSD_D_INNER, wide):
        z_ref[:, c:c + wide] = proj(_Z0 + c, _Z0 + c + wide)
        yield
    dk = proj_tail(_DT0, _TAIL_COLS)
    dt_ref[...] = dk[:, :LANES]
    yield
    cq = _rms_norm(proj_tail(_CQ0, _CKV0), qg_ref[...]).astype(BF16)
    yield
    for c in range(0, SSD_CONV_DIM, wide):
        xbc_ref[:, c:c + wide] = proj(_XBC0 + c, _XBC0 + c + wide)
        yield
    for h in range(MLA_HEADS):
        qh = _dot(cq, wuq_ref[:, h * QK_PAD:(h + 1) * QK_PAD])
        q_nope = qh[:, :LANES].astype(BF16)
        if absorbed:
            q_nope = _dot_nt(q_nope, wukv_ref[:, h * kv_w:h * kv_w + QK_NOPE]).astype(BF16)
        q_ref[h, :, :q_nope.shape[1]] = q_nope
        q_ref[h, :, q_nope.shape[1]:] = _rope_tail(qh[:, LANES:], cc_ref[...], ss_ref[...]).astype(BF16)
        yield
    lat = _rms_norm(proj_tail(_CKV0, _DT0), kvg_ref[...])
    lat_ref[...] = lat
    kr_tail = _rope_tail(dk[:, LANES:], cc_ref[...], ss_ref[...])
    kr_ref[...] = kr_tail[:, :QK_ROPE]
    yield
    if absorbed:
        return
    latb = lat.astype(BF16)
    kr_tail = kr_tail.astype(BF16)
    for h in range(MLA_HEADS):
        kv = _dot(latb, wukv_ref[:, h * kv_w:(h + 1) * kv_w])
        k_ref[h, :, :LANES] = kv[:, :QK_NOPE].astype(BF16)
        k_ref[h, :, LANES:] = kr_tail
        v_ref[:, h * V_HEAD:(h + 1) * V_HEAD] = kv[:, QK_NOPE:].astype(BF16)
        yield


def _in_proj_kernel(*refs, absorbed):
    for _ in _in_proj_units(*refs, absorbed=absorbed):
        pass


def _in_proj(x1, cc, ss, win, wtail, qg, wuq, kvg, wukv, *, tokens, tm, seq_len, absorbed):
    T = tokens
    nt = T // tm
    tok = lambda w: pl.BlockSpec((tm, w), lambda i: (i, 0))
    if cc.shape[0] == T:
        tab = pl.BlockSpec((tm, LANES), lambda i: (i, 0))
    else:
        per_seq = seq_len // tm
        tab = pl.BlockSpec((tm, LANES), lambda i: (i % per_seq, 0))
    q_width = KV_LORA + LANES if absorbed else QK_PAD
    head_tok = lambda w: pl.BlockSpec((MLA_HEADS, tm, w), lambda i: (0, i, 0))
    head_shape = lambda w: jax.ShapeDtypeStruct((MLA_HEADS, T, w), BF16)
    out_shape = [
        jax.ShapeDtypeStruct((T, SSD_D_INNER), F32),
        jax.ShapeDtypeStruct((T, SSD_CONV_DIM), F32),
        jax.ShapeDtypeStruct((T, LANES), F32),
        jax.ShapeDtypeStruct((T, KV_LORA), F32),
        jax.ShapeDtypeStruct((T, QK_ROPE), F32),
        head_shape(q_width),
    ]
    out_specs = [tok(SSD_D_INNER), tok(SSD_CONV_DIM), tok(LANES), tok(KV_LORA), tok(QK_ROPE),
                 head_tok(q_width)]
    weights = [win, wtail, qg, wuq, kvg, wukv]
    if not absorbed:
        out_shape += [head_shape(QK_PAD), jax.ShapeDtypeStruct((T, MLA_WIDTH), BF16)]
        out_specs += [head_tok(QK_PAD), tok(MLA_WIDTH)]
    return pl.pallas_call(
        functools.partial(_in_proj_kernel, absorbed=absorbed),
        grid=(nt,),
        in_specs=[tok(D_MODEL), tab, tab] + [_resident(w.shape) for w in weights],
        out_specs=out_specs,
        out_shape=out_shape,
        compiler_params=pltpu.CompilerParams(dimension_semantics=("parallel",),
                                             vmem_limit_bytes=VMEM_LIMIT),
        name="in_proj",
    )(x1, cc, ss, *weights)


def _split3(x):
    hi = x.astype(BF16)
    r1 = x - hi.astype(F32)
    mid = r1.astype(BF16)
    lo = (r1 - mid.astype(F32)).astype(BF16)
    return hi, mid, lo


def _softplus(x):
    return jnp.maximum(x, 0.0) + jnp.log1p(jnp.exp(-jnp.abs(x)))


def _ssd_block(out, xbc, hist, z, dt_raw, h_in, lb, cw_ref, cb_ref, dtb_ref, alog_ref, dsk_ref, ng_ref):
    Q = SSD_Q
    if lb < Q:
        xbc = jnp.concatenate([xbc, jnp.zeros((Q - lb, SSD_CONV_DIM), F32)], axis=0)
        dt_raw = jnp.concatenate([dt_raw, jnp.zeros((Q - lb, LANES), F32)], axis=0)
    xw = jnp.concatenate([hist, xbc], axis=0)
    conv = xw * cw_ref[0:1, :]
    for k in range(1, CONV_W):
        conv = pltpu.roll(conv, 1, 0) + xw * cw_ref[k:k + 1, :]
    conv = conv[8:, :] + cb_ref[...]
    xc = _silu(conv)
    xs = xc[:, :SSD_D_INNER]
    yield

    row_id = lax.broadcasted_iota(jnp.int32, (Q, LANES), 0)
    dt_col = jnp.where(row_id < lb, _softplus(dt_raw + dtb_ref[...]), 0.0)
    da_col = dt_col * (-jnp.exp(alog_ref[...]))
    ii = lax.broadcasted_iota(jnp.int32, (Q, Q), 0)
    jj = lax.broadcasted_iota(jnp.int32, (Q, Q), 1)
    causal = ii >= jj
    tri = causal.astype(BF16)
    a_col = sum(_dot(tri, p) for p in _split3(da_col)) * LOG2E
    a_row = a_col.T
    dt_row = dt_col.T
    e_col = jnp.exp2(a_col)
    a_last = a_row[:, Q - 1:Q]
    w_row = dt_row * jnp.exp2(a_last - a_row)
    e_last = jnp.exp2(jnp.broadcast_to(a_last, (LANES, LANES)))

    xs_t = xs.T
    yield
    lane = lax.broadcasted_iota(jnp.int32, (Q, LANES), 1)
    lo_half = lane < SSD_HEAD_DIM
    y_parts = []
    h_out = []
    for g in range(SSD_GROUPS):
        b_g = xc[:, SSD_D_INNER + g * SSD_STATE:SSD_D_INNER + (g + 1) * SSD_STATE].astype(BF16)
        c0 = SSD_D_INNER + SSD_GROUPS * SSD_STATE + g * SSD_STATE
        c_g = xc[:, c0:c0 + SSD_STATE].astype(BF16)
        cb = _dot_nt(c_g, b_g)
        h0 = g * SSD_HEADS_PER_GROUP
        hp = h_in[h0:h0 + SSD_HEADS_PER_GROUP].reshape(SSD_HEADS_PER_GROUP * SSD_HEAD_DIM, SSD_STATE)
        y_off = _dot_nt(c_g, hp.astype(BF16))
        lhs = []
        for hh in range(SSD_HEADS_PER_GROUP):
            h = h0 + hh
            lhs.append(xs_t[h * SSD_HEAD_DIM:(h + 1) * SSD_HEAD_DIM, :] * w_row[h:h + 1, :])
        st = _dot(jnp.concatenate(lhs, axis=0).astype(BF16), b_g)
        for hh in range(SSD_HEADS_PER_GROUP):
            h = h0 + hh
            h_out.append(e_last[h:h + 1, :] * h_in[h]
                         + st[hh * SSD_HEAD_DIM:(hh + 1) * SSD_HEAD_DIM, :])
        yield
        for pr in range(SSD_HEADS_PER_GROUP // 2):
            ha = h0 + 2 * pr
            ms = []
            for h in (ha, ha + 1):
                seg = a_col[:, h:h + 1] - a_row[h:h + 1, :]
                m = jnp.exp2(jnp.where(causal, seg, -jnp.inf)) * cb * dt_row[h:h + 1, :]
                ms.append(m.astype(BF16))
            xp = xs[:, ha * SSD_HEAD_DIM:(ha + 2) * SSD_HEAD_DIM]
            rhs = jnp.concatenate([jnp.where(lo_half, xp, 0.0), jnp.where(lo_half, 0.0, xp)],
                                  axis=0).astype(BF16)
            y_d = _dot(jnp.concatenate(ms, axis=1), rhs)
            yo = y_off[:, 2 * pr * SSD_HEAD_DIM:(2 * pr + 2) * SSD_HEAD_DIM]
            dec = jnp.where(lo_half, e_col[:, ha:ha + 1], e_col[:, ha + 1:ha + 2])
            y_parts.append(y_d + yo * dec)
            if pr % 2 == 1:
                yield
    y = jnp.concatenate(y_parts, axis=1) + dsk_ref[...] * xs
    y = y[:lb] * _silu(z)
    half = SSD_D_INNER // SSD_GROUPS
    ng = ng_ref[...]
    y = jnp.concatenate([_rms_norm(y[:, g * half:(g + 1) * half], ng[:, g * half:(g + 1) * half])
                         for g in range(SSD_GROUPS)], axis=1)
    out["y"] = y
    out["h"] = jnp.stack(h_out)


def _ssd_short_kernel(xbc_ref, z_ref, dt_ref, cprev_ref, h0_ref, *refs, seq_len, seqs):
    *params, y_ref, nconv_ref, hnew_ref, tail_ref = refs

    def one(s):
        r0 = s * seq_len
        tail_ref[s] = jnp.zeros(tail_ref.shape[1:], F32)
        tail_ref[s, 8 - (CONV_W - 1):, :] = cprev_ref[s]
        out = {}
        yield from _ssd_block(out, xbc_ref[r0:r0 + seq_len, :], tail_ref[s], z_ref[r0:r0 + seq_len, :],
                              dt_ref[r0:r0 + seq_len, :], h0_ref[s], seq_len, *params)
        y_ref[r0:r0 + seq_len, :] = out["y"].astype(BF16)
        nconv_ref[s] = xbc_ref[r0 + seq_len - (CONV_W - 1):r0 + seq_len, :]
        hnew_ref[s] = out["h"]

    active = [one(s) for s in range(seqs)]
    while active:
        active = [g for g in active if next(g, True) is None]


def _in_proj_ssd_kernel(x_ref, cc_ref, ss_ref, win_w, wtail, qg, wuq, kvg, wukv, cw, cb, dtb, alog, dsk, ng,
                        lat_ref, kr_ref, q_ref, k_ref, v_ref, y_ref, nconv_ref, hnew_ref,
                        zs_ref, xs_ref, dts_ref, tail_ref, h_ref, *, tiles_per_seq):
    i = pl.program_id(0)
    slot = i % 2
    tm = x_ref.shape[0]

    @pl.when(i == 0)
    def _():
        zs_ref[1] = jnp.zeros(zs_ref.shape[1:], F32)
        xs_ref[1] = jnp.zeros(xs_ref.shape[1:], F32)
        dts_ref[1] = jnp.zeros(dts_ref.shape[1:], F32)
        tail_ref[...] = jnp.zeros(tail_ref.shape, F32)
        h_ref[...] = jnp.zeros(h_ref.shape, F32)

    proj_units = _in_proj_units(x_ref, cc_ref, ss_ref, win_w, wtail, qg, wuq, kvg, wukv,
                                zs_ref.at[slot], xs_ref.at[slot], dts_ref.at[slot],
                                lat_ref, kr_ref, q_ref, k_ref, v_ref, absorbed=False)

    prev = 1 - slot
    fresh = (i - 1) % tiles_per_seq == 0
    hist = jnp.where(fresh, 0.0, tail_ref[...])
    h = jnp.where(fresh, 0.0, h_ref[...])
    next(proj_units, None)
    for sb in range(tm // SSD_Q):
        r0 = sb * SSD_Q
        if sb > 0:
            hist = xs_ref[prev, r0 - 8:r0, :]
        out = {}
        for _ in _ssd_block(out, xs_ref[prev, r0:r0 + SSD_Q, :], hist, zs_ref[prev, r0:r0 + SSD_Q, :],
                            dts_ref[prev, r0:r0 + SSD_Q, :], h, SSD_Q, cw, cb, dtb, alog, dsk, ng):
            next(proj_units, None)
        y_ref[r0:r0 + SSD_Q, :] = out["y"].astype(BF16)
        h = out["h"]
    for _ in proj_units:
        pass
    tail_ref[...] = xs_ref[prev, tm - 8:tm, :]
    h_ref[...] = h
    nconv_ref[0] = xs_ref[prev, tm - (CONV_W - 1):tm, :]
    hnew_ref[0] = h


def _in_proj_ssd(x1, cc, ss, win, wtail, qg, wuq, kvg, wukv, cw, cb, dtb, alog, dsk, ng, *,
                 lead_tiles, tm, seq_len):
    T = x1.shape[0] - lead_tiles * tm
    nt = T // tm
    per_seq = seq_len // tm
    nb = T // seq_len
    cur = lambda i: jnp.minimum(i, nt - 1)
    old = lambda i: jnp.maximum(i - 1, 0)
    tok = lambda w: pl.BlockSpec((tm, w), lambda i: (cur(i), 0))
    x_spec = pl.BlockSpec((tm, D_MODEL), lambda i: (cur(i) + lead_tiles, 0))
    tab = pl.BlockSpec((tm, LANES), lambda i: (cur(i) % per_seq, 0))
    head_tok = pl.BlockSpec((MLA_HEADS, tm, QK_PAD), lambda i: (0, cur(i), 0))
    head_shape = jax.ShapeDtypeStruct((MLA_HEADS, T, QK_PAD), BF16)
    weights = [win, wtail, qg, wuq, kvg, wukv, cw, cb, dtb, alog, dsk, ng]
    return pl.pallas_call(
        functools.partial(_in_proj_ssd_kernel, tiles_per_seq=per_seq),
        grid=(nt + 1,),
        in_specs=[x_spec, tab, tab] + [_resident(w.shape) for w in weights],
        out_specs=[tok(KV_LORA),
                   pl.BlockSpec((None, tm, QK_ROPE), lambda i: (cur(i) // per_seq, cur(i) % per_seq, 0)),
                   head_tok, head_tok, tok(MLA_WIDTH),
                   pl.BlockSpec((tm, SSD_D_INNER), lambda i: (old(i), 0)),
                   pl.BlockSpec((1, CONV_W - 1, SSD_CONV_DIM), lambda i: (old(i) // per_seq, 0, 0)),
                   pl.BlockSpec((1, SSD_HEADS, SSD_HEAD_DIM, SSD_STATE),
                                lambda i: (old(i) // per_seq, 0, 0, 0))],
        out_shape=[jax.ShapeDtypeStruct((T, KV_LORA), F32),
                   jax.ShapeDtypeStruct((nb, seq_len, QK_ROPE), F32),
                   head_shape, head_shape,
                   jax.ShapeDtypeStruct((T, MLA_WIDTH), BF16),
                   jax.ShapeDtypeStruct((T, SSD_D_INNER), BF16),
                   jax.ShapeDtypeStruct((nb, CONV_W - 1, SSD_CONV_DIM), F32),
                   jax.ShapeDtypeStruct((nb, SSD_HEADS, SSD_HEAD_DIM, SSD_STATE), F32)],
        scratch_shapes=[pltpu.VMEM((2, tm, SSD_D_INNER), F32),
                        pltpu.VMEM((2, tm, SSD_CONV_DIM), F32),
                        pltpu.VMEM((2, tm, LANES), F32),
                        pltpu.VMEM((8, SSD_CONV_DIM), F32),
                        pltpu.VMEM((SSD_HEADS, SSD_HEAD_DIM, SSD_STATE), F32)],
        compiler_params=pltpu.CompilerParams(dimension_semantics=("arbitrary",),
                                             vmem_limit_bytes=VMEM_LIMIT),
        name="in_proj_ssd",
    )(x1, cc, ss, *weights)


def _ssd_short(xbc, z, dt, cprev, h0, cw, cb, dtb, alog, dsk, ng, *, nb, seq_len):
    assert seq_len <= SSD_Q and seq_len % 8 == 0 and nb % SSD_SEQS_PER_STEP == 0
    seqs = SSD_SEQS_PER_STEP
    tokb = lambda w: pl.BlockSpec((seqs * seq_len, w), lambda b: (b, 0))
    conv_spec = pl.BlockSpec((seqs, CONV_W - 1, SSD_CONV_DIM), lambda b: (b, 0, 0))
    state_spec = pl.BlockSpec((seqs, SSD_HEADS, SSD_HEAD_DIM, SSD_STATE), lambda b: (b, 0, 0, 0))
    params = [cw, cb, dtb, alog, dsk, ng]
    return pl.pallas_call(
        functools.partial(_ssd_short_kernel, seq_len=seq_len, seqs=seqs),
        grid=(nb // seqs,),
        in_specs=[tokb(SSD_CONV_DIM), tokb(SSD_D_INNER), tokb(LANES), conv_spec, state_spec]
        + [pl.BlockSpec(p.shape, lambda b: (0, 0)) for p in params],
        out_specs=(tokb(SSD_D_INNER), conv_spec, state_spec),
        out_shape=(jax.ShapeDtypeStruct((nb * seq_len, SSD_D_INNER), BF16),
                   jax.ShapeDtypeStruct((nb, CONV_W - 1, SSD_CONV_DIM), F32),
                   jax.ShapeDtypeStruct((nb, SSD_HEADS, SSD_HEAD_DIM, SSD_STATE), F32)),
        scratch_shapes=[pltpu.VMEM((seqs, 8, SSD_CONV_DIM), F32)],
        compiler_params=pltpu.CompilerParams(dimension_semantics=("parallel",),
                                             vmem_limit_bytes=VMEM_LIMIT),
        name="ssd",
    )(xbc, z, dt, cprev, h0, *params)


def _lane_tile(x, width):
    return jnp.concatenate([x] * (width // LANES), axis=1)


def _attn_prompt_kernel(q_ref, k_ref, v_ref, o_ref, vx_ref, m_ref, acc_ref, *, seq_len, tq, tk, nh):
    ri = lax.broadcasted_iota(jnp.int32, (tk, tk), 0)
    ci = lax.broadcasted_iota(jnp.int32, (tk, tk), 1)
    diag_mask = (ci // CHUNK) <= (ri // CHUNK)

    def head_steps(hh):
        cols = slice(hh * V_HEAD, (hh + 1) * V_HEAD)
        vx_ref[hh, :, :V_HEAD] = v_ref[:, cols]
        vx_ref[hh, :, V_HEAD:] = jnp.ones((seq_len, LANES), BF16)

        def step(r0, k0, row0, masked):
            s = _dot_nt(q_ref[hh, r0 + row0:r0 + tq, :], k_ref[hh, k0:k0 + tk, :])
            if masked:
                top = jnp.where(diag_mask, s[:tk], NEG_BIG)
                s = top if tq - row0 == tk else jnp.concatenate([top, s[tk:]], axis=0)
            m_prev = m_ref[hh, row0:, :]
            m_next = jnp.maximum(m_prev, jnp.max(s, axis=-1, keepdims=True))
            alpha = jnp.exp2(m_prev - m_next)
            p = jnp.exp2(s - _lane_tile(m_next, tk))
            m_ref[hh, row0:, :] = m_next
            acc_ref[hh, row0:, :] = (_lane_tile(alpha, V_HEAD + LANES) * acc_ref[hh, row0:, :]
                                     + _dot(p.astype(BF16), vx_ref[hh, k0:k0 + tk, :]))

        for qb in range(seq_len // tq):
            r0 = qb * tq
            m_ref[hh] = jnp.full(m_ref.shape[1:], NEG_BIG, F32)
            acc_ref[hh] = jnp.zeros(acc_ref.shape[1:], F32)
            for j in range(r0 // tk):
                step(r0, j * tk, 0, False)
                yield
            for d in range(tq // tk):
                step(r0, r0 + d * tk, d * tk, True)
                yield
            o_ref[r0:r0 + tq, cols] = (acc_ref[hh, :, :V_HEAD] / acc_ref[hh, :, V_HEAD:]).astype(BF16)

    active = [head_steps(hh) for hh in range(nh)]
    while active:
        active = [g for g in active if next(g, True) is None]


def _attn_prompt(q, k, v, *, nb, seq_len):
    tq = min(ATT_TQ, seq_len)
    tk = min(ATT_TK, tq)
    nh = ATT_HEADS_PER_STEP
    scratch = [pltpu.VMEM((nh, seq_len, V_HEAD + LANES), BF16), pltpu.VMEM((nh, tq, LANES), F32),
               pltpu.VMEM((nh, tq, V_HEAD + LANES), F32)]
    return pl.pallas_call(
        functools.partial(_attn_prompt_kernel, seq_len=seq_len, tq=tq, tk=tk, nh=nh),
        grid=(nb, MLA_HEADS // nh),
        in_specs=[pl.BlockSpec((nh, seq_len, QK_PAD), lambda b, h: (h, b, 0)),
                  pl.BlockSpec((nh, seq_len, QK_PAD), lambda b, h: (h, b, 0)),
                  pl.BlockSpec((seq_len, nh * V_HEAD), lambda b, h: (b, h))],
        out_specs=pl.BlockSpec((seq_len, nh * V_HEAD), lambda b, h: (b, h)),
        out_shape=jax.ShapeDtypeStruct((nb * seq_len, MLA_WIDTH), BF16),
        scratch_shapes=scratch,
        compiler_params=pltpu.CompilerParams(dimension_semantics=("parallel", "parallel"),
                                             vmem_limit_bytes=VMEM_LIMIT),
        name="attn_prompt",
    )(q, k, v)


def _attn_sample_kernel(q_ref, latc_ref, krc_ref, latn_ref, krn_ref, o_ref, *, past_len, new_len):
    rows = MLA_HEADS * new_len
    qa = jnp.concatenate([q_ref[h] for h in range(MLA_HEADS)], axis=0)
    q_lat = qa[:, :KV_LORA]
    q_rope = qa[:, KV_LORA:KV_LORA + QK_ROPE]
    latc = latc_ref[...].astype(BF16)
    latn = latn_ref[...].astype(BF16)
    sc = _dot_nt(q_lat, latc) + _dot_nt(q_rope, krc_ref[...].astype(BF16))
    sn = _dot_nt(q_lat, latn) + _dot_nt(q_rope, krn_ref[...].astype(BF16))
    qpos = past_len + lax.broadcasted_iota(jnp.int32, (rows, 1), 0) % new_len
    kpos_c = lax.broadcasted_iota(jnp.int32, (1, past_len), 1)
    kpos_n = past_len + lax.broadcasted_iota(jnp.int32, (1, new_len), 1)
    sc = jnp.where((kpos_c // CHUNK) <= (qpos // CHUNK), sc, NEG_BIG)
    sn = jnp.where((kpos_n // CHUNK) <= (qpos // CHUNK), sn, NEG_BIG)
    m = jnp.maximum(jnp.max(sc, axis=-1, keepdims=True), jnp.max(sn, axis=-1, keepdims=True))
    pc = jnp.exp2(sc - m)
    pn = jnp.exp2(sn - m)
    l = jnp.sum(pc, axis=-1, keepdims=True) + jnp.sum(pn, axis=-1, keepdims=True)
    o = (_dot(pc.astype(BF16), latc) + _dot(pn.astype(BF16), latn)) / l
    for h in range(MLA_HEADS):
        o_ref[h] = o[h * new_len:(h + 1) * new_len].astype(BF16)


def _attn_sample(qa, lat_c, kr_c, lat_n, kr_n, *, nb, past_len, new_len):
    return pl.pallas_call(
        functools.partial(_attn_sample_kernel, past_len=past_len, new_len=new_len),
        grid=(nb,),
        in_specs=[pl.BlockSpec((MLA_HEADS, new_len, KV_LORA + LANES), lambda b: (0, b, 0)),
                  pl.BlockSpec((None, past_len, KV_LORA), lambda b: (b, 0, 0)),
                  pl.BlockSpec((None, past_len, QK_ROPE), lambda b: (b, 0, 0)),
                  pl.BlockSpec((new_len, KV_LORA), lambda b: (b, 0)),
                  pl.BlockSpec((new_len, QK_ROPE), lambda b: (b, 0))],
        out_specs=pl.BlockSpec((MLA_HEADS, new_len, KV_LORA), lambda b: (0, b, 0)),
        out_shape=jax.ShapeDtypeStruct((MLA_HEADS, nb * new_len, KV_LORA), BF16),
        compiler_params=pltpu.CompilerParams(dimension_semantics=("parallel",),
                                             vmem_limit_bytes=VMEM_LIMIT),
        name="attn_sample",
    )(qa, lat_c, kr_c, lat_n, kr_n)


def _rope_tables(pos):
    inv = ROPE_THETA ** (-jnp.arange(0, QK_ROPE, 2, dtype=F32) / QK_ROPE)
    ang = pos.astype(F32)[:, None] * inv[None, :]
    cos, sin = jnp.cos(ang), jnp.sin(ang)
    pad = jnp.zeros((pos.shape[0], LANES - QK_ROPE), F32)
    return (jnp.concatenate([cos, cos, pad], axis=1), jnp.concatenate([-sin, sin, pad], axis=1))


def _pack_weights(w_in, conv_w, conv_b, dt_bias, a_log, d_skip, ssd_norm_g, q_norm_g, w_uq,
                  kv_norm_g, w_ukv, w_out):
    i0 = SSD_D_INNER
    i1 = i0 + SSD_CONV_DIM
    i2 = i1 + SSD_HEADS
    i3 = i2 + Q_LORA
    i4 = i3 + KV_LORA
    w_dt = jnp.pad(w_in[:, i1:i2], ((0, 0), (0, LANES - SSD_HEADS)))
    w_kr = w_in[:, i4:]
    wtail = jnp.concatenate([w_in[:, i2:i3], w_in[:, i3:i4], w_dt, w_kr,
                             w_kr[:, QK_ROPE // 2:], w_kr[:, :QK_ROPE // 2]], axis=1)
    uq = (w_uq * (SM_SCALE * LOG2E)).reshape(Q_LORA, MLA_HEADS, QK_NOPE + QK_ROPE)
    rope = uq[:, :, QK_NOPE:]
    wuq = jnp.concatenate([uq, rope[:, :, QK_ROPE // 2:], rope[:, :, :QK_ROPE // 2]], axis=2)
    wuq = wuq.reshape(Q_LORA, MLA_HEADS * QK_PAD)
    lane_pad = lambda v: jnp.pad(v, (0, LANES - SSD_HEADS)).reshape(1, LANES)
    return dict(
        wtail=wtail, wuq=wuq,
        qg=q_norm_g.reshape(1, Q_LORA), kvg=kv_norm_g.reshape(1, KV_LORA),
        cw=conv_w, cb=conv_b.reshape(1, SSD_CONV_DIM),
        dtb=lane_pad(dt_bias), alog=lane_pad(a_log),
        dsk=jnp.repeat(d_skip, SSD_HEAD_DIM).reshape(1, SSD_D_INNER),
        ng=ssd_norm_g.reshape(1, SSD_D_INNER),
    )


def kernel(x_prompt, x_sample, cache_latent, cache_k_rope, state_conv, state_ssm, ln1_g, ln1_b, ffn1_w_gate, ffn1_w_up, ffn1_w_down, w_in, conv_w, conv_b, dt_bias, a_log, d_skip, ssd_norm_g, q_norm_g, w_uq, kv_norm_g, w_ukv, w_out, ln2_g, ln2_b, ffn2_w_gate, ffn2_w_up, ffn2_w_down, ln3_g, ln3_b):
    assert w_in.shape[0] == DEPTH == 1
    l = 0
    w = _pack_weights(w_in[l], conv_w[l], conv_b[l], dt_bias[l], a_log[l], d_skip[l],
                      ssd_norm_g[l], q_norm_g[l], w_uq[l], kv_norm_g[l], w_ukv[l], w_out[l])
    f1 = _cast_bf16([ffn1_w_gate, ffn1_w_up, ffn1_w_down])
    row = lambda v: v[l].reshape(1, D_MODEL)
    nbp, lp, _ = x_prompt.shape
    nbs, ls, _ = x_sample.shape
    tm = nbs * ls
    past = cache_latent.shape[2]
    assert tm == TOKEN_TILE and lp % tm == 0 and tm % SSD_Q == 0

    x1, cast = _ffn_ln(x_sample.reshape(tm, D_MODEL), x_prompt.reshape(nbp * lp, D_MODEL), *f1,
                       row(ln1_g), row(ln1_b),
                       casts=[ffn2_w_gate, ffn2_w_up, ffn2_w_down, w_in, w["wtail"], w["wuq"], w_ukv, w_out])
    f2 = cast[:3]
    win, wtail, wuq, wukv, wo = cast[3:]
    proj_w = (win, wtail, w["qg"], wuq, w["kvg"], wukv)
    ssd_w = (w["cw"], w["cb"], w["dtb"], w["alog"], w["dsk"], w["ng"])

    cc, ss = _rope_tables(jnp.arange(lp))
    lat_p, kr_p, q, kcat, v, y_p, conv_p, ssm_p = _in_proj_ssd(x1, cc, ss, *proj_w, *ssd_w, lead_tiles=1,
                                                               tm=tm, seq_len=lp)
    o_p = _attn_prompt(q, kcat, v, nb=nbp, seq_len=lp)

    cc, ss = _rope_tables(past + jnp.arange(ls))
    cc, ss = jnp.tile(cc, (nbs, 1)), jnp.tile(ss, (nbs, 1))
    z, xbc, dt, lat_s, kr_s, qa = _in_proj(x1, cc, ss, *proj_w, tokens=tm, tm=tm, seq_len=ls, absorbed=True)
    y_s, conv_s, ssm_s = _ssd_short(xbc, z, dt, state_conv[l], state_ssm[l], *ssd_w, nb=nbs, seq_len=ls)
    o_s = _attn_sample(qa, cache_latent[l], cache_k_rope[l], lat_s, kr_s, nb=nbs, past_len=past, new_len=ls)

    out_p, out_s = _mix_ffn_ln(x1, y_p, o_p, y_s, o_s, wukv, wo, row(ln2_g), row(ln2_b), *f2,
                               row(ln3_g), row(ln3_b))
    st = lambda a: a[None]
    return (out_p.reshape(nbp, lp, D_MODEL), out_s.reshape(nbs, ls, D_MODEL),
            st(lat_p.reshape(nbp, lp, KV_LORA)), st(kr_p), st(conv_p), st(ssm_p),
            st(lat_s.reshape(nbs, ls, KV_LORA)), st(kr_s.reshape(nbs, ls, QK_ROPE)), st(conv_s), st(ssm_s))
```

```python
import functools

import numpy as np
import jax
import jax.numpy as jnp
from jax import lax
from jax.experimental import pallas as pl
from jax.experimental.pallas import tpu as pltpu

F32 = jnp.float32
BF16 = jnp.bfloat16

D_MODEL = 1024
D_FF = 2816
CHUNK = 64
SSD_HEADS = 16
SSD_HEAD_DIM = 64
SSD_D_INNER = SSD_HEADS * SSD_HEAD_DIM
SSD_GROUPS = 2
SSD_HEADS_PER_GROUP = SSD_HEADS // SSD_GROUPS
SSD_STATE = 128
CONV_W = 4
SSD_CONV_DIM = SSD_D_INNER + 2 * SSD_GROUPS * SSD_STATE
MLA_HEADS = 8
Q_LORA = 512
KV_LORA = 512
QK_NOPE = 128
QK_ROPE = 64
V_HEAD = 128
MLA_WIDTH = MLA_HEADS * V_HEAD
ROPE_THETA = 10000.0
DEPTH = 1
ALPHA = (2 * DEPTH) ** 0.25
EPS = 1e-5
SM_SCALE = (QK_NOPE + QK_ROPE) ** -0.5
LOG2E = 1.4426950408889634
NEG_BIG = -1e30

LANES = 128
BF16_SUBLANES = 16
QK_PAD = 2 * LANES
SSD_Q = 128
FF_CHUNK = 256
TOKEN_TILE = 512
LN_ROW_BLOCKS = 8
SIDE_LAG = 2
CAST_STEPS = 16
SSD_SEQS_PER_STEP = 4
ATT_SAMPLE_SEQS_PER_STEP = 2
ATT_TQ = 1024
ATT_TK = 256
ATT_HEADS_PER_STEP = 4
VMEM_LIMIT = 56 * 1024 * 1024

_NT = (((1,), (1,)), ((), ()))


def _resident(shape):
    nd = len(shape)
    return pl.BlockSpec(shape, lambda *_: (0,) * nd, pipeline_mode=pl.Buffered(1))


def _dot(a, b):
    return jnp.dot(a, b, preferred_element_type=F32)


def _dot_nt(a, b):
    return lax.dot_general(a, b, _NT, preferred_element_type=F32)


def _sigmoid(x):
    return 1.0 / (1.0 + jnp.exp2(x * (-LOG2E)))


def _silu(x):
    return x * _sigmoid(x)


def _layer_norm(y, g, b):
    mu = jnp.mean(y, axis=-1, keepdims=True)
    d = y - mu
    var = jnp.mean(d * d, axis=-1, keepdims=True)
    return d * lax.rsqrt(var + EPS) * g + b


def _rms_norm(x, g):
    return x * lax.rsqrt(jnp.mean(x * x, axis=-1, keepdims=True) + EPS) * g


def _zero_after(values):
    t = values[0:8]
    for r in range(8, values.shape[0], 8):
        t = t + values[r:r + 8]
    u = t[:, :LANES]
    for c in range(LANES, values.shape[1], LANES):
        u = u + t[:, c:c + LANES]
    return pltpu.bitcast((pltpu.bitcast(u, jnp.uint32) >> 16) >> 16, F32)


def _ffn_residual(x, wg_ref, wu_ref, wd_ref, side=()):
    side = iter(side)
    rows = x.shape[0]
    xb = x.astype(BF16)
    acc = jnp.zeros((rows, D_MODEL), F32)
    due = {}
    for c in range(D_FF // FF_CHUNK):
        cs = slice(c * FF_CHUNK, (c + 1) * FF_CHUNK)
        h = _silu(_dot(xb, wg_ref[:, cs])) * _dot(xb, wu_ref[:, cs])
        for v in due.pop(c, ()):
            h = h + jnp.tile(_zero_after(v), (rows // 8, FF_CHUNK // LANES))
        acc = acc + _dot(h.astype(BF16), wd_ref[cs, :])
        produced = next(side, None)
        if produced is not None and c + SIDE_LAG < D_FF // FF_CHUNK:
            due[c + SIDE_LAG] = produced
    for _ in side:
        pass
    return ALPHA * x + 0.5 * acc


def _layer_norm_rows(src, g_ref, b_ref, dst_ref):
    nblk = LN_ROW_BLOCKS if dst_ref.shape[0] % (8 * LN_ROW_BLOCKS) == 0 else 1
    rows = dst_ref.shape[0] // nblk
    for r in range(nblk):
        sl = slice(r * rows, (r + 1) * rows)
        dst_ref[sl, :] = out = _layer_norm(src[sl, :], g_ref[...], b_ref[...])
        yield (out,)


def _run(gen):
    for _ in gen:
        pass


def _ffn_ln_kernel(xl_ref, x_ref, wg_ref, wu_ref, wd_ref, g_ref, b_ref, *refs):
    n_cast = (len(refs) - 2) // 2
    cast_in, o_ref, cast_out, pre_ref = refs[:n_cast], refs[n_cast], refs[n_cast + 1:-1], refs[-1]
    i = pl.program_id(0)
    nt = pl.num_programs(0) - 1

    def norm_prev():
        return _layer_norm_rows(pre_ref, g_ref, b_ref, o_ref)

    @pl.when(i == 0)
    def _():
        pre_ref[...] = jnp.zeros(pre_ref.shape, F32)

    @pl.when(i < nt)
    def _():
        x = jnp.where(i == 0, xl_ref[...], x_ref[...])
        pre = _ffn_residual(x, wg_ref, wu_ref, wd_ref, norm_prev())
        pre_ref[...] = pre
        for src, dst in zip(cast_in, cast_out):
            dst[...] = src[...].astype(BF16)

    @pl.when(i == nt)
    def _():
        _run(norm_prev())


def _cast_row_block(rows, steps):
    units = rows // BF16_SUBLANES
    nblk = max(d for d in range(1, min(units, steps) + 1) if units % d == 0)
    return rows // nblk, nblk


def _cast_specs(arrays, steps):
    cast_in, cast_out = [], []
    for a in arrays:
        rows, cols = a.shape[-2:]
        rpb, nblk = _cast_row_block(rows, steps)
        row_blk = lambda i, last=nblk - 1: jnp.minimum(i, last)
        cast_out.append(pl.BlockSpec((rpb, cols), lambda i, f=row_blk: (f(i), 0)))
        cast_in.append(cast_out[-1] if a.ndim == 2 else
                       pl.BlockSpec((None, rpb, cols), lambda i, f=row_blk: (0, f(i), 0)))
    return cast_in, cast_out


def _cast_kernel(*refs):
    n = len(refs) // 2
    for src, dst in zip(refs[:n], refs[n:]):
        dst[...] = src[...].astype(BF16)


def _cast_bf16(arrays):
    cast_in, cast_out = _cast_specs(arrays, CAST_STEPS)
    return pl.pallas_call(
        _cast_kernel,
        grid=(CAST_STEPS,),
        in_specs=cast_in,
        out_specs=cast_out,
        out_shape=[jax.ShapeDtypeStruct(a.shape[-2:], BF16) for a in arrays],
        compiler_params=pltpu.CompilerParams(dimension_semantics=("arbitrary",),
                                             vmem_limit_bytes=VMEM_LIMIT),
        name="cast_bf16",
    )(*arrays)


def _ffn_ln(x_lead, x, wg, wu, wd, g, b, casts=()):
    tm = x_lead.shape[0]
    nt = x.shape[0] // tm
    m = nt + 1
    cast_in, cast_out = _cast_specs(casts, m)
    out = pl.pallas_call(
        _ffn_ln_kernel,
        grid=(m + 1,),
        in_specs=[pl.BlockSpec((tm, D_MODEL), lambda i: (0, 0)),
                  pl.BlockSpec((tm, D_MODEL), lambda i: (jnp.clip(i - 1, 0, nt - 1), 0)),
                  _resident(wg.shape), _resident(wu.shape), _resident(wd.shape),
                  _resident(g.shape), _resident(b.shape)] + cast_in,
        out_specs=[pl.BlockSpec((tm, D_MODEL), lambda i: (jnp.maximum(i - 1, 0), 0))] + cast_out,
        out_shape=[jax.ShapeDtypeStruct((m * tm, D_MODEL), F32)]
        + [jax.ShapeDtypeStruct(a.shape[-2:], BF16) for a in casts],
        scratch_shapes=[pltpu.VMEM((tm, D_MODEL), F32)],
        compiler_params=pltpu.CompilerParams(dimension_semantics=("arbitrary",),
                                             vmem_limit_bytes=VMEM_LIMIT),
        name="ffn_ln",
    )(x_lead, x, wg, wu, wd, g, b, *casts)
    return out[0], out[1:]


def _mix_ffn_ln_kernel(x1_ref, y_ref, o_ref, yl_ref, ol_ref, wukv_ref, wo_ref, g2_ref, b2_ref,
                       wg_ref, wu_ref, wd_ref, g3_ref, b3_ref, out_ref, outl_ref, pre3_ref, olead_ref):
    i = pl.program_id(0)
    m = pl.num_programs(0) - 1

    @pl.when(i == 0)
    def _():
        pre3_ref[...] = jnp.zeros(pre3_ref.shape, F32)
        kv_w = QK_NOPE + V_HEAD
        for h in range(MLA_HEADS):
            olead_ref[:, h * V_HEAD:(h + 1) * V_HEAD] = _dot(
                ol_ref[h], wukv_ref[:, h * kv_w + QK_NOPE:(h + 1) * kv_w]).astype(BF16)

    def norm_prev():
        return _layer_norm_rows(pre3_ref, g3_ref, b3_ref, out_ref)

    @pl.when(i < m)
    def _():
        lead = i == 0
        y = jnp.where(lead, yl_ref[...], y_ref[...])
        o = jnp.where(lead, olead_ref[...], o_ref[...])
        mix = _dot(y, wo_ref[:SSD_D_INNER, :]) + _dot(o, wo_ref[SSD_D_INNER:, :])
        x2 = _layer_norm(ALPHA * x1_ref[...] + mix, g2_ref[...], b2_ref[...])
        pre3 = _ffn_residual(x2, wg_ref, wu_ref, wd_ref, norm_prev())
        pre3_ref[...] = pre3

    @pl.when(i == m)
    def _():
        _run(norm_prev())

    @pl.when(i == 1)
    def _():
        outl_ref[...] = out_ref[...]


def _mix_ffn_ln(x1, y, o, y_lead, o_lead, wukv, wo, g2, b2, wg, wu, wd, g3, b3):
    tm = y_lead.shape[0]
    nt = y.shape[0] // tm
    m = nt + 1
    main = pl.BlockSpec((tm, D_MODEL), lambda i: (jnp.clip(i - 1, 0, nt - 1), 0))
    weights = [wukv, wo, g2, b2, wg, wu, wd, g3, b3]
    return pl.pallas_call(
        _mix_ffn_ln_kernel,
        grid=(m + 1,),
        in_specs=[pl.BlockSpec((tm, D_MODEL), lambda i: (jnp.minimum(i, m - 1), 0)), main, main,
                  pl.BlockSpec((tm, D_MODEL), lambda i: (0, 0)),
                  pl.BlockSpec((MLA_HEADS, tm, KV_LORA), lambda i: (0, 0, 0))]
        + [_resident(w.shape) for w in weights],
        out_specs=[pl.BlockSpec((tm, D_MODEL), lambda i: (jnp.maximum(i - 2, 0), 0)),
                   pl.BlockSpec((tm, D_MODEL), lambda i: (0, 0))],
        out_shape=[jax.ShapeDtypeStruct((nt * tm, D_MODEL), F32),
                   jax.ShapeDtypeStruct((tm, D_MODEL), F32)],
        scratch_shapes=[pltpu.VMEM((tm, D_MODEL), F32), pltpu.VMEM((tm, MLA_WIDTH), BF16)],
        compiler_params=pltpu.CompilerParams(dimension_semantics=("arbitrary",),
                                             vmem_limit_bytes=VMEM_LIMIT),
        name="mix_ffn_ln",
    )(x1, y, o, y_lead, o_lead, *weights)


_Z0, _XBC0 = 0, SSD_D_INNER
_CQ0, _CKV0, _DT0, _KR0, _TAIL_COLS = 0, 512, 1024, 1152, 1280


def _rope_tail(rs, cc, ss):
    return rs * cc + pltpu.roll(rs, 64, 1) * ss


def _in_proj_units(x_ref, cc_ref, ss_ref, win_ref, wtail_ref, qg_ref, wuq_ref, kvg_ref, wukv_ref, *refs,
                   absorbed):
    if absorbed:
        z_ref, xbc_ref, dt_ref, lat_ref, kr_ref, q_ref = refs
    else:
        z_ref, xbc_ref, dt_ref, lat_ref, kr_ref, q_ref, k_ref, v_ref = refs
    wide = 2 * LANES
    kv_w = QK_NOPE + V_HEAD
    xb = x_ref[...].astype(BF16)

    def proj(c0, c1):
        return _dot(xb, win_ref[:, c0:c1])

    def proj_tail(c0, c1):
        return _dot(xb, wtail_ref[:, c0:c1])

    for c in range(0, SSD_D_INNER, wide):
        z_ref[:, c:c + wide] = proj(_Z0 + c, _Z0 + c + wide)
        yield
    dk = proj_tail(_DT0, _TAIL_COLS)
    dt_ref[...] = dk[:, :LANES]
    yield
    cq = _rms_norm(proj_tail(_CQ0, _CKV0), qg_ref[...]).astype(BF16)
    yield
    for c in range(0, SSD_CONV_DIM, wide):
        xbc_ref[:, c:c + wide] = proj(_XBC0 + c, _XBC0 + c + wide)
        yield
    for h in range(MLA_HEADS):
        qh = _dot(cq, wuq_ref[:, h * QK_PAD:(h + 1) * QK_PAD])
        q_nope = qh[:, :LANES].astype(BF16)
        if absorbed:
            q_nope = _dot_nt(q_nope, wukv_ref[:, h * kv_w:h * kv_w + QK_NOPE]).astype(BF16)
        q_ref[h, :, :q_nope.shape[1]] = q_nope
        q_ref[h, :, q_nope.shape[1]:] = _rope_tail(qh[:, LANES:], cc_ref[...], ss_ref[...]).astype(BF16)
        yield
    lat = _rms_norm(proj_tail(_CKV0, _DT0), kvg_ref[...])
    lat_ref[...] = lat
    kr_tail = _rope_tail(dk[:, LANES:], cc_ref[...], ss_ref[...])
    kr_ref[...] = kr_tail[:, :QK_ROPE]
    yield
    if absorbed:
        return
    latb = lat.astype(BF16)
    kr_tail = kr_tail.astype(BF16)
    for h in range(MLA_HEADS):
        kv = _dot(latb, wukv_ref[:, h * kv_w:(h + 1) * kv_w])
        k_ref[h, :, :LANES] = kv[:, :QK_NOPE].astype(BF16)
        k_ref[h, :, LANES:] = kr_tail
        v_ref[:, h * V_HEAD:(h + 1) * V_HEAD] = kv[:, QK_NOPE:].astype(BF16)
        yield


def _in_proj_kernel(*refs, absorbed):
    for _ in _in_proj_units(*refs, absorbed=absorbed):
        pass


def _in_proj(x1, cc, ss, win, wtail, qg, wuq, kvg, wukv, *, tokens, tm, seq_len, absorbed):
    T = tokens
    nt = T // tm
    tok = lambda w: pl.BlockSpec((tm, w), lambda i: (i, 0))
    if cc.shape[0] == T:
        tab = pl.BlockSpec((tm, LANES), lambda i: (i, 0))
    else:
        per_seq = seq_len // tm
        tab = pl.BlockSpec((tm, LANES), lambda i: (i % per_seq, 0))
    q_width = KV_LORA + LANES if absorbed else QK_PAD
    head_tok = lambda w: pl.BlockSpec((MLA_HEADS, tm, w), lambda i: (0, i, 0))
    head_shape = lambda w: jax.ShapeDtypeStruct((MLA_HEADS, T, w), BF16)
    out_shape = [
        jax.ShapeDtypeStruct((T, SSD_D_INNER), F32),
        jax.ShapeDtypeStruct((T, SSD_CONV_DIM), F32),
        jax.ShapeDtypeStruct((T, LANES), F32),
        jax.ShapeDtypeStruct((T, KV_LORA), F32),
        jax.ShapeDtypeStruct((T, QK_ROPE), F32),
        head_shape(q_width),
    ]
    out_specs = [tok(SSD_D_INNER), tok(SSD_CONV_DIM), tok(LANES), tok(KV_LORA), tok(QK_ROPE),
                 head_tok(q_width)]
    weights = [win, wtail, qg, wuq, kvg, wukv]
    if not absorbed:
        out_shape += [head_shape(QK_PAD), jax.ShapeDtypeStruct((T, MLA_WIDTH), BF16)]
        out_specs += [head_tok(QK_PAD), tok(MLA_WIDTH)]
    return pl.pallas_call(
        functools.partial(_in_proj_kernel, absorbed=absorbed),
        grid=(nt,),
        in_specs=[tok(D_MODEL), tab, tab] + [_resident(w.shape) for w in weights],
        out_specs=out_specs,
        out_shape=out_shape,
        compiler_params=pltpu.CompilerParams(dimension_semantics=("parallel",),
                                             vmem_limit_bytes=VMEM_LIMIT),
        name="in_proj",
    )(x1, cc, ss, *weights)


def _split3(x):
    hi = x.astype(BF16)
    r1 = x - hi.astype(F32)
    mid = r1.astype(BF16)
    lo = (r1 - mid.astype(F32)).astype(BF16)
    return hi, mid, lo


def _softplus(x):
    return jnp.maximum(x, 0.0) + jnp.log1p(jnp.exp(-jnp.abs(x)))


def _ssd_block(out, xbc, hist, z, dt_raw, h_in, lb, cw_ref, cb_ref, dtb_ref, alog_ref, dsk_ref, ng_ref):
    Q = SSD_Q
    if lb < Q:
        xbc = jnp.concatenate([xbc, jnp.zeros((Q - lb, SSD_CONV_DIM), F32)], axis=0)
        dt_raw = jnp.concatenate([dt_raw, jnp.zeros((Q - lb, LANES), F32)], axis=0)
    xw = jnp.concatenate([hist, xbc], axis=0)
    conv = xw * cw_ref[0:1, :]
    for k in range(1, CONV_W):
        conv = pltpu.roll(conv, 1, 0) + xw * cw_ref[k:k + 1, :]
    conv = conv[8:, :] + cb_ref[...]
    xc = _silu(conv)
    xs = xc[:, :SSD_D_INNER]
    yield

    row_id = lax.broadcasted_iota(jnp.int32, (Q, LANES), 0)
    dt_col = jnp.where(row_id < lb, _softplus(dt_raw + dtb_ref[...]), 0.0)
    da_col = dt_col * (-jnp.exp(alog_ref[...]))
    ii = lax.broadcasted_iota(jnp.int32, (Q, Q), 0)
    jj = lax.broadcasted_iota(jnp.int32, (Q, Q), 1)
    causal = ii >= jj
    tri = causal.astype(BF16)
    a_col = sum(_dot(tri, p) for p in _split3(da_col)) * LOG2E
    a_row = a_col.T
    dt_row = dt_col.T
    e_col = jnp.exp2(a_col)
    a_last = a_row[:, Q - 1:Q]
    w_row = dt_row * jnp.exp2(a_last - a_row)
    e_last = jnp.exp2(jnp.broadcast_to(a_last, (LANES, LANES)))

    xs_t = xs.T
    yield
    lane = lax.broadcasted_iota(jnp.int32, (Q, LANES), 1)
    lo_half = lane < SSD_HEAD_DIM
    y_parts = []
    h_out = []
    for g in range(SSD_GROUPS):
        b_g = xc[:, SSD_D_INNER + g * SSD_STATE:SSD_D_INNER + (g + 1) * SSD_STATE].astype(BF16)
        c0 = SSD_D_INNER + SSD_GROUPS * SSD_STATE + g * SSD_STATE
        c_g = xc[:, c0:c0 + SSD_STATE].astype(BF16)
        cb = _dot_nt(c_g, b_g)
        h0 = g * SSD_HEADS_PER_GROUP
        hp = h_in[h0:h0 + SSD_HEADS_PER_GROUP].reshape(SSD_HEADS_PER_GROUP * SSD_HEAD_DIM, SSD_STATE)
        y_off = _dot_nt(c_g, hp.astype(BF16))
        lhs = []
        for hh in range(SSD_HEADS_PER_GROUP):
            h = h0 + hh
            lhs.append(xs_t[h * SSD_HEAD_DIM:(h + 1) * SSD_HEAD_DIM, :] * w_row[h:h + 1, :])
        st = _dot(jnp.concatenate(lhs, axis=0).astype(BF16), b_g)
        for hh in range(SSD_HEADS_PER_GROUP):
            h = h0 + hh
            h_out.append(e_last[h:h + 1, :] * h_in[h]
                         + st[hh * SSD_HEAD_DIM:(hh + 1) * SSD_HEAD_DIM, :])
        yield
        for pr in range(SSD_HEADS_PER_GROUP // 2):
            ha = h0 + 2 * pr
            ms = []
            for h in (ha, ha + 1):
                seg = a_col[:, h:h + 1] - a_row[h:h + 1, :]
                m = jnp.exp2(jnp.where(causal, seg, -jnp.inf)) * cb * dt_row[h:h + 1, :]
                ms.append(m.astype(BF16))
            xp = xs[:, ha * SSD_HEAD_DIM:(ha + 2) * SSD_HEAD_DIM]
            rhs = jnp.concatenate([jnp.where(lo_half, xp, 0.0), jnp.where(lo_half, 0.0, xp)],
                                  axis=0).astype(BF16)
            y_d = _dot(jnp.concatenate(ms, axis=1), rhs)
            yo = y_off[:, 2 * pr * SSD_HEAD_DIM:(2 * pr + 2) * SSD_HEAD_DIM]
            dec = jnp.where(lo_half, e_col[:, ha:ha + 1], e_col[:, ha + 1:ha + 2])
            y_parts.append(y_d + yo * dec)
            if pr % 2 == 1:
                yield
    y = jnp.concatenate(y_parts, axis=1) + dsk_ref[...] * xs
    y = y[:lb] * _silu(z)
    half = SSD_D_INNER // SSD_GROUPS
    ng = ng_ref[...]
    y = jnp.concatenate([_rms_norm(y[:, g * half:(g + 1) * half], ng[:, g * half:(g + 1) * half])
                         for g in range(SSD_GROUPS)], axis=1)
    out["y"] = y
    out["h"] = jnp.stack(h_out)


def _ssd_short_kernel(xbc_ref, z_ref, dt_ref, cprev_ref, h0_ref, *refs, seq_len, seqs):
    *params, y_ref, nconv_ref, hnew_ref, tail_ref = refs

    def one(s):
        r0 = s * seq_len
        tail_ref[s] = jnp.zeros(tail_ref.shape[1:], F32)
        tail_ref[s, 8 - (CONV_W - 1):, :] = cprev_ref[s]
        out = {}
        yield from _ssd_block(out, xbc_ref[r0:r0 + seq_len, :], tail_ref[s], z_ref[r0:r0 + seq_len, :],
                              dt_ref[r0:r0 + seq_len, :], h0_ref[s], seq_len, *params)
        y_ref[r0:r0 + seq_len, :] = out["y"].astype(BF16)
        nconv_ref[s] = xbc_ref[r0 + seq_len - (CONV_W - 1):r0 + seq_len, :]
        hnew_ref[s] = out["h"]

    active = [one(s) for s in range(seqs)]
    while active:
        active = [g for g in active if next(g, True) is None]


def _in_proj_ssd_kernel(x_ref, cc_ref, ss_ref, win_w, wtail, qg, wuq, kvg, wukv, cw, cb, dtb, alog, dsk, ng,
                        lat_ref, kr_ref, q_ref, k_ref, v_ref, y_ref, nconv_ref, hnew_ref,
                        zs_ref, xs_ref, dts_ref, tail_ref, h_ref, *, tiles_per_seq):
    i = pl.program_id(0)
    slot = i % 2
    tm = x_ref.shape[0]

    @pl.when(i == 0)
    def _():
        zs_ref[1] = jnp.zeros(zs_ref.shape[1:], F32)
        xs_ref[1] = jnp.zeros(xs_ref.shape[1:], F32)
        dts_ref[1] = jnp.zeros(dts_ref.shape[1:], F32)
        tail_ref[...] = jnp.zeros(tail_ref.shape, F32)
        h_ref[...] = jnp.zeros(h_ref.shape, F32)

    proj_units = _in_proj_units(x_ref, cc_ref, ss_ref, win_w, wtail, qg, wuq, kvg, wukv,
                                zs_ref.at[slot], xs_ref.at[slot], dts_ref.at[slot],
                                lat_ref, kr_ref, q_ref, k_ref, v_ref, absorbed=False)

    prev = 1 - slot
    fresh = (i - 1) % tiles_per_seq == 0
    hist = jnp.where(fresh, 0.0, tail_ref[...])
    h = jnp.where(fresh, 0.0, h_ref[...])
    next(proj_units, None)
    for sb in range(tm // SSD_Q):
        r0 = sb * SSD_Q
        if sb > 0:
            hist = xs_ref[prev, r0 - 8:r0, :]
        out = {}
        for _ in _ssd_block(out, xs_ref[prev, r0:r0 + SSD_Q, :], hist, zs_ref[prev, r0:r0 + SSD_Q, :],
                            dts_ref[prev, r0:r0 + SSD_Q, :], h, SSD_Q, cw, cb, dtb, alog, dsk, ng):
            next(proj_units, None)
        y_ref[r0:r0 + SSD_Q, :] = out["y"].astype(BF16)
        h = out["h"]
    for _ in proj_units:
        pass
    tail_ref[...] = xs_ref[prev, tm - 8:tm, :]
    h_ref[...] = h
    nconv_ref[0] = xs_ref[prev, tm - (CONV_W - 1):tm, :]
    hnew_ref[0] = h


def _in_proj_ssd(x1, cc, ss, win, wtail, qg, wuq, kvg, wukv, cw, cb, dtb, alog, dsk, ng, *,
                 lead_tiles, tm, seq_len):
    T = x1.shape[0] - lead_tiles * tm
    nt = T // tm
    per_seq = seq_len // tm
    nb = T // seq_len
    cur = lambda i: jnp.minimum(i, nt - 1)
    old = lambda i: jnp.maximum(i - 1, 0)
    tok = lambda w: pl.BlockSpec((tm, w), lambda i: (cur(i), 0))
    x_spec = pl.BlockSpec((tm, D_MODEL), lambda i: (cur(i) + lead_tiles, 0))
    tab = pl.BlockSpec((tm, LANES), lambda i: (cur(i) % per_seq, 0))
    head_tok = pl.BlockSpec((MLA_HEADS, tm, QK_PAD), lambda i: (0, cur(i), 0))
    head_shape = jax.ShapeDtypeStruct((MLA_HEADS, T, QK_PAD), BF16)
    weights = [win, wtail, qg, wuq, kvg, wukv, cw, cb, dtb, alog, dsk, ng]
    return pl.pallas_call(
        functools.partial(_in_proj_ssd_kernel, tiles_per_seq=per_seq),
        grid=(nt + 1,),
        in_specs=[x_spec, tab, tab] + [_resident(w.shape) for w in weights],
        out_specs=[tok(KV_LORA),
                   pl.BlockSpec((None, tm, QK_ROPE), lambda i: (cur(i) // per_seq, cur(i) % per_seq, 0)),
                   head_tok, head_tok, tok(MLA_WIDTH),
                   pl.BlockSpec((tm, SSD_D_INNER), lambda i: (old(i), 0)),
                   pl.BlockSpec((1, CONV_W - 1, SSD_CONV_DIM), lambda i: (old(i) // per_seq, 0, 0)),
                   pl.BlockSpec((1, SSD_HEADS, SSD_HEAD_DIM, SSD_STATE),
                                lambda i: (old(i) // per_seq, 0, 0, 0))],
        out_shape=[jax.ShapeDtypeStruct((T, KV_LORA), F32),
                   jax.ShapeDtypeStruct((nb, seq_len, QK_ROPE), F32),
                   head_shape, head_shape,
                   jax.ShapeDtypeStruct((T, MLA_WIDTH), BF16),
                   jax.ShapeDtypeStruct((T, SSD_D_INNER), BF16),
                   jax.ShapeDtypeStruct((nb, CONV_W - 1, SSD_CONV_DIM), F32),
                   jax.ShapeDtypeStruct((nb, SSD_HEADS, SSD_HEAD_DIM, SSD_STATE), F32)],
        scratch_shapes=[pltpu.VMEM((2, tm, SSD_D_INNER), F32),
                        pltpu.VMEM((2, tm, SSD_CONV_DIM), F32),
                        pltpu.VMEM((2, tm, LANES), F32),
                        pltpu.VMEM((8, SSD_CONV_DIM), F32),
                        pltpu.VMEM((SSD_HEADS, SSD_HEAD_DIM, SSD_STATE), F32)],
        compiler_params=pltpu.CompilerParams(dimension_semantics=("arbitrary",),
                                             vmem_limit_bytes=VMEM_LIMIT),
        name="in_proj_ssd",
    )(x1, cc, ss, *weights)


def _ssd_short(xbc, z, dt, cprev, h0, cw, cb, dtb, alog, dsk, ng, *, nb, seq_len):
    assert seq_len <= SSD_Q and seq_len % 8 == 0 and nb % SSD_SEQS_PER_STEP == 0
    seqs = SSD_SEQS_PER_STEP
    tokb = lambda w: pl.BlockSpec((seqs * seq_len, w), lambda b: (b, 0))
    conv_spec = pl.BlockSpec((seqs, CONV_W - 1, SSD_CONV_DIM), lambda b: (b, 0, 0))
    state_spec = pl.BlockSpec((seqs, SSD_HEADS, SSD_HEAD_DIM, SSD_STATE), lambda b: (b, 0, 0, 0))
    params = [cw, cb, dtb, alog, dsk, ng]
    return pl.pallas_call(
        functools.partial(_ssd_short_kernel, seq_len=seq_len, seqs=seqs),
        grid=(nb // seqs,),
        in_specs=[tokb(SSD_CONV_DIM), tokb(SSD_D_INNER), tokb(LANES), conv_spec, state_spec]
        + [pl.BlockSpec(p.shape, lambda b: (0, 0)) for p in params],
        out_specs=(tokb(SSD_D_INNER), conv_spec, state_spec),
        out_shape=(jax.ShapeDtypeStruct((nb * seq_len, SSD_D_INNER), BF16),
                   jax.ShapeDtypeStruct((nb, CONV_W - 1, SSD_CONV_DIM), F32),
                   jax.ShapeDtypeStruct((nb, SSD_HEADS, SSD_HEAD_DIM, SSD_STATE), F32)),
        scratch_shapes=[pltpu.VMEM((seqs, 8, SSD_CONV_DIM), F32)],
        compiler_params=pltpu.CompilerParams(dimension_semantics=("parallel",),
                                             vmem_limit_bytes=VMEM_LIMIT),
        name="ssd",
    )(xbc, z, dt, cprev, h0, *params)


def _lane_tile(x, width):
    return jnp.concatenate([x] * (width // LANES), axis=1)


def _attn_prompt_kernel(q_ref, k_ref, v_ref, o_ref, vx_ref, m_ref, acc_ref, *, seq_len, tq, tk, nh):
    ri = lax.broadcasted_iota(jnp.int32, (tk, tk), 0)
    ci = lax.broadcasted_iota(jnp.int32, (tk, tk), 1)
    diag_mask = (ci // CHUNK) <= (ri // CHUNK)

    def head_steps(hh):
        cols = slice(hh * V_HEAD, (hh + 1) * V_HEAD)
        vx_ref[hh, :, :V_HEAD] = v_ref[:, cols]
        vx_ref[hh, :, V_HEAD:] = jnp.ones((seq_len, LANES), BF16)

        def step(r0, k0, row0, masked):
            s = _dot_nt(q_ref[hh, r0 + row0:r0 + tq, :], k_ref[hh, k0:k0 + tk, :])
            if masked:
                top = jnp.where(diag_mask, s[:tk], NEG_BIG)
                s = top if tq - row0 == tk else jnp.concatenate([top, s[tk:]], axis=0)
            m_prev = m_ref[hh, row0:, :]
            m_next = jnp.maximum(m_prev, jnp.max(s, axis=-1, keepdims=True))
            alpha = jnp.exp2(m_prev - m_next)
            p = jnp.exp2(s - _lane_tile(m_next, tk))
            m_ref[hh, row0:, :] = m_next
            acc_ref[hh, row0:, :] = (_lane_tile(alpha, V_HEAD + LANES) * acc_ref[hh, row0:, :]
                                     + _dot(p.astype(BF16), vx_ref[hh, k0:k0 + tk, :]))

        for qb in range(seq_len // tq):
            r0 = qb * tq
            m_ref[hh] = jnp.full(m_ref.shape[1:], NEG_BIG, F32)
            acc_ref[hh] = jnp.zeros(acc_ref.shape[1:], F32)
            for j in range(r0 // tk):
                step(r0, j * tk, 0, False)
                yield
            for d in range(tq // tk):
                step(r0, r0 + d * tk, d * tk, True)
                yield
            o_ref[r0:r0 + tq, cols] = (acc_ref[hh, :, :V_HEAD] / acc_ref[hh, :, V_HEAD:]).astype(BF16)

    active = [head_steps(hh) for hh in range(nh)]
    while active:
        active = [g for g in active if next(g, True) is None]


def _attn_prompt(q, k, v, *, nb, seq_len):
    tq = min(ATT_TQ, seq_len)
    tk = min(ATT_TK, tq)
    nh = ATT_HEADS_PER_STEP
    scratch = [pltpu.VMEM((nh, seq_len, V_HEAD + LANES), BF16), pltpu.VMEM((nh, tq, LANES), F32),
               pltpu.VMEM((nh, tq, V_HEAD + LANES), F32)]
    return pl.pallas_call(
        functools.partial(_attn_prompt_kernel, seq_len=seq_len, tq=tq, tk=tk, nh=nh),
        grid=(nb, MLA_HEADS // nh),
        in_specs=[pl.BlockSpec((nh, seq_len, QK_PAD), lambda b, h: (h, b, 0)),
                  pl.BlockSpec((nh, seq_len, QK_PAD), lambda b, h: (h, b, 0)),
                  pl.BlockSpec((seq_len, nh * V_HEAD), lambda b, h: (b, h))],
        out_specs=pl.BlockSpec((seq_len, nh * V_HEAD), lambda b, h: (b, h)),
        out_shape=jax.ShapeDtypeStruct((nb * seq_len, MLA_WIDTH), BF16),
        scratch_shapes=scratch,
        compiler_params=pltpu.CompilerParams(dimension_semantics=("parallel", "parallel"),
                                             vmem_limit_bytes=VMEM_LIMIT),
        name="attn_prompt",
    )(q, k, v)


def _attn_sample_kernel(q_ref, latc_ref, krc_ref, latn_ref, krn_ref, o_ref, *, past_len, new_len, seqs):
    rows = MLA_HEADS * new_len
    qpos = past_len + lax.broadcasted_iota(jnp.int32, (rows, 1), 0) % new_len
    kpos_c = lax.broadcasted_iota(jnp.int32, (1, past_len), 1)
    kpos_n = past_len + lax.broadcasted_iota(jnp.int32, (1, new_len), 1)
    mask_c = (kpos_c // CHUNK) <= (qpos // CHUNK)
    mask_n = (kpos_n // CHUNK) <= (qpos // CHUNK)

    def one(s):
        tok = slice(s * new_len, (s + 1) * new_len)
        qa = jnp.concatenate([q_ref[h, tok, :] for h in range(MLA_HEADS)], axis=0)
        q_lat = qa[:, :KV_LORA]
        q_rope = qa[:, KV_LORA:KV_LORA + QK_ROPE]
        latc = latc_ref[s].astype(BF16)
        latn = latn_ref[tok, :].astype(BF16)
        sc = _dot_nt(q_lat, latc) + _dot_nt(q_rope, krc_ref[s].astype(BF16))
        sn = _dot_nt(q_lat, latn) + _dot_nt(q_rope, krn_ref[tok, :].astype(BF16))
        yield
        sc = jnp.where(mask_c, sc, NEG_BIG)
        sn = jnp.where(mask_n, sn, NEG_BIG)
        m = jnp.maximum(jnp.max(sc, axis=-1, keepdims=True), jnp.max(sn, axis=-1, keepdims=True))
        pc = jnp.exp2(sc - m)
        pn = jnp.exp2(sn - m)
        l = jnp.sum(pc, axis=-1, keepdims=True) + jnp.sum(pn, axis=-1, keepdims=True)
        yield
        o = (_dot(pc.astype(BF16), latc) + _dot(pn.astype(BF16), latn)) / l
        for h in range(MLA_HEADS):
            o_ref[h, tok, :] = o[h * new_len:(h + 1) * new_len].astype(BF16)

    active = [one(s) for s in range(seqs)]
    while active:
        active = [g for g in active if next(g, True) is None]


def _attn_sample(qa, lat_c, kr_c, lat_n, kr_n, *, nb, past_len, new_len):
    seqs = ATT_SAMPLE_SEQS_PER_STEP
    assert nb % seqs == 0
    return pl.pallas_call(
        functools.partial(_attn_sample_kernel, past_len=past_len, new_len=new_len, seqs=seqs),
        grid=(nb // seqs,),
        in_specs=[pl.BlockSpec((MLA_HEADS, seqs * new_len, KV_LORA + LANES), lambda b: (0, b, 0)),
                  pl.BlockSpec((seqs, past_len, KV_LORA), lambda b: (b, 0, 0)),
                  pl.BlockSpec((seqs, past_len, QK_ROPE), lambda b: (b, 0, 0)),
                  pl.BlockSpec((seqs * new_len, KV_LORA), lambda b: (b, 0)),
                  pl.BlockSpec((seqs * new_len, QK_ROPE), lambda b: (b, 0))],
        out_specs=pl.BlockSpec((MLA_HEADS, seqs * new_len, KV_LORA), lambda b: (0, b, 0)),
        out_shape=jax.ShapeDtypeStruct((MLA_HEADS, nb * new_len, KV_LORA), BF16),
        compiler_params=pltpu.CompilerParams(dimension_semantics=("parallel",),
                                             vmem_limit_bytes=VMEM_LIMIT),
        name="attn_sample",
    )(qa, lat_c, kr_c, lat_n, kr_n)


def _rope_tables(pos0, length, repeats=1):
    inv = ROPE_THETA ** (-np.arange(0, QK_ROPE, 2, dtype=np.float64) / QK_ROPE)
    ang = np.arange(pos0, pos0 + length, dtype=np.float64)[:, None] * inv[None, :]
    cos, sin = np.cos(ang), np.sin(ang)
    pad = np.zeros((length, LANES - QK_ROPE))
    cc = np.tile(np.concatenate([cos, cos, pad], axis=1), (repeats, 1))
    ss = np.tile(np.concatenate([-sin, sin, pad], axis=1), (repeats, 1))
    return jnp.asarray(cc, F32), jnp.asarray(ss, F32)


def _pack_weights(w_in, conv_w, conv_b, dt_bias, a_log, d_skip, ssd_norm_g, q_norm_g, w_uq,
                  kv_norm_g, w_ukv, w_out):
    i0 = SSD_D_INNER
    i1 = i0 + SSD_CONV_DIM
    i2 = i1 + SSD_HEADS
    i3 = i2 + Q_LORA
    i4 = i3 + KV_LORA
    w_dt = jnp.pad(w_in[:, i1:i2], ((0, 0), (0, LANES - SSD_HEADS)))
    w_kr = w_in[:, i4:]
    wtail = jnp.concatenate([w_in[:, i2:i3], w_in[:, i3:i4], w_dt, w_kr,
                             w_kr[:, QK_ROPE // 2:], w_kr[:, :QK_ROPE // 2]], axis=1)
    uq = (w_uq * (SM_SCALE * LOG2E)).reshape(Q_LORA, MLA_HEADS, QK_NOPE + QK_ROPE)
    rope = uq[:, :, QK_NOPE:]
    wuq = jnp.concatenate([uq, rope[:, :, QK_ROPE // 2:], rope[:, :, :QK_ROPE // 2]], axis=2)
    wuq = wuq.reshape(Q_LORA, MLA_HEADS * QK_PAD)
    lane_pad = lambda v: jnp.pad(v, (0, LANES - SSD_HEADS)).reshape(1, LANES)
    return dict(
        win=w_in[:, :i1], wtail=wtail, wuq=wuq,
        qg=q_norm_g.reshape(1, Q_LORA), kvg=kv_norm_g.reshape(1, KV_LORA),
        cw=conv_w, cb=conv_b.reshape(1, SSD_CONV_DIM),
        dtb=lane_pad(dt_bias), alog=lane_pad(a_log),
        dsk=jnp.repeat(d_skip, SSD_HEAD_DIM).reshape(1, SSD_D_INNER),
        ng=ssd_norm_g.reshape(1, SSD_D_INNER),
    )


def kernel(x_prompt, x_sample, cache_latent, cache_k_rope, state_conv, state_ssm, ln1_g, ln1_b, ffn1_w_gate, ffn1_w_up, ffn1_w_down, w_in, conv_w, conv_b, dt_bias, a_log, d_skip, ssd_norm_g, q_norm_g, w_uq, kv_norm_g, w_ukv, w_out, ln2_g, ln2_b, ffn2_w_gate, ffn2_w_up, ffn2_w_down, ln3_g, ln3_b):
    assert w_in.shape[0] == DEPTH == 1
    l = 0
    w = _pack_weights(w_in[l], conv_w[l], conv_b[l], dt_bias[l], a_log[l], d_skip[l],
                      ssd_norm_g[l], q_norm_g[l], w_uq[l], kv_norm_g[l], w_ukv[l], w_out[l])
    f1 = _cast_bf16([ffn1_w_gate, ffn1_w_up, ffn1_w_down])
    row = lambda v: v[l].reshape(1, D_MODEL)
    nbp, lp, _ = x_prompt.shape
    nbs, ls, _ = x_sample.shape
    tm = nbs * ls
    past = cache_latent.shape[2]
    assert tm == TOKEN_TILE and lp % tm == 0 and tm % SSD_Q == 0

    x1, cast = _ffn_ln(x_sample.reshape(tm, D_MODEL), x_prompt.reshape(nbp * lp, D_MODEL), *f1,
                       row(ln1_g), row(ln1_b),
                       casts=[ffn2_w_gate, ffn2_w_up, ffn2_w_down, w["win"], w["wtail"], w["wuq"], w_ukv, w_out])
    f2 = cast[:3]
    win, wtail, wuq, wukv, wo = cast[3:]
    proj_w = (win, wtail, w["qg"], wuq, w["kvg"], wukv)
    ssd_w = (w["cw"], w["cb"], w["dtb"], w["alog"], w["dsk"], w["ng"])

    cc, ss = _rope_tables(0, lp)
    lat_p, kr_p, q, kcat, v, y_p, conv_p, ssm_p = _in_proj_ssd(x1, cc, ss, *proj_w, *ssd_w, lead_tiles=1,
                                                               tm=tm, seq_len=lp)
    o_p = _attn_prompt(q, kcat, v, nb=nbp, seq_len=lp)

    cc, ss = _rope_tables(past, ls, repeats=nbs)
    z, xbc, dt, lat_s, kr_s, qa = _in_proj(x1, cc, ss, *proj_w, tokens=tm, tm=tm, seq_len=ls, absorbed=True)
    y_s, conv_s, ssm_s = _ssd_short(xbc, z, dt, state_conv[l], state_ssm[l], *ssd_w, nb=nbs, seq_len=ls)
    o_s = _attn_sample(qa, cache_latent[l], cache_k_rope[l], lat_s, kr_s, nb=nbs, past_len=past, new_len=ls)

    out_p, out_s = _mix_ffn_ln(x1, y_p, o_p, y_s, o_s, wukv, wo, row(ln2_g), row(ln2_b), *f2,
                               row(ln3_g), row(ln3_b))
    st = lambda a: a[None]
    return (out_p.reshape(nbp, lp, D_MODEL), out_s.reshape(nbs, ls, D_MODEL),
            st(lat_p.reshape(nbp, lp, KV_LORA)), st(kr_p), st(conv_p), st(ssm_p),
            st(lat_s.reshape(nbs, ls, KV_LORA)), st(kr_s.reshape(nbs, ls, QK_ROPE)), st(conv_s), st(ssm_s))
```

```python
import functools

import numpy as np
import jax
import jax.numpy as jnp
from jax import lax
from jax.experimental import pallas as pl
from jax.experimental.pallas import tpu as pltpu

F32 = jnp.float32
BF16 = jnp.bfloat16

D_MODEL = 1024
D_FF = 2816
CHUNK = 64
SSD_HEADS = 16
SSD_HEAD_DIM = 64
SSD_D_INNER = SSD_HEADS * SSD_HEAD_DIM
SSD_GROUPS = 2
SSD_HEADS_PER_GROUP = SSD_HEADS // SSD_GROUPS
SSD_STATE = 128
CONV_W = 4
SSD_CONV_DIM = SSD_D_INNER + 2 * SSD_GROUPS * SSD_STATE
MLA_HEADS = 8
Q_LORA = 512
KV_LORA = 512
QK_NOPE = 128
QK_ROPE = 64
V_HEAD = 128
MLA_WIDTH = MLA_HEADS * V_HEAD
ROPE_THETA = 10000.0
DEPTH = 1
ALPHA = (2 * DEPTH) ** 0.25
EPS = 1e-5
SM_SCALE = (QK_NOPE + QK_ROPE) ** -0.5
LOG2E = 1.4426950408889634
NEG_BIG = -1e30

LANES = 128
BF16_SUBLANES = 16
QK_PAD = 2 * LANES
SSD_Q = 128
FF_CHUNK = 256
TOKEN_TILE = 512
LN_ROW_BLOCKS = 8
SIDE_LAG = 2
CAST_STEPS = 4
SSD_SEQS_PER_STEP = 4
ATT_SAMPLE_SEQS_PER_STEP = 2
ATT_TQ = 1024
ATT_TK = 256
ATT_HEADS_PER_STEP = 4
VMEM_LIMIT = 56 * 1024 * 1024

_NT = (((1,), (1,)), ((), ()))


def _resident(shape):
    nd = len(shape)
    return pl.BlockSpec(shape, lambda *_: (0,) * nd, pipeline_mode=pl.Buffered(1))


def _dot(a, b):
    return jnp.dot(a, b, preferred_element_type=F32)


def _dot_nt(a, b):
    return lax.dot_general(a, b, _NT, preferred_element_type=F32)


def _sigmoid(x):
    return 1.0 / (1.0 + jnp.exp2(x * (-LOG2E)))


def _silu(x):
    return x * _sigmoid(x)


def _layer_norm(y, g, b):
    mu = jnp.mean(y, axis=-1, keepdims=True)
    d = y - mu
    var = jnp.mean(d * d, axis=-1, keepdims=True)
    return d * lax.rsqrt(var + EPS) * g + b


def _rms_norm(x, g):
    return x * lax.rsqrt(jnp.mean(x * x, axis=-1, keepdims=True) + EPS) * g


def _zero_after(values):
    t = values[0:8]
    for r in range(8, values.shape[0], 8):
        t = t + values[r:r + 8]
    u = t[:, :LANES]
    for c in range(LANES, values.shape[1], LANES):
        u = u + t[:, c:c + LANES]
    return pltpu.bitcast((pltpu.bitcast(u, jnp.uint32) >> 16) >> 16, F32)


def _ffn_residual(x, wg_ref, wu_ref, wd_ref, side=()):
    side = iter(side)
    rows = x.shape[0]
    xb = x.astype(BF16)
    acc = jnp.zeros((rows, D_MODEL), F32)
    due = {}
    for c in range(D_FF // FF_CHUNK):
        cs = slice(c * FF_CHUNK, (c + 1) * FF_CHUNK)
        h = _silu(_dot(xb, wg_ref[:, cs])) * _dot(xb, wu_ref[:, cs])
        for v in due.pop(c, ()):
            h = h + jnp.tile(_zero_after(v), (rows // 8, FF_CHUNK // LANES))
        acc = acc + _dot(h.astype(BF16), wd_ref[cs, :])
        produced = next(side, None)
        if produced is not None and c + SIDE_LAG < D_FF // FF_CHUNK:
            due[c + SIDE_LAG] = produced
    for _ in side:
        pass
    return ALPHA * x + 0.5 * acc


def _layer_norm_rows(src, g_ref, b_ref, dst_ref):
    nblk = LN_ROW_BLOCKS if dst_ref.shape[0] % (8 * LN_ROW_BLOCKS) == 0 else 1
    rows = dst_ref.shape[0] // nblk
    for r in range(nblk):
        sl = slice(r * rows, (r + 1) * rows)
        dst_ref[sl, :] = out = _layer_norm(src[sl, :], g_ref[...], b_ref[...])
        yield (out,)


def _run(gen):
    for _ in gen:
        pass


def _ffn_ln_kernel(xl_ref, x_ref, wg_ref, wu_ref, wd_ref, g_ref, b_ref, *refs):
    n_cast = (len(refs) - 2) // 2
    cast_in, o_ref, cast_out, pre_ref = refs[:n_cast], refs[n_cast], refs[n_cast + 1:-1], refs[-1]
    i = pl.program_id(0)
    nt = pl.num_programs(0) - 1

    def norm_prev():
        return _layer_norm_rows(pre_ref, g_ref, b_ref, o_ref)

    @pl.when(i == 0)
    def _():
        pre_ref[...] = jnp.zeros(pre_ref.shape, F32)

    @pl.when(i < nt)
    def _():
        x = jnp.where(i == 0, xl_ref[...], x_ref[...])
        pre = _ffn_residual(x, wg_ref, wu_ref, wd_ref, norm_prev())
        pre_ref[...] = pre
        for src, dst in zip(cast_in, cast_out):
            dst[...] = src[...].astype(BF16)

    @pl.when(i == nt)
    def _():
        _run(norm_prev())


def _cast_row_block(rows, steps):
    units = rows // BF16_SUBLANES
    nblk = max(d for d in range(1, min(units, steps) + 1) if units % d == 0)
    return rows // nblk, nblk


def _cast_specs(arrays, steps):
    cast_in, cast_out = [], []
    for a in arrays:
        rows, cols = a.shape[-2:]
        rpb, nblk = _cast_row_block(rows, steps)
        row_blk = lambda i, last=nblk - 1: jnp.minimum(i, last)
        cast_out.append(pl.BlockSpec((rpb, cols), lambda i, f=row_blk: (f(i), 0)))
        cast_in.append(cast_out[-1] if a.ndim == 2 else
                       pl.BlockSpec((None, rpb, cols), lambda i, f=row_blk: (0, f(i), 0)))
    return cast_in, cast_out


def _cast_kernel(*refs):
    n = len(refs) // 2
    for src, dst in zip(refs[:n], refs[n:]):
        dst[...] = src[...].astype(BF16)


def _cast_bf16(arrays):
    cast_in, cast_out = _cast_specs(arrays, CAST_STEPS)
    return pl.pallas_call(
        _cast_kernel,
        grid=(CAST_STEPS,),
        in_specs=cast_in,
        out_specs=cast_out,
        out_shape=[jax.ShapeDtypeStruct(a.shape[-2:], BF16) for a in arrays],
        compiler_params=pltpu.CompilerParams(dimension_semantics=("arbitrary",),
                                             vmem_limit_bytes=VMEM_LIMIT),
        name="cast_bf16",
    )(*arrays)


def _ffn_ln(x_lead, x, wg, wu, wd, g, b, casts=()):
    tm = x_lead.shape[0]
    nt = x.shape[0] // tm
    m = nt + 1
    cast_in, cast_out = _cast_specs(casts, m)
    out = pl.pallas_call(
        _ffn_ln_kernel,
        grid=(m + 1,),
        in_specs=[pl.BlockSpec((tm, D_MODEL), lambda i: (0, 0)),
                  pl.BlockSpec((tm, D_MODEL), lambda i: (jnp.clip(i - 1, 0, nt - 1), 0)),
                  _resident(wg.shape), _resident(wu.shape), _resident(wd.shape),
                  _resident(g.shape), _resident(b.shape)] + cast_in,
        out_specs=[pl.BlockSpec((tm, D_MODEL), lambda i: (jnp.maximum(i - 1, 0), 0))] + cast_out,
        out_shape=[jax.ShapeDtypeStruct((m * tm, D_MODEL), F32)]
        + [jax.ShapeDtypeStruct(a.shape[-2:], BF16) for a in casts],
        scratch_shapes=[pltpu.VMEM((tm, D_MODEL), F32)],
        compiler_params=pltpu.CompilerParams(dimension_semantics=("arbitrary",),
                                             vmem_limit_bytes=VMEM_LIMIT),
        name="ffn_ln",
    )(x_lead, x, wg, wu, wd, g, b, *casts)
    return out[0], out[1:]


def _mix_ffn_ln_kernel(x1_ref, y_ref, o_ref, yl_ref, ol_ref, wukv_ref, wo_ref, g2_ref, b2_ref,
                       wg_ref, wu_ref, wd_ref, g3_ref, b3_ref, out_ref, outl_ref, pre3_ref, olead_ref):
    i = pl.program_id(0)
    m = pl.num_programs(0) - 1

    @pl.when(i == 0)
    def _():
        pre3_ref[...] = jnp.zeros(pre3_ref.shape, F32)
        kv_w = QK_NOPE + V_HEAD
        for h in range(MLA_HEADS):
            olead_ref[:, h * V_HEAD:(h + 1) * V_HEAD] = _dot(
                ol_ref[h], wukv_ref[:, h * kv_w + QK_NOPE:(h + 1) * kv_w]).astype(BF16)

    def norm_prev():
        return _layer_norm_rows(pre3_ref, g3_ref, b3_ref, out_ref)

    @pl.when(i < m)
    def _():
        lead = i == 0
        y = jnp.where(lead, yl_ref[...], y_ref[...])
        o = jnp.where(lead, olead_ref[...], o_ref[...])
        mix = _dot(y, wo_ref[:SSD_D_INNER, :]) + _dot(o, wo_ref[SSD_D_INNER:, :])
        x2 = _layer_norm(ALPHA * x1_ref[...] + mix, g2_ref[...], b2_ref[...])
        pre3 = _ffn_residual(x2, wg_ref, wu_ref, wd_ref, norm_prev())
        pre3_ref[...] = pre3

    @pl.when(i == m)
    def _():
        _run(norm_prev())

    @pl.when(i == 1)
    def _():
        outl_ref[...] = out_ref[...]


def _mix_ffn_ln(x1, y, o, y_lead, o_lead, wukv, wo, g2, b2, wg, wu, wd, g3, b3):
    tm = y_lead.shape[0]
    nt = y.shape[0] // tm
    m = nt + 1
    main = pl.BlockSpec((tm, D_MODEL), lambda i: (jnp.clip(i - 1, 0, nt - 1), 0))
    weights = [wukv, wo, g2, b2, wg, wu, wd, g3, b3]
    return pl.pallas_call(
        _mix_ffn_ln_kernel,
        grid=(m + 1,),
        in_specs=[pl.BlockSpec((tm, D_MODEL), lambda i: (jnp.minimum(i, m - 1), 0)), main, main,
                  pl.BlockSpec((tm, D_MODEL), lambda i: (0, 0)),
                  pl.BlockSpec((MLA_HEADS, tm, KV_LORA), lambda i: (0, 0, 0))]
        + [_resident(w.shape) for w in weights],
        out_specs=[pl.BlockSpec((tm, D_MODEL), lambda i: (jnp.maximum(i - 2, 0), 0)),
                   pl.BlockSpec((tm, D_MODEL), lambda i: (0, 0))],
        out_shape=[jax.ShapeDtypeStruct((nt * tm, D_MODEL), F32),
                   jax.ShapeDtypeStruct((tm, D_MODEL), F32)],
        scratch_shapes=[pltpu.VMEM((tm, D_MODEL), F32), pltpu.VMEM((tm, MLA_WIDTH), BF16)],
        compiler_params=pltpu.CompilerParams(dimension_semantics=("arbitrary",),
                                             vmem_limit_bytes=VMEM_LIMIT),
        name="mix_ffn_ln",
    )(x1, y, o, y_lead, o_lead, *weights)


_Z0, _XBC0 = 0, SSD_D_INNER
_CQ0, _CKV0, _DT0, _KR0, _TAIL_COLS = 0, 512, 1024, 1152, 1280


def _rope_tail(rs, cc, ss):
    return rs * cc + pltpu.roll(rs, 64, 1) * ss


def _in_proj_units(x_ref, cc_ref, ss_ref, win_ref, wtail_ref, qg_ref, wuq_ref, kvg_ref, wukv_ref, *refs,
                   absorbed):
    if absorbed:
        z_ref, xbc_ref, dt_ref, lat_ref, kr_ref, q_ref = refs
    else:
        z_ref, xbc_ref, dt_ref, lat_ref, kr_ref, q_ref, k_ref, v_ref = refs
    wide = 2 * LANES
    kv_w = QK_NOPE + V_HEAD
    xb = x_ref[...].astype(BF16)

    def proj(c0, c1):
        return _dot(xb, win_ref[:, c0:c1])

    def proj_tail(c0, c1):
        return _dot(xb, wtail_ref[:, c0:c1])

    for c in range(0, SSD_D_INNER, wide):
        z_ref[:, c:c + wide] = proj(_Z0 + c, _Z0 + c + wide)
        yield
    dk = proj_tail(_DT0, _TAIL_COLS)
    dt_ref[...] = dk[:, :LANES]
    yield
    cq = _rms_norm(proj_tail(_CQ0, _CKV0), qg_ref[...]).astype(BF16)
    yield
    for c in range(0, SSD_CONV_DIM, wide):
        xbc_ref[:, c:c + wide] = proj(_XBC0 + c, _XBC0 + c + wide)
        yield
    for h in range(MLA_HEADS):
        qh = _dot(cq, wuq_ref[:, h * QK_PAD:(h + 1) * QK_PAD])
        q_nope = qh[:, :LANES].astype(BF16)
        if absorbed:
            q_nope = _dot_nt(q_nope, wukv_ref[:, h * kv_w:h * kv_w + QK_NOPE]).astype(BF16)
        q_ref[h, :, :q_nope.shape[1]] = q_nope
        q_ref[h, :, q_nope.shape[1]:] = _rope_tail(qh[:, LANES:], cc_ref[...], ss_ref[...]).astype(BF16)
        yield
    lat = _rms_norm(proj_tail(_CKV0, _DT0), kvg_ref[...])
    lat_ref[...] = lat
    kr_tail = _rope_tail(dk[:, LANES:], cc_ref[...], ss_ref[...])
    kr_ref[...] = kr_tail[:, :QK_ROPE]
    yield
    if absorbed:
        return
    latb = lat.astype(BF16)
    kr_tail = kr_tail.astype(BF16)
    for h in range(MLA_HEADS):
        kv = _dot(latb, wukv_ref[:, h * kv_w:(h + 1) * kv_w])
        k_ref[h, :, :LANES] = kv[:, :QK_NOPE].astype(BF16)
        k_ref[h, :, LANES:] = kr_tail
        v_ref[:, h * V_HEAD:(h + 1) * V_HEAD] = kv[:, QK_NOPE:].astype(BF16)
        yield


def _in_proj_kernel(*refs, absorbed):
    for _ in _in_proj_units(*refs, absorbed=absorbed):
        pass


def _in_proj(x1, cc, ss, win, wtail, qg, wuq, kvg, wukv, *, tokens, tm, seq_len, absorbed):
    T = tokens
    nt = T // tm
    tok = lambda w: pl.BlockSpec((tm, w), lambda i: (i, 0))
    if cc.shape[0] == T:
        tab = pl.BlockSpec((tm, LANES), lambda i: (i, 0))
    else:
        per_seq = seq_len // tm
        tab = pl.BlockSpec((tm, LANES), lambda i: (i % per_seq, 0))
    q_width = KV_LORA + LANES if absorbed else QK_PAD
    head_tok = lambda w: pl.BlockSpec((MLA_HEADS, tm, w), lambda i: (0, i, 0))
    head_shape = lambda w: jax.ShapeDtypeStruct((MLA_HEADS, T, w), BF16)
    out_shape = [
        jax.ShapeDtypeStruct((T, SSD_D_INNER), F32),
        jax.ShapeDtypeStruct((T, SSD_CONV_DIM), F32),
        jax.ShapeDtypeStruct((T, LANES), F32),
        jax.ShapeDtypeStruct((T, KV_LORA), F32),
        jax.ShapeDtypeStruct((T, QK_ROPE), F32),
        head_shape(q_width),
    ]
    out_specs = [tok(SSD_D_INNER), tok(SSD_CONV_DIM), tok(LANES), tok(KV_LORA), tok(QK_ROPE),
                 head_tok(q_width)]
    weights = [win, wtail, qg, wuq, kvg, wukv]
    if not absorbed:
        out_shape += [head_shape(QK_PAD), jax.ShapeDtypeStruct((T, MLA_WIDTH), BF16)]
        out_specs += [head_tok(QK_PAD), tok(MLA_WIDTH)]
    return pl.pallas_call(
        functools.partial(_in_proj_kernel, absorbed=absorbed),
        grid=(nt,),
        in_specs=[tok(D_MODEL), tab, tab] + [_resident(w.shape) for w in weights],
        out_specs=out_specs,
        out_shape=out_shape,
        compiler_params=pltpu.CompilerParams(dimension_semantics=("parallel",),
                                             vmem_limit_bytes=VMEM_LIMIT),
        name="in_proj",
    )(x1, cc, ss, *weights)


def _split3(x):
    hi = x.astype(BF16)
    r1 = x - hi.astype(F32)
    mid = r1.astype(BF16)
    lo = (r1 - mid.astype(F32)).astype(BF16)
    return hi, mid, lo


def _softplus(x):
    return jnp.maximum(x, 0.0) + jnp.log1p(jnp.exp(-jnp.abs(x)))


def _ssd_block(out, xbc, hist, z, dt_raw, h_in, lb, cw_ref, cb_ref, dtb_ref, alog_ref, dsk_ref, ng_ref):
    Q = SSD_Q
    if lb < Q:
        xbc = jnp.concatenate([xbc, jnp.zeros((Q - lb, SSD_CONV_DIM), F32)], axis=0)
        dt_raw = jnp.concatenate([dt_raw, jnp.zeros((Q - lb, LANES), F32)], axis=0)
    xw = jnp.concatenate([hist, xbc], axis=0)
    conv = xw * cw_ref[0:1, :]
    for k in range(1, CONV_W):
        conv = pltpu.roll(conv, 1, 0) + xw * cw_ref[k:k + 1, :]
    conv = conv[8:, :] + cb_ref[...]
    xc = _silu(conv)
    xs = xc[:, :SSD_D_INNER]
    yield

    row_id = lax.broadcasted_iota(jnp.int32, (Q, LANES), 0)
    dt_col = jnp.where(row_id < lb, _softplus(dt_raw + dtb_ref[...]), 0.0)
    da_col = dt_col * (-jnp.exp(alog_ref[...]))
    ii = lax.broadcasted_iota(jnp.int32, (Q, Q), 0)
    jj = lax.broadcasted_iota(jnp.int32, (Q, Q), 1)
    causal = ii >= jj
    tri = causal.astype(BF16)
    a_col = sum(_dot(tri, p) for p in _split3(da_col)) * LOG2E
    a_row = a_col.T
    dt_row = dt_col.T
    e_col = jnp.exp2(a_col)
    a_last = a_row[:, Q - 1:Q]
    w_row = dt_row * jnp.exp2(a_last - a_row)
    e_last = jnp.exp2(jnp.broadcast_to(a_last, (LANES, LANES)))

    xs_t = xs.T
    yield
    lane = lax.broadcasted_iota(jnp.int32, (Q, LANES), 1)
    lo_half = lane < SSD_HEAD_DIM
    y_parts = []
    h_out = []
    for g in range(SSD_GROUPS):
        b_g = xc[:, SSD_D_INNER + g * SSD_STATE:SSD_D_INNER + (g + 1) * SSD_STATE].astype(BF16)
        c0 = SSD_D_INNER + SSD_GROUPS * SSD_STATE + g * SSD_STATE
        c_g = xc[:, c0:c0 + SSD_STATE].astype(BF16)
        cb = _dot_nt(c_g, b_g)
        h0 = g * SSD_HEADS_PER_GROUP
        hp = h_in[h0:h0 + SSD_HEADS_PER_GROUP].reshape(SSD_HEADS_PER_GROUP * SSD_HEAD_DIM, SSD_STATE)
        y_off = _dot_nt(c_g, hp.astype(BF16))
        lhs = []
        for hh in range(SSD_HEADS_PER_GROUP):
            h = h0 + hh
            lhs.append(xs_t[h * SSD_HEAD_DIM:(h + 1) * SSD_HEAD_DIM, :] * w_row[h:h + 1, :])
        st = _dot(jnp.concatenate(lhs, axis=0).astype(BF16), b_g)
        for hh in range(SSD_HEADS_PER_GROUP):
            h = h0 + hh
            h_out.append(e_last[h:h + 1, :] * h_in[h]
                         + st[hh * SSD_HEAD_DIM:(hh + 1) * SSD_HEAD_DIM, :])
        yield
        for pr in range(SSD_HEADS_PER_GROUP // 2):
            ha = h0 + 2 * pr
            ms = []
            for h in (ha, ha + 1):
                seg = a_col[:, h:h + 1] - a_row[h:h + 1, :]
                m = jnp.exp2(jnp.where(causal, seg, -jnp.inf)) * cb * dt_row[h:h + 1, :]
                ms.append(m.astype(BF16))
            xp = xs[:, ha * SSD_HEAD_DIM:(ha + 2) * SSD_HEAD_DIM]
            rhs = jnp.concatenate([jnp.where(lo_half, xp, 0.0), jnp.where(lo_half, 0.0, xp)],
                                  axis=0).astype(BF16)
            y_d = _dot(jnp.concatenate(ms, axis=1), rhs)
            yo = y_off[:, 2 * pr * SSD_HEAD_DIM:(2 * pr + 2) * SSD_HEAD_DIM]
            dec = jnp.where(lo_half, e_col[:, ha:ha + 1], e_col[:, ha + 1:ha + 2])
            y_parts.append(y_d + yo * dec)
            if pr % 2 == 1:
                yield
    y = jnp.concatenate(y_parts, axis=1) + dsk_ref[...] * xs
    y = y[:lb] * _silu(z)
    half = SSD_D_INNER // SSD_GROUPS
    ng = ng_ref[...]
    y = jnp.concatenate([_rms_norm(y[:, g * half:(g + 1) * half], ng[:, g * half:(g + 1) * half])
                         for g in range(SSD_GROUPS)], axis=1)
    out["y"] = y
    out["h"] = jnp.stack(h_out)


def _ssd_short_kernel(xbc_ref, z_ref, dt_ref, cprev_ref, h0_ref, *refs, seq_len, seqs):
    *params, y_ref, nconv_ref, hnew_ref, tail_ref = refs

    def one(s):
        r0 = s * seq_len
        tail_ref[s] = jnp.zeros(tail_ref.shape[1:], F32)
        tail_ref[s, 8 - (CONV_W - 1):, :] = cprev_ref[s]
        out = {}
        yield from _ssd_block(out, xbc_ref[r0:r0 + seq_len, :], tail_ref[s], z_ref[r0:r0 + seq_len, :],
                              dt_ref[r0:r0 + seq_len, :], h0_ref[s], seq_len, *params)
        y_ref[r0:r0 + seq_len, :] = out["y"].astype(BF16)
        nconv_ref[s] = xbc_ref[r0 + seq_len - (CONV_W - 1):r0 + seq_len, :]
        hnew_ref[s] = out["h"]

    active = [one(s) for s in range(seqs)]
    while active:
        active = [g for g in active if next(g, True) is None]


def _in_proj_ssd_kernel(x_ref, cc_ref, ss_ref, win_w, wtail, qg, wuq, kvg, wukv, cw, cb, dtb, alog, dsk, ng,
                        lat_ref, kr_ref, q_ref, k_ref, v_ref, y_ref, nconv_ref, hnew_ref,
                        zs_ref, xs_ref, dts_ref, tail_ref, h_ref, *, tiles_per_seq):
    i = pl.program_id(0)
    slot = i % 2
    tm = x_ref.shape[0]

    @pl.when(i == 0)
    def _():
        zs_ref[1] = jnp.zeros(zs_ref.shape[1:], F32)
        xs_ref[1] = jnp.zeros(xs_ref.shape[1:], F32)
        dts_ref[1] = jnp.zeros(dts_ref.shape[1:], F32)
        tail_ref[...] = jnp.zeros(tail_ref.shape, F32)
        h_ref[...] = jnp.zeros(h_ref.shape, F32)

    proj_units = _in_proj_units(x_ref, cc_ref, ss_ref, win_w, wtail, qg, wuq, kvg, wukv,
                                zs_ref.at[slot], xs_ref.at[slot], dts_ref.at[slot],
                                lat_ref, kr_ref, q_ref, k_ref, v_ref, absorbed=False)

    prev = 1 - slot
    fresh = (i - 1) % tiles_per_seq == 0
    hist = jnp.where(fresh, 0.0, tail_ref[...])
    h = jnp.where(fresh, 0.0, h_ref[...])
    next(proj_units, None)
    for sb in range(tm // SSD_Q):
        r0 = sb * SSD_Q
        if sb > 0:
            hist = xs_ref[prev, r0 - 8:r0, :]
        out = {}
        for _ in _ssd_block(out, xs_ref[prev, r0:r0 + SSD_Q, :], hist, zs_ref[prev, r0:r0 + SSD_Q, :],
                            dts_ref[prev, r0:r0 + SSD_Q, :], h, SSD_Q, cw, cb, dtb, alog, dsk, ng):
            next(proj_units, None)
        y_ref[r0:r0 + SSD_Q, :] = out["y"].astype(BF16)
        h = out["h"]
    for _ in proj_units:
        pass
    tail_ref[...] = xs_ref[prev, tm - 8:tm, :]
    h_ref[...] = h
    nconv_ref[0] = xs_ref[prev, tm - (CONV_W - 1):tm, :]
    hnew_ref[0] = h


def _in_proj_ssd(x1, cc, ss, win, wtail, qg, wuq, kvg, wukv, cw, cb, dtb, alog, dsk, ng, *,
                 lead_tiles, tm, seq_len):
    T = x1.shape[0] - lead_tiles * tm
    nt = T // tm
    per_seq = seq_len // tm
    nb = T // seq_len
    cur = lambda i: jnp.minimum(i, nt - 1)
    old = lambda i: jnp.maximum(i - 1, 0)
    tok = lambda w: pl.BlockSpec((tm, w), lambda i: (cur(i), 0))
    x_spec = pl.BlockSpec((tm, D_MODEL), lambda i: (cur(i) + lead_tiles, 0))
    tab = pl.BlockSpec((tm, LANES), lambda i: (cur(i) % per_seq, 0))
    head_tok = pl.BlockSpec((MLA_HEADS, tm, QK_PAD), lambda i: (0, cur(i), 0))
    head_shape = jax.ShapeDtypeStruct((MLA_HEADS, T, QK_PAD), BF16)
    weights = [win, wtail, qg, wuq, kvg, wukv, cw, cb, dtb, alog, dsk, ng]
    return pl.pallas_call(
        functools.partial(_in_proj_ssd_kernel, tiles_per_seq=per_seq),
        grid=(nt + 1,),
        in_specs=[x_spec, tab, tab] + [_resident(w.shape) for w in weights],
        out_specs=[tok(KV_LORA),
                   pl.BlockSpec((None, tm, QK_ROPE), lambda i: (cur(i) // per_seq, cur(i) % per_seq, 0)),
                   head_tok, head_tok, tok(MLA_WIDTH),
                   pl.BlockSpec((tm, SSD_D_INNER), lambda i: (old(i), 0)),
                   pl.BlockSpec((1, CONV_W - 1, SSD_CONV_DIM), lambda i: (old(i) // per_seq, 0, 0)),
                   pl.BlockSpec((1, SSD_HEADS, SSD_HEAD_DIM, SSD_STATE),
                                lambda i: (old(i) // per_seq, 0, 0, 0))],
        out_shape=[jax.ShapeDtypeStruct((T, KV_LORA), F32),
                   jax.ShapeDtypeStruct((nb, seq_len, QK_ROPE), F32),
                   head_shape, head_shape,
                   jax.ShapeDtypeStruct((T, MLA_WIDTH), BF16),
                   jax.ShapeDtypeStruct((T, SSD_D_INNER), BF16),
                   jax.ShapeDtypeStruct((nb, CONV_W - 1, SSD_CONV_DIM), F32),
                   jax.ShapeDtypeStruct((nb, SSD_HEADS, SSD_HEAD_DIM, SSD_STATE), F32)],
        scratch_shapes=[pltpu.VMEM((2, tm, SSD_D_INNER), F32),
                        pltpu.VMEM((2, tm, SSD_CONV_DIM), F32),
                        pltpu.VMEM((2, tm, LANES), F32),
                        pltpu.VMEM((8, SSD_CONV_DIM), F32),
                        pltpu.VMEM((SSD_HEADS, SSD_HEAD_DIM, SSD_STATE), F32)],
        compiler_params=pltpu.CompilerParams(dimension_semantics=("arbitrary",),
                                             vmem_limit_bytes=VMEM_LIMIT),
        name="in_proj_ssd",
    )(x1, cc, ss, *weights)


def _ssd_short(xbc, z, dt, cprev, h0, cw, cb, dtb, alog, dsk, ng, *, nb, seq_len):
    assert seq_len <= SSD_Q and seq_len % 8 == 0 and nb % SSD_SEQS_PER_STEP == 0
    seqs = SSD_SEQS_PER_STEP
    tokb = lambda w: pl.BlockSpec((seqs * seq_len, w), lambda b: (b, 0))
    conv_spec = pl.BlockSpec((seqs, CONV_W - 1, SSD_CONV_DIM), lambda b: (b, 0, 0))
    state_spec = pl.BlockSpec((seqs, SSD_HEADS, SSD_HEAD_DIM, SSD_STATE), lambda b: (b, 0, 0, 0))
    params = [cw, cb, dtb, alog, dsk, ng]
    return pl.pallas_call(
        functools.partial(_ssd_short_kernel, seq_len=seq_len, seqs=seqs),
        grid=(nb // seqs,),
        in_specs=[tokb(SSD_CONV_DIM), tokb(SSD_D_INNER), tokb(LANES), conv_spec, state_spec]
        + [pl.BlockSpec(p.shape, lambda b: (0, 0)) for p in params],
        out_specs=(tokb(SSD_D_INNER), conv_spec, state_spec),
        out_shape=(jax.ShapeDtypeStruct((nb * seq_len, SSD_D_INNER), BF16),
                   jax.ShapeDtypeStruct((nb, CONV_W - 1, SSD_CONV_DIM), F32),
                   jax.ShapeDtypeStruct((nb, SSD_HEADS, SSD_HEAD_DIM, SSD_STATE), F32)),
        scratch_shapes=[pltpu.VMEM((seqs, 8, SSD_CONV_DIM), F32)],
        compiler_params=pltpu.CompilerParams(dimension_semantics=("parallel",),
                                             vmem_limit_bytes=VMEM_LIMIT),
        name="ssd",
    )(xbc, z, dt, cprev, h0, *params)


def _lane_tile(x, width):
    return jnp.concatenate([x] * (width // LANES), axis=1)


def _attn_prompt_kernel(q_ref, k_ref, v_ref, o_ref, vx_ref, m_ref, acc_ref, *, seq_len, tq, tk, nh):
    ri = lax.broadcasted_iota(jnp.int32, (tk, tk), 0)
    ci = lax.broadcasted_iota(jnp.int32, (tk, tk), 1)
    diag_mask = (ci // CHUNK) <= (ri // CHUNK)

    def head_steps(hh):
        cols = slice(hh * V_HEAD, (hh + 1) * V_HEAD)
        vx_ref[hh, :, :V_HEAD] = v_ref[:, cols]
        vx_ref[hh, :, V_HEAD:] = jnp.ones((seq_len, LANES), BF16)

        def step(r0, k0, row0, masked):
            s = _dot_nt(q_ref[hh, r0 + row0:r0 + tq, :], k_ref[hh, k0:k0 + tk, :])
            if masked:
                top = jnp.where(diag_mask, s[:tk], NEG_BIG)
                s = top if tq - row0 == tk else jnp.concatenate([top, s[tk:]], axis=0)
            m_prev = m_ref[hh, row0:, :]
            m_next = jnp.maximum(m_prev, jnp.max(s, axis=-1, keepdims=True))
            alpha = jnp.exp2(m_prev - m_next)
            p = jnp.exp2(s - _lane_tile(m_next, tk))
            m_ref[hh, row0:, :] = m_next
            acc_ref[hh, row0:, :] = (_lane_tile(alpha, V_HEAD + LANES) * acc_ref[hh, row0:, :]
                                     + _dot(p.astype(BF16), vx_ref[hh, k0:k0 + tk, :]))

        for qb in range(seq_len // tq):
            r0 = qb * tq
            m_ref[hh] = jnp.full(m_ref.shape[1:], NEG_BIG, F32)
            acc_ref[hh] = jnp.zeros(acc_ref.shape[1:], F32)
            for j in range(r0 // tk):
                step(r0, j * tk, 0, False)
                yield
            for d in range(tq // tk):
                step(r0, r0 + d * tk, d * tk, True)
                yield
            o_ref[r0:r0 + tq, cols] = (acc_ref[hh, :, :V_HEAD] / acc_ref[hh, :, V_HEAD:]).astype(BF16)

    active = [head_steps(hh) for hh in range(nh)]
    while active:
        active = [g for g in active if next(g, True) is None]


def _attn_prompt(q, k, v, *, nb, seq_len):
    tq = min(ATT_TQ, seq_len)
    tk = min(ATT_TK, tq)
    nh = ATT_HEADS_PER_STEP
    scratch = [pltpu.VMEM((nh, seq_len, V_HEAD + LANES), BF16), pltpu.VMEM((nh, tq, LANES), F32),
               pltpu.VMEM((nh, tq, V_HEAD + LANES), F32)]
    return pl.pallas_call(
        functools.partial(_attn_prompt_kernel, seq_len=seq_len, tq=tq, tk=tk, nh=nh),
        grid=(nb, MLA_HEADS // nh),
        in_specs=[pl.BlockSpec((nh, seq_len, QK_PAD), lambda b, h: (h, b, 0)),
                  pl.BlockSpec((nh, seq_len, QK_PAD), lambda b, h: (h, b, 0)),
                  pl.BlockSpec((seq_len, nh * V_HEAD), lambda b, h: (b, h))],
        out_specs=pl.BlockSpec((seq_len, nh * V_HEAD), lambda b, h: (b, h)),
        out_shape=jax.ShapeDtypeStruct((nb * seq_len, MLA_WIDTH), BF16),
        scratch_shapes=scratch,
        compiler_params=pltpu.CompilerParams(dimension_semantics=("parallel", "parallel"),
                                             vmem_limit_bytes=VMEM_LIMIT),
        name="attn_prompt",
    )(q, k, v)


def _attn_sample_kernel(q_ref, latc_ref, krc_ref, latn_ref, krn_ref, o_ref, *, past_len, new_len, seqs):
    rows = MLA_HEADS * new_len
    qpos = past_len + lax.broadcasted_iota(jnp.int32, (rows, 1), 0) % new_len
    kpos_c = lax.broadcasted_iota(jnp.int32, (1, past_len), 1)
    kpos_n = past_len + lax.broadcasted_iota(jnp.int32, (1, new_len), 1)
    mask_c = (kpos_c // CHUNK) <= (qpos // CHUNK)
    mask_n = (kpos_n // CHUNK) <= (qpos // CHUNK)

    def one(s):
        tok = slice(s * new_len, (s + 1) * new_len)
        qa = jnp.concatenate([q_ref[h, tok, :] for h in range(MLA_HEADS)], axis=0)
        q_lat = qa[:, :KV_LORA]
        q_rope = qa[:, KV_LORA:KV_LORA + QK_ROPE]
        latc = latc_ref[s].astype(BF16)
        latn = latn_ref[tok, :].astype(BF16)
        sc = _dot_nt(q_lat, latc) + _dot_nt(q_rope, krc_ref[s].astype(BF16))
        sn = _dot_nt(q_lat, latn) + _dot_nt(q_rope, krn_ref[tok, :].astype(BF16))
        yield
        sc = jnp.where(mask_c, sc, NEG_BIG)
        sn = jnp.where(mask_n, sn, NEG_BIG)
        m = jnp.maximum(jnp.max(sc, axis=-1, keepdims=True), jnp.max(sn, axis=-1, keepdims=True))
        pc = jnp.exp2(sc - m)
        pn = jnp.exp2(sn - m)
        l = jnp.sum(pc, axis=-1, keepdims=True) + jnp.sum(pn, axis=-1, keepdims=True)
        yield
        o = (_dot(pc.astype(BF16), latc) + _dot(pn.astype(BF16), latn)) / l
        for h in range(MLA_HEADS):
            o_ref[h, tok, :] = o[h * new_len:(h + 1) * new_len].astype(BF16)

    active = [one(s) for s in range(seqs)]
    while active:
        active = [g for g in active if next(g, True) is None]


def _attn_sample(qa, lat_c, kr_c, lat_n, kr_n, *, nb, past_len, new_len):
    seqs = ATT_SAMPLE_SEQS_PER_STEP
    assert nb % seqs == 0
    return pl.pallas_call(
        functools.partial(_attn_sample_kernel, past_len=past_len, new_len=new_len, seqs=seqs),
        grid=(nb // seqs,),
        in_specs=[pl.BlockSpec((MLA_HEADS, seqs * new_len, KV_LORA + LANES), lambda b: (0, b, 0)),
                  pl.BlockSpec((seqs, past_len, KV_LORA), lambda b: (b, 0, 0)),
                  pl.BlockSpec((seqs, past_len, QK_ROPE), lambda b: (b, 0, 0)),
                  pl.BlockSpec((seqs * new_len, KV_LORA), lambda b: (b, 0)),
                  pl.BlockSpec((seqs * new_len, QK_ROPE), lambda b: (b, 0))],
        out_specs=pl.BlockSpec((MLA_HEADS, seqs * new_len, KV_LORA), lambda b: (0, b, 0)),
        out_shape=jax.ShapeDtypeStruct((MLA_HEADS, nb * new_len, KV_LORA), BF16),
        compiler_params=pltpu.CompilerParams(dimension_semantics=("parallel",),
                                             vmem_limit_bytes=VMEM_LIMIT),
        name="attn_sample",
    )(qa, lat_c, kr_c, lat_n, kr_n)


def _rope_tables(pos0, length, repeats=1):
    inv = ROPE_THETA ** (-np.arange(0, QK_ROPE, 2, dtype=np.float64) / QK_ROPE)
    ang = np.arange(pos0, pos0 + length, dtype=np.float64)[:, None] * inv[None, :]
    cos, sin = np.cos(ang), np.sin(ang)
    pad = np.zeros((length, LANES - QK_ROPE))
    cc = np.tile(np.concatenate([cos, cos, pad], axis=1), (repeats, 1))
    ss = np.tile(np.concatenate([-sin, sin, pad], axis=1), (repeats, 1))
    return jnp.asarray(cc, F32), jnp.asarray(ss, F32)


def _pack_weights(w_in, conv_w, conv_b, dt_bias, a_log, d_skip, ssd_norm_g, q_norm_g, w_uq,
                  kv_norm_g, w_ukv, w_out):
    i0 = SSD_D_INNER
    i1 = i0 + SSD_CONV_DIM
    i2 = i1 + SSD_HEADS
    i3 = i2 + Q_LORA
    i4 = i3 + KV_LORA
    w_dt = jnp.pad(w_in[:, i1:i2], ((0, 0), (0, LANES - SSD_HEADS)))
    w_kr = w_in[:, i4:]
    wtail = jnp.concatenate([w_in[:, i2:i3], w_in[:, i3:i4], w_dt, w_kr,
                             w_kr[:, QK_ROPE // 2:], w_kr[:, :QK_ROPE // 2]], axis=1)
    uq = (w_uq * (SM_SCALE * LOG2E)).reshape(Q_LORA, MLA_HEADS, QK_NOPE + QK_ROPE)
    rope = uq[:, :, QK_NOPE:]
    wuq = jnp.concatenate([uq, rope[:, :, QK_ROPE // 2:], rope[:, :, :QK_ROPE // 2]], axis=2)
    wuq = wuq.reshape(Q_LORA, MLA_HEADS * QK_PAD)
    lane_pad = lambda v: jnp.pad(v, (0, LANES - SSD_HEADS)).reshape(1, LANES)
    return dict(
        wtail=wtail, wuq=wuq,
        qg=q_norm_g.reshape(1, Q_LORA), kvg=kv_norm_g.reshape(1, KV_LORA),
        cw=conv_w, cb=conv_b.reshape(1, SSD_CONV_DIM),
        dtb=lane_pad(dt_bias), alog=lane_pad(a_log),
        dsk=jnp.repeat(d_skip, SSD_HEAD_DIM).reshape(1, SSD_D_INNER),
        ng=ssd_norm_g.reshape(1, SSD_D_INNER),
    )


def kernel(x_prompt, x_sample, cache_latent, cache_k_rope, state_conv, state_ssm, ln1_g, ln1_b, ffn1_w_gate, ffn1_w_up, ffn1_w_down, w_in, conv_w, conv_b, dt_bias, a_log, d_skip, ssd_norm_g, q_norm_g, w_uq, kv_norm_g, w_ukv, w_out, ln2_g, ln2_b, ffn2_w_gate, ffn2_w_up, ffn2_w_down, ln3_g, ln3_b):
    assert w_in.shape[0] == DEPTH == 1
    l = 0
    w = _pack_weights(w_in[l], conv_w[l], conv_b[l], dt_bias[l], a_log[l], d_skip[l],
                      ssd_norm_g[l], q_norm_g[l], w_uq[l], kv_norm_g[l], w_ukv[l], w_out[l])
    f1 = _cast_bf16([ffn1_w_gate, ffn1_w_up, ffn1_w_down])
    row = lambda v: v[l].reshape(1, D_MODEL)
    nbp, lp, _ = x_prompt.shape
    nbs, ls, _ = x_sample.shape
    tm = nbs * ls
    past = cache_latent.shape[2]
    assert tm == TOKEN_TILE and lp % tm == 0 and tm % SSD_Q == 0

    x1, cast = _ffn_ln(x_sample.reshape(tm, D_MODEL), x_prompt.reshape(nbp * lp, D_MODEL), *f1,
                       row(ln1_g), row(ln1_b),
                       casts=[ffn2_w_gate, ffn2_w_up, ffn2_w_down, w_in, w["wtail"], w["wuq"], w_ukv, w_out])
    f2 = cast[:3]
    win, wtail, wuq, wukv, wo = cast[3:]
    proj_w = (win, wtail, w["qg"], wuq, w["kvg"], wukv)
    ssd_w = (w["cw"], w["cb"], w["dtb"], w["alog"], w["dsk"], w["ng"])

    cc, ss = _rope_tables(0, lp)
    lat_p, kr_p, q, kcat, v, y_p, conv_p, ssm_p = _in_proj_ssd(x1, cc, ss, *proj_w, *ssd_w, lead_tiles=1,
                                                               tm=tm, seq_len=lp)
    o_p = _attn_prompt(q, kcat, v, nb=nbp, seq_len=lp)

    cc, ss = _rope_tables(past, ls, repeats=nbs)
    z, xbc, dt, lat_s, kr_s, qa = _in_proj(x1, cc, ss, *proj_w, tokens=tm, tm=tm, seq_len=ls, absorbed=True)
    y_s, conv_s, ssm_s = _ssd_short(xbc, z, dt, state_conv[l], state_ssm[l], *ssd_w, nb=nbs, seq_len=ls)
    o_s = _attn_sample(qa, cache_latent[l], cache_k_rope[l], lat_s, kr_s, nb=nbs, past_len=past, new_len=ls)

    out_p, out_s = _mix_ffn_ln(x1, y_p, o_p, y_s, o_s, wukv, wo, row(ln2_g), row(ln2_b), *f2,
                               row(ln3_g), row(ln3_b))
    st = lambda a: a[None]
    return (out_p.reshape(nbp, lp, D_MODEL), out_s.reshape(nbs, ls, D_MODEL),
            st(lat_p.reshape(nbp, lp, KV_LORA)), st(kr_p), st(conv_p), st(ssm_p),
            st(lat_s.reshape(nbs, ls, KV_LORA)), st(kr_s.reshape(nbs, ls, QK_ROPE)), st(conv_s), st(ssm_s))
```

```python
import functools

import numpy as np
import jax
import jax.numpy as jnp
from jax import lax
from jax.experimental import pallas as pl
from jax.experimental.pallas import tpu as pltpu

F32 = jnp.float32
BF16 = jnp.bfloat16

D_MODEL = 1024
D_FF = 2816
CHUNK = 64
SSD_HEADS = 16
SSD_HEAD_DIM = 64
SSD_D_INNER = SSD_HEADS * SSD_HEAD_DIM
SSD_GROUPS = 2
SSD_HEADS_PER_GROUP = SSD_HEADS // SSD_GROUPS
SSD_STATE = 128
CONV_W = 4
SSD_CONV_DIM = SSD_D_INNER + 2 * SSD_GROUPS * SSD_STATE
MLA_HEADS = 8
Q_LORA = 512
KV_LORA = 512
QK_NOPE = 128
QK_ROPE = 64
V_HEAD = 128
MLA_WIDTH = MLA_HEADS * V_HEAD
ROPE_THETA = 10000.0
DEPTH = 1
ALPHA = (2 * DEPTH) ** 0.25
EPS = 1e-5
SM_SCALE = (QK_NOPE + QK_ROPE) ** -0.5
LOG2E = 1.4426950408889634
NEG_BIG = -1e30

LANES = 128
BF16_SUBLANES = 16
QK_PAD = 2 * LANES
SSD_Q = 128
FF_CHUNK = 256
TOKEN_TILE = 512
LN_ROW_BLOCKS = 8
SIDE_LAG = 2
CAST_STEPS = 4
SSD_SEQS_PER_STEP = 4
ATT_SAMPLE_SEQS_PER_STEP = 4
ATT_TQ = 1024
ATT_TK = 256
ATT_HEADS_PER_STEP = 4
VMEM_LIMIT = 56 * 1024 * 1024

_NT = (((1,), (1,)), ((), ()))


def _resident(shape):
    nd = len(shape)
    return pl.BlockSpec(shape, lambda *_: (0,) * nd, pipeline_mode=pl.Buffered(1))


def _dot(a, b):
    return jnp.dot(a, b, preferred_element_type=F32)


def _dot_nt(a, b):
    return lax.dot_general(a, b, _NT, preferred_element_type=F32)


def _sigmoid(x):
    return 1.0 / (1.0 + jnp.exp2(x * (-LOG2E)))


def _silu(x):
    return x * _sigmoid(x)


def _layer_norm(y, g, b):
    mu = jnp.mean(y, axis=-1, keepdims=True)
    d = y - mu
    var = jnp.mean(d * d, axis=-1, keepdims=True)
    return d * lax.rsqrt(var + EPS) * g + b


def _rms_norm(x, g):
    return x * lax.rsqrt(jnp.mean(x * x, axis=-1, keepdims=True) + EPS) * g


def _zero_after(values):
    t = values[0:8]
    for r in range(8, values.shape[0], 8):
        t = t + values[r:r + 8]
    u = t[:, :LANES]
    for c in range(LANES, values.shape[1], LANES):
        u = u + t[:, c:c + LANES]
    return pltpu.bitcast((pltpu.bitcast(u, jnp.uint32) >> 16) >> 16, F32)


def _ffn_residual(x, wg_ref, wu_ref, wd_ref, side=()):
    side = iter(side)
    rows = x.shape[0]
    xb = x.astype(BF16)
    acc = jnp.zeros((rows, D_MODEL), F32)
    due = {}
    for c in range(D_FF // FF_CHUNK):
        cs = slice(c * FF_CHUNK, (c + 1) * FF_CHUNK)
        h = _silu(_dot(xb, wg_ref[:, cs])) * _dot(xb, wu_ref[:, cs])
        for v in due.pop(c, ()):
            h = jnp.concatenate([h[:8] + jnp.tile(_zero_after(v), (1, FF_CHUNK // LANES)), h[8:]], axis=0)
        acc = acc + _dot(h.astype(BF16), wd_ref[cs, :])
        produced = next(side, None)
        if produced is not None and c + SIDE_LAG < D_FF // FF_CHUNK:
            due[c + SIDE_LAG] = produced
    for _ in side:
        pass
    return ALPHA * x + 0.5 * acc


def _layer_norm_rows(src, g_ref, b_ref, dst_ref):
    nblk = LN_ROW_BLOCKS if dst_ref.shape[0] % (8 * LN_ROW_BLOCKS) == 0 else 1
    rows = dst_ref.shape[0] // nblk
    for r in range(nblk):
        sl = slice(r * rows, (r + 1) * rows)
        dst_ref[sl, :] = out = _layer_norm(src[sl, :], g_ref[...], b_ref[...])
        yield (out,)


def _run(gen):
    for _ in gen:
        pass


def _ffn_ln_kernel(xl_ref, x_ref, wg_ref, wu_ref, wd_ref, g_ref, b_ref, *refs):
    n_cast = (len(refs) - 2) // 2
    cast_in, o_ref, cast_out, pre_ref = refs[:n_cast], refs[n_cast], refs[n_cast + 1:-1], refs[-1]
    i = pl.program_id(0)
    nt = pl.num_programs(0) - 1

    def norm_prev():
        return _layer_norm_rows(pre_ref, g_ref, b_ref, o_ref)

    @pl.when(i == 0)
    def _():
        pre_ref[...] = jnp.zeros(pre_ref.shape, F32)

    @pl.when(i < nt)
    def _():
        x = jnp.where(i == 0, xl_ref[...], x_ref[...])
        pre = _ffn_residual(x, wg_ref, wu_ref, wd_ref, norm_prev())
        pre_ref[...] = pre
        for src, dst in zip(cast_in, cast_out):
            dst[...] = src[...].astype(BF16)

    @pl.when(i == nt)
    def _():
        _run(norm_prev())


def _cast_row_block(rows, steps):
    units = rows // BF16_SUBLANES
    nblk = max(d for d in range(1, min(units, steps) + 1) if units % d == 0)
    return rows // nblk, nblk


def _cast_specs(arrays, steps):
    cast_in, cast_out = [], []
    for a in arrays:
        rows, cols = a.shape[-2:]
        rpb, nblk = _cast_row_block(rows, steps)
        row_blk = lambda i, last=nblk - 1: jnp.minimum(i, last)
        cast_out.append(pl.BlockSpec((rpb, cols), lambda i, f=row_blk: (f(i), 0)))
        cast_in.append(cast_out[-1] if a.ndim == 2 else
                       pl.BlockSpec((None, rpb, cols), lambda i, f=row_blk: (0, f(i), 0)))
    return cast_in, cast_out


def _cast_kernel(*refs):
    n = len(refs) // 2
    for src, dst in zip(refs[:n], refs[n:]):
        dst[...] = src[...].astype(BF16)


def _cast_bf16(arrays):
    cast_in, cast_out = _cast_specs(arrays, CAST_STEPS)
    return pl.pallas_call(
        _cast_kernel,
        grid=(CAST_STEPS,),
        in_specs=cast_in,
        out_specs=cast_out,
        out_shape=[jax.ShapeDtypeStruct(a.shape[-2:], BF16) for a in arrays],
        compiler_params=pltpu.CompilerParams(dimension_semantics=("arbitrary",),
                                             vmem_limit_bytes=VMEM_LIMIT),
        name="cast_bf16",
    )(*arrays)


def _ffn_ln(x_lead, x, wg, wu, wd, g, b, casts=()):
    tm = x_lead.shape[0]
    nt = x.shape[0] // tm
    m = nt + 1
    cast_in, cast_out = _cast_specs(casts, m)
    out = pl.pallas_call(
        _ffn_ln_kernel,
        grid=(m + 1,),
        in_specs=[pl.BlockSpec((tm, D_MODEL), lambda i: (0, 0)),
                  pl.BlockSpec((tm, D_MODEL), lambda i: (jnp.clip(i - 1, 0, nt - 1), 0)),
                  _resident(wg.shape), _resident(wu.shape), _resident(wd.shape),
                  _resident(g.shape), _resident(b.shape)] + cast_in,
        out_specs=[pl.BlockSpec((tm, D_MODEL), lambda i: (jnp.maximum(i - 1, 0), 0))] + cast_out,
        out_shape=[jax.ShapeDtypeStruct((m * tm, D_MODEL), F32)]
        + [jax.ShapeDtypeStruct(a.shape[-2:], BF16) for a in casts],
        scratch_shapes=[pltpu.VMEM((tm, D_MODEL), F32)],
        compiler_params=pltpu.CompilerParams(dimension_semantics=("arbitrary",),
                                             vmem_limit_bytes=VMEM_LIMIT),
        name="ffn_ln",
    )(x_lead, x, wg, wu, wd, g, b, *casts)
    return out[0], out[1:]


def _mix_ffn_ln_kernel(x1_ref, y_ref, o_ref, yl_ref, ol_ref, wukv_ref, wo_ref, g2_ref, b2_ref,
                       wg_ref, wu_ref, wd_ref, g3_ref, b3_ref, out_ref, outl_ref, pre3_ref, olead_ref):
    i = pl.program_id(0)
    m = pl.num_programs(0) - 1

    @pl.when(i == 0)
    def _():
        pre3_ref[...] = jnp.zeros(pre3_ref.shape, F32)
        kv_w = QK_NOPE + V_HEAD
        for h in range(MLA_HEADS):
            olead_ref[:, h * V_HEAD:(h + 1) * V_HEAD] = _dot(
                ol_ref[h], wukv_ref[:, h * kv_w + QK_NOPE:(h + 1) * kv_w]).astype(BF16)

    def norm_prev():
        return _layer_norm_rows(pre3_ref, g3_ref, b3_ref, out_ref)

    @pl.when(i < m)
    def _():
        lead = i == 0
        y = jnp.where(lead, yl_ref[...], y_ref[...])
        o = jnp.where(lead, olead_ref[...], o_ref[...])
        mix = _dot(y, wo_ref[:SSD_D_INNER, :]) + _dot(o, wo_ref[SSD_D_INNER:, :])
        x2 = _layer_norm(ALPHA * x1_ref[...] + mix, g2_ref[...], b2_ref[...])
        pre3 = _ffn_residual(x2, wg_ref, wu_ref, wd_ref, norm_prev())
        pre3_ref[...] = pre3

    @pl.when(i == m)
    def _():
        _run(norm_prev())

    @pl.when(i == 1)
    def _():
        outl_ref[...] = out_ref[...]


def _mix_ffn_ln(x1, y, o, y_lead, o_lead, wukv, wo, g2, b2, wg, wu, wd, g3, b3):
    tm = y_lead.shape[0]
    nt = y.shape[0] // tm
    m = nt + 1
    main = pl.BlockSpec((tm, D_MODEL), lambda i: (jnp.clip(i - 1, 0, nt - 1), 0))
    weights = [wukv, wo, g2, b2, wg, wu, wd, g3, b3]
    return pl.pallas_call(
        _mix_ffn_ln_kernel,
        grid=(m + 1,),
        in_specs=[pl.BlockSpec((tm, D_MODEL), lambda i: (jnp.minimum(i, m - 1), 0)), main, main,
                  pl.BlockSpec((tm, D_MODEL), lambda i: (0, 0)),
                  pl.BlockSpec((MLA_HEADS, tm, KV_LORA), lambda i: (0, 0, 0))]
        + [_resident(w.shape) for w in weights],
        out_specs=[pl.BlockSpec((tm, D_MODEL), lambda i: (jnp.maximum(i - 2, 0), 0)),
                   pl.BlockSpec((tm, D_MODEL), lambda i: (0, 0))],
        out_shape=[jax.ShapeDtypeStruct((nt * tm, D_MODEL), F32),
                   jax.ShapeDtypeStruct((tm, D_MODEL), F32)],
        scratch_shapes=[pltpu.VMEM((tm, D_MODEL), F32), pltpu.VMEM((tm, MLA_WIDTH), BF16)],
        compiler_params=pltpu.CompilerParams(dimension_semantics=("arbitrary",),
                                             vmem_limit_bytes=VMEM_LIMIT),
        name="mix_ffn_ln",
    )(x1, y, o, y_lead, o_lead, *weights)


_Z0, _XBC0 = 0, SSD_D_INNER
_CQ0, _CKV0, _DT0, _KR0, _TAIL_COLS = 0, 512, 1024, 1152, 1280


def _rope_tail(rs, cc, ss):
    return rs * cc + pltpu.roll(rs, 64, 1) * ss


def _in_proj_units(x_ref, cc_ref, ss_ref, win_ref, wtail_ref, qg_ref, wuq_ref, kvg_ref, wukv_ref, *refs,
                   absorbed):
    if absorbed:
        z_ref, xbc_ref, dt_ref, lat_ref, kr_ref, q_ref = refs
    else:
        z_ref, xbc_ref, dt_ref, lat_ref, kr_ref, q_ref, k_ref, v_ref = refs
    wide = 2 * LANES
    kv_w = QK_NOPE + V_HEAD
    xb = x_ref[...].astype(BF16)

    def proj(c0, c1):
        return _dot(xb, win_ref[:, c0:c1])

    def proj_tail(c0, c1):
        return _dot(xb, wtail_ref[:, c0:c1])

    for c in range(0, SSD_D_INNER, wide):
        z_ref[:, c:c + wide] = proj(_Z0 + c, _Z0 + c + wide)
        yield
    dk = proj_tail(_DT0, _TAIL_COLS)
    dt_ref[...] = dk[:, :LANES]
    yield
    cq = _rms_norm(proj_tail(_CQ0, _CKV0), qg_ref[...]).astype(BF16)
    yield
    for c in range(0, SSD_CONV_DIM, wide):
        xbc_ref[:, c:c + wide] = proj(_XBC0 + c, _XBC0 + c + wide)
        yield
    for h in range(MLA_HEADS):
        qh = _dot(cq, wuq_ref[:, h * QK_PAD:(h + 1) * QK_PAD])
        q_nope = qh[:, :LANES].astype(BF16)
        if absorbed:
            q_nope = _dot_nt(q_nope, wukv_ref[:, h * kv_w:h * kv_w + QK_NOPE]).astype(BF16)
        q_ref[h, :, :q_nope.shape[1]] = q_nope
        q_ref[h, :, q_nope.shape[1]:] = _rope_tail(qh[:, LANES:], cc_ref[...], ss_ref[...]).astype(BF16)
        yield
    lat = _rms_norm(proj_tail(_CKV0, _DT0), kvg_ref[...])
    lat_ref[...] = lat
    kr_tail = _rope_tail(dk[:, LANES:], cc_ref[...], ss_ref[...])
    kr_ref[...] = kr_tail[:, :QK_ROPE]
    yield
    if absorbed:
        return
    latb = lat.astype(BF16)
    kr_tail = kr_tail.astype(BF16)
    for h in range(MLA_HEADS):
        kv = _dot(latb, wukv_ref[:, h * kv_w:(h + 1) * kv_w])
        k_ref[h, :, :LANES] = kv[:, :QK_NOPE].astype(BF16)
        k_ref[h, :, LANES:] = kr_tail
        v_ref[:, h * V_HEAD:(h + 1) * V_HEAD] = kv[:, QK_NOPE:].astype(BF16)
        yield


def _in_proj_kernel(*refs, absorbed):
    for _ in _in_proj_units(*refs, absorbed=absorbed):
        pass


def _in_proj(x1, cc, ss, win, wtail, qg, wuq, kvg, wukv, *, tokens, tm, seq_len, absorbed):
    T = tokens
    nt = T // tm
    tok = lambda w: pl.BlockSpec((tm, w), lambda i: (i, 0))
    if cc.shape[0] == T:
        tab = pl.BlockSpec((tm, LANES), lambda i: (i, 0))
    else:
        per_seq = seq_len // tm
        tab = pl.BlockSpec((tm, LANES), lambda i: (i % per_seq, 0))
    q_width = KV_LORA + LANES if absorbed else QK_PAD
    head_tok = lambda w: pl.BlockSpec((MLA_HEADS, tm, w), lambda i: (0, i, 0))
    head_shape = lambda w: jax.ShapeDtypeStruct((MLA_HEADS, T, w), BF16)
    out_shape = [
        jax.ShapeDtypeStruct((T, SSD_D_INNER), F32),
        jax.ShapeDtypeStruct((T, SSD_CONV_DIM), F32),
        jax.ShapeDtypeStruct((T, LANES), F32),
        jax.ShapeDtypeStruct((T, KV_LORA), F32),
        jax.ShapeDtypeStruct((T, QK_ROPE), F32),
        head_shape(q_width),
    ]
    out_specs = [tok(SSD_D_INNER), tok(SSD_CONV_DIM), tok(LANES), tok(KV_LORA), tok(QK_ROPE),
                 head_tok(q_width)]
    weights = [win, wtail, qg, wuq, kvg, wukv]
    if not absorbed:
        out_shape += [head_shape(QK_PAD), jax.ShapeDtypeStruct((T, MLA_WIDTH), BF16)]
        out_specs += [head_tok(QK_PAD), tok(MLA_WIDTH)]
    return pl.pallas_call(
        functools.partial(_in_proj_kernel, absorbed=absorbed),
        grid=(nt,),
        in_specs=[tok(D_MODEL), tab, tab] + [_resident(w.shape) for w in weights],
        out_specs=out_specs,
        out_shape=out_shape,
        compiler_params=pltpu.CompilerParams(dimension_semantics=("parallel",),
                                             vmem_limit_bytes=VMEM_LIMIT),
        name="in_proj",
    )(x1, cc, ss, *weights)


def _split3(x):
    hi = x.astype(BF16)
    r1 = x - hi.astype(F32)
    mid = r1.astype(BF16)
    lo = (r1 - mid.astype(F32)).astype(BF16)
    return hi, mid, lo


def _softplus(x):
    return jnp.maximum(x, 0.0) + jnp.log1p(jnp.exp(-jnp.abs(x)))


def _ssd_block(out, xbc, hist, z, dt_raw, h_in, lb, cw_ref, cb_ref, dtb_ref, alog_ref, dsk_ref, ng_ref):
    Q = SSD_Q
    if lb < Q:
        xbc = jnp.concatenate([xbc, jnp.zeros((Q - lb, SSD_CONV_DIM), F32)], axis=0)
        dt_raw = jnp.concatenate([dt_raw, jnp.zeros((Q - lb, LANES), F32)], axis=0)
    xw = jnp.concatenate([hist, xbc], axis=0)
    conv = xw * cw_ref[0:1, :]
    for k in range(1, CONV_W):
        conv = pltpu.roll(conv, 1, 0) + xw * cw_ref[k:k + 1, :]
    conv = conv[8:, :] + cb_ref[...]
    xc = _silu(conv)
    xs = xc[:, :SSD_D_INNER]
    yield

    row_id = lax.broadcasted_iota(jnp.int32, (Q, LANES), 0)
    dt_col = jnp.where(row_id < lb, _softplus(dt_raw + dtb_ref[...]), 0.0)
    da_col = dt_col * (-jnp.exp(alog_ref[...]))
    ii = lax.broadcasted_iota(jnp.int32, (Q, Q), 0)
    jj = lax.broadcasted_iota(jnp.int32, (Q, Q), 1)
    causal = ii >= jj
    tri = causal.astype(BF16)
    a_col = sum(_dot(tri, p) for p in _split3(da_col)) * LOG2E
    a_row = a_col.T
    dt_row = dt_col.T
    e_col = jnp.exp2(a_col)
    a_last = a_row[:, Q - 1:Q]
    w_row = dt_row * jnp.exp2(a_last - a_row)
    e_last = jnp.exp2(jnp.broadcast_to(a_last, (LANES, LANES)))

    xs_t = xs.T
    yield
    lane = lax.broadcasted_iota(jnp.int32, (Q, LANES), 1)
    lo_half = lane < SSD_HEAD_DIM
    y_parts = []
    h_out = []
    for g in range(SSD_GROUPS):
        b_g = xc[:, SSD_D_INNER + g * SSD_STATE:SSD_D_INNER + (g + 1) * SSD_STATE].astype(BF16)
        c0 = SSD_D_INNER + SSD_GROUPS * SSD_STATE + g * SSD_STATE
        c_g = xc[:, c0:c0 + SSD_STATE].astype(BF16)
        cb = _dot_nt(c_g, b_g)
        h0 = g * SSD_HEADS_PER_GROUP
        hp = h_in[h0:h0 + SSD_HEADS_PER_GROUP].reshape(SSD_HEADS_PER_GROUP * SSD_HEAD_DIM, SSD_STATE)
        y_off = _dot_nt(c_g, hp.astype(BF16))
        lhs = []
        for hh in range(SSD_HEADS_PER_GROUP):
            h = h0 + hh
            lhs.append(xs_t[h * SSD_HEAD_DIM:(h + 1) * SSD_HEAD_DIM, :] * w_row[h:h + 1, :])
        st = _dot(jnp.concatenate(lhs, axis=0).astype(BF16), b_g)
        for hh in range(SSD_HEADS_PER_GROUP):
            h = h0 + hh
            h_out.append(e_last[h:h + 1, :] * h_in[h]
                         + st[hh * SSD_HEAD_DIM:(hh + 1) * SSD_HEAD_DIM, :])
        yield
        for pr in range(SSD_HEADS_PER_GROUP // 2):
            ha = h0 + 2 * pr
            ms = []
            for h in (ha, ha + 1):
                seg = a_col[:, h:h + 1] - a_row[h:h + 1, :]
                m = jnp.exp2(jnp.where(causal, seg, -jnp.inf)) * cb * dt_row[h:h + 1, :]
                ms.append(m.astype(BF16))
            xp = xs[:, ha * SSD_HEAD_DIM:(ha + 2) * SSD_HEAD_DIM]
            rhs = jnp.concatenate([jnp.where(lo_half, xp, 0.0), jnp.where(lo_half, 0.0, xp)],
                                  axis=0).astype(BF16)
            y_d = _dot(jnp.concatenate(ms, axis=1), rhs)
            yo = y_off[:, 2 * pr * SSD_HEAD_DIM:(2 * pr + 2) * SSD_HEAD_DIM]
            dec = jnp.where(lo_half, e_col[:, ha:ha + 1], e_col[:, ha + 1:ha + 2])
            y_parts.append(y_d + yo * dec)
            if pr % 2 == 1:
                yield
    y = jnp.concatenate(y_parts, axis=1) + dsk_ref[...] * xs
    y = y[:lb] * _silu(z)
    half = SSD_D_INNER // SSD_GROUPS
    ng = ng_ref[...]
    y = jnp.concatenate([_rms_norm(y[:, g * half:(g + 1) * half], ng[:, g * half:(g + 1) * half])
                         for g in range(SSD_GROUPS)], axis=1)
    out["y"] = y
    out["h"] = jnp.stack(h_out)


def _ssd_short_kernel(xbc_ref, z_ref, dt_ref, cprev_ref, h0_ref, *refs, seq_len, seqs):
    *params, y_ref, nconv_ref, hnew_ref, tail_ref = refs

    def one(s):
        r0 = s * seq_len
        tail_ref[s] = jnp.zeros(tail_ref.shape[1:], F32)
        tail_ref[s, 8 - (CONV_W - 1):, :] = cprev_ref[s]
        out = {}
        yield from _ssd_block(out, xbc_ref[r0:r0 + seq_len, :], tail_ref[s], z_ref[r0:r0 + seq_len, :],
                              dt_ref[r0:r0 + seq_len, :], h0_ref[s], seq_len, *params)
        y_ref[r0:r0 + seq_len, :] = out["y"].astype(BF16)
        nconv_ref[s] = xbc_ref[r0 + seq_len - (CONV_W - 1):r0 + seq_len, :]
        hnew_ref[s] = out["h"]

    active = [one(s) for s in range(seqs)]
    while active:
        active = [g for g in active if next(g, True) is None]


def _in_proj_ssd_kernel(x_ref, cc_ref, ss_ref, win_w, wtail, qg, wuq, kvg, wukv, cw, cb, dtb, alog, dsk, ng,
                        lat_ref, kr_ref, q_ref, k_ref, v_ref, y_ref, nconv_ref, hnew_ref,
                        zs_ref, xs_ref, dts_ref, tail_ref, h_ref, *, tiles_per_seq):
    i = pl.program_id(0)
    slot = i % 2
    tm = x_ref.shape[0]

    @pl.when(i == 0)
    def _():
        zs_ref[1] = jnp.zeros(zs_ref.shape[1:], F32)
        xs_ref[1] = jnp.zeros(xs_ref.shape[1:], F32)
        dts_ref[1] = jnp.zeros(dts_ref.shape[1:], F32)
        tail_ref[...] = jnp.zeros(tail_ref.shape, F32)
        h_ref[...] = jnp.zeros(h_ref.shape, F32)

    proj_units = _in_proj_units(x_ref, cc_ref, ss_ref, win_w, wtail, qg, wuq, kvg, wukv,
                                zs_ref.at[slot], xs_ref.at[slot], dts_ref.at[slot],
                                lat_ref, kr_ref, q_ref, k_ref, v_ref, absorbed=False)

    prev = 1 - slot
    fresh = (i - 1) % tiles_per_seq == 0
    hist = jnp.where(fresh, 0.0, tail_ref[...])
    h = jnp.where(fresh, 0.0, h_ref[...])
    next(proj_units, None)
    for sb in range(tm // SSD_Q):
        r0 = sb * SSD_Q
        if sb > 0:
            hist = xs_ref[prev, r0 - 8:r0, :]
        out = {}
        for _ in _ssd_block(out, xs_ref[prev, r0:r0 + SSD_Q, :], hist, zs_ref[prev, r0:r0 + SSD_Q, :],
                            dts_ref[prev, r0:r0 + SSD_Q, :], h, SSD_Q, cw, cb, dtb, alog, dsk, ng):
            next(proj_units, None)
        y_ref[r0:r0 + SSD_Q, :] = out["y"].astype(BF16)
        h = out["h"]
    for _ in proj_units:
        pass
    tail_ref[...] = xs_ref[prev, tm - 8:tm, :]
    h_ref[...] = h
    nconv_ref[0] = xs_ref[prev, tm - (CONV_W - 1):tm, :]
    hnew_ref[0] = h


def _in_proj_ssd(x1, cc, ss, win, wtail, qg, wuq, kvg, wukv, cw, cb, dtb, alog, dsk, ng, *,
                 lead_tiles, tm, seq_len):
    T = x1.shape[0] - lead_tiles * tm
    nt = T // tm
    per_seq = seq_len // tm
    nb = T // seq_len
    cur = lambda i: jnp.minimum(i, nt - 1)
    old = lambda i: jnp.maximum(i - 1, 0)
    tok = lambda w: pl.BlockSpec((tm, w), lambda i: (cur(i), 0))
    x_spec = pl.BlockSpec((tm, D_MODEL), lambda i: (cur(i) + lead_tiles, 0))
    tab = pl.BlockSpec((tm, LANES), lambda i: (cur(i) % per_seq, 0))
    head_tok = pl.BlockSpec((MLA_HEADS, tm, QK_PAD), lambda i: (0, cur(i), 0))
    head_shape = jax.ShapeDtypeStruct((MLA_HEADS, T, QK_PAD), BF16)
    weights = [win, wtail, qg, wuq, kvg, wukv, cw, cb, dtb, alog, dsk, ng]
    return pl.pallas_call(
        functools.partial(_in_proj_ssd_kernel, tiles_per_seq=per_seq),
        grid=(nt + 1,),
        in_specs=[x_spec, tab, tab] + [_resident(w.shape) for w in weights],
        out_specs=[tok(KV_LORA),
                   pl.BlockSpec((None, tm, QK_ROPE), lambda i: (cur(i) // per_seq, cur(i) % per_seq, 0)),
                   head_tok, head_tok, tok(MLA_WIDTH),
                   pl.BlockSpec((tm, SSD_D_INNER), lambda i: (old(i), 0)),
                   pl.BlockSpec((1, CONV_W - 1, SSD_CONV_DIM), lambda i: (old(i) // per_seq, 0, 0)),
                   pl.BlockSpec((1, SSD_HEADS, SSD_HEAD_DIM, SSD_STATE),
                                lambda i: (old(i) // per_seq, 0, 0, 0))],
        out_shape=[jax.ShapeDtypeStruct((T, KV_LORA), F32),
                   jax.ShapeDtypeStruct((nb, seq_len, QK_ROPE), F32),
                   head_shape, head_shape,
                   jax.ShapeDtypeStruct((T, MLA_WIDTH), BF16),
                   jax.ShapeDtypeStruct((T, SSD_D_INNER), BF16),
                   jax.ShapeDtypeStruct((nb, CONV_W - 1, SSD_CONV_DIM), F32),
                   jax.ShapeDtypeStruct((nb, SSD_HEADS, SSD_HEAD_DIM, SSD_STATE), F32)],
        scratch_shapes=[pltpu.VMEM((2, tm, SSD_D_INNER), F32),
                        pltpu.VMEM((2, tm, SSD_CONV_DIM), F32),
                        pltpu.VMEM((2, tm, LANES), F32),
                        pltpu.VMEM((8, SSD_CONV_DIM), F32),
                        pltpu.VMEM((SSD_HEADS, SSD_HEAD_DIM, SSD_STATE), F32)],
        compiler_params=pltpu.CompilerParams(dimension_semantics=("arbitrary",),
                                             vmem_limit_bytes=VMEM_LIMIT),
        name="in_proj_ssd",
    )(x1, cc, ss, *weights)


def _ssd_short(xbc, z, dt, cprev, h0, cw, cb, dtb, alog, dsk, ng, *, nb, seq_len):
    assert seq_len <= SSD_Q and seq_len % 8 == 0 and nb % SSD_SEQS_PER_STEP == 0
    seqs = SSD_SEQS_PER_STEP
    tokb = lambda w: pl.BlockSpec((seqs * seq_len, w), lambda b: (b, 0))
    conv_spec = pl.BlockSpec((seqs, CONV_W - 1, SSD_CONV_DIM), lambda b: (b, 0, 0))
    state_spec = pl.BlockSpec((seqs, SSD_HEADS, SSD_HEAD_DIM, SSD_STATE), lambda b: (b, 0, 0, 0))
    params = [cw, cb, dtb, alog, dsk, ng]
    return pl.pallas_call(
        functools.partial(_ssd_short_kernel, seq_len=seq_len, seqs=seqs),
        grid=(nb // seqs,),
        in_specs=[tokb(SSD_CONV_DIM), tokb(SSD_D_INNER), tokb(LANES), conv_spec, state_spec]
        + [pl.BlockSpec(p.shape, lambda b: (0, 0)) for p in params],
        out_specs=(tokb(SSD_D_INNER), conv_spec, state_spec),
        out_shape=(jax.ShapeDtypeStruct((nb * seq_len, SSD_D_INNER), BF16),
                   jax.ShapeDtypeStruct((nb, CONV_W - 1, SSD_CONV_DIM), F32),
                   jax.ShapeDtypeStruct((nb, SSD_HEADS, SSD_HEAD_DIM, SSD_STATE), F32)),
        scratch_shapes=[pltpu.VMEM((seqs, 8, SSD_CONV_DIM), F32)],
        compiler_params=pltpu.CompilerParams(dimension_semantics=("parallel",),
                                             vmem_limit_bytes=VMEM_LIMIT),
        name="ssd",
    )(xbc, z, dt, cprev, h0, *params)


def _lane_tile(x, width):
    return jnp.concatenate([x] * (width // LANES), axis=1)


def _attn_prompt_kernel(q_ref, k_ref, v_ref, o_ref, vx_ref, m_ref, acc_ref, *, seq_len, tq, tk, nh):
    ri = lax.broadcasted_iota(jnp.int32, (tk, tk), 0)
    ci = lax.broadcasted_iota(jnp.int32, (tk, tk), 1)
    diag_mask = (ci // CHUNK) <= (ri // CHUNK)

    def head_steps(hh):
        cols = slice(hh * V_HEAD, (hh + 1) * V_HEAD)
        vx_ref[hh, :, :V_HEAD] = v_ref[:, cols]
        vx_ref[hh, :, V_HEAD:] = jnp.ones((seq_len, LANES), BF16)

        def step(r0, k0, row0, masked):
            s = _dot_nt(q_ref[hh, r0 + row0:r0 + tq, :], k_ref[hh, k0:k0 + tk, :])
            if masked:
                top = jnp.where(diag_mask, s[:tk], NEG_BIG)
                s = top if tq - row0 == tk else jnp.concatenate([top, s[tk:]], axis=0)
            m_prev = m_ref[hh, row0:, :]
            m_next = jnp.maximum(m_prev, jnp.max(s, axis=-1, keepdims=True))
            alpha = jnp.exp2(m_prev - m_next)
            p = jnp.exp2(s - _lane_tile(m_next, tk))
            m_ref[hh, row0:, :] = m_next
            acc_ref[hh, row0:, :] = (_lane_tile(alpha, V_HEAD + LANES) * acc_ref[hh, row0:, :]
                                     + _dot(p.astype(BF16), vx_ref[hh, k0:k0 + tk, :]))

        for qb in range(seq_len // tq):
            r0 = qb * tq
            m_ref[hh] = jnp.full(m_ref.shape[1:], NEG_BIG, F32)
            acc_ref[hh] = jnp.zeros(acc_ref.shape[1:], F32)
            for j in range(r0 // tk):
                step(r0, j * tk, 0, False)
                yield
            for d in range(tq // tk):
                step(r0, r0 + d * tk, d * tk, True)
                yield
            o_ref[r0:r0 + tq, cols] = (acc_ref[hh, :, :V_HEAD] / acc_ref[hh, :, V_HEAD:]).astype(BF16)

    active = [head_steps(hh) for hh in range(nh)]
    while active:
        active = [g for g in active if next(g, True) is None]


def _attn_prompt(q, k, v, *, nb, seq_len):
    tq = min(ATT_TQ, seq_len)
    tk = min(ATT_TK, tq)
    nh = ATT_HEADS_PER_STEP
    scratch = [pltpu.VMEM((nh, seq_len, V_HEAD + LANES), BF16), pltpu.VMEM((nh, tq, LANES), F32),
               pltpu.VMEM((nh, tq, V_HEAD + LANES), F32)]
    return pl.pallas_call(
        functools.partial(_attn_prompt_kernel, seq_len=seq_len, tq=tq, tk=tk, nh=nh),
        grid=(nb, MLA_HEADS // nh),
        in_specs=[pl.BlockSpec((nh, seq_len, QK_PAD), lambda b, h: (h, b, 0)),
                  pl.BlockSpec((nh, seq_len, QK_PAD), lambda b, h: (h, b, 0)),
                  pl.BlockSpec((seq_len, nh * V_HEAD), lambda b, h: (b, h))],
        out_specs=pl.BlockSpec((seq_len, nh * V_HEAD), lambda b, h: (b, h)),
        out_shape=jax.ShapeDtypeStruct((nb * seq_len, MLA_WIDTH), BF16),
        scratch_shapes=scratch,
        compiler_params=pltpu.CompilerParams(dimension_semantics=("parallel", "parallel"),
                                             vmem_limit_bytes=VMEM_LIMIT),
        name="attn_prompt",
    )(q, k, v)


def _attn_sample_kernel(q_ref, latc_ref, krc_ref, latn_ref, krn_ref, o_ref, *, past_len, new_len, seqs):
    rows = MLA_HEADS * new_len
    qpos = past_len + lax.broadcasted_iota(jnp.int32, (rows, 1), 0) % new_len
    kpos_c = lax.broadcasted_iota(jnp.int32, (1, past_len), 1)
    kpos_n = past_len + lax.broadcasted_iota(jnp.int32, (1, new_len), 1)
    mask_c = (kpos_c // CHUNK) <= (qpos // CHUNK)
    mask_n = (kpos_n // CHUNK) <= (qpos // CHUNK)

    def one(s):
        tok = slice(s * new_len, (s + 1) * new_len)
        qa = jnp.concatenate([q_ref[h, tok, :] for h in range(MLA_HEADS)], axis=0)
        q_lat = qa[:, :KV_LORA]
        q_rope = qa[:, KV_LORA:KV_LORA + QK_ROPE]
        latc = latc_ref[s].astype(BF16)
        latn = latn_ref[tok, :].astype(BF16)
        sc = _dot_nt(q_lat, latc) + _dot_nt(q_rope, krc_ref[s].astype(BF16))
        sn = _dot_nt(q_lat, latn) + _dot_nt(q_rope, krn_ref[tok, :].astype(BF16))
        yield
        sc = jnp.where(mask_c, sc, NEG_BIG)
        sn = jnp.where(mask_n, sn, NEG_BIG)
        m = jnp.maximum(jnp.max(sc, axis=-1, keepdims=True), jnp.max(sn, axis=-1, keepdims=True))
        pc = jnp.exp2(sc - m)
        pn = jnp.exp2(sn - m)
        l = jnp.sum(pc, axis=-1, keepdims=True) + jnp.sum(pn, axis=-1, keepdims=True)
        yield
        o = (_dot(pc.astype(BF16), latc) + _dot(pn.astype(BF16), latn)) / l
        for h in range(MLA_HEADS):
            o_ref[h, tok, :] = o[h * new_len:(h + 1) * new_len].astype(BF16)

    active = [one(s) for s in range(seqs)]
    while active:
        active = [g for g in active if next(g, True) is None]


def _attn_sample(qa, lat_c, kr_c, lat_n, kr_n, *, nb, past_len, new_len):
    seqs = ATT_SAMPLE_SEQS_PER_STEP
    assert nb % seqs == 0
    return pl.pallas_call(
        functools.partial(_attn_sample_kernel, past_len=past_len, new_len=new_len, seqs=seqs),
        grid=(nb // seqs,),
        in_specs=[pl.BlockSpec((MLA_HEADS, seqs * new_len, KV_LORA + LANES), lambda b: (0, b, 0)),
                  pl.BlockSpec((seqs, past_len, KV_LORA), lambda b: (b, 0, 0)),
                  pl.BlockSpec((seqs, past_len, QK_ROPE), lambda b: (b, 0, 0)),
                  pl.BlockSpec((seqs * new_len, KV_LORA), lambda b: (b, 0)),
                  pl.BlockSpec((seqs * new_len, QK_ROPE), lambda b: (b, 0))],
        out_specs=pl.BlockSpec((MLA_HEADS, seqs * new_len, KV_LORA), lambda b: (0, b, 0)),
        out_shape=jax.ShapeDtypeStruct((MLA_HEADS, nb * new_len, KV_LORA), BF16),
        compiler_params=pltpu.CompilerParams(dimension_semantics=("parallel",),
                                             vmem_limit_bytes=VMEM_LIMIT),
        name="attn_sample",
    )(qa, lat_c, kr_c, lat_n, kr_n)


def _rope_tables(pos0, length, repeats=1):
    inv = ROPE_THETA ** (-np.arange(0, QK_ROPE, 2, dtype=np.float64) / QK_ROPE)
    ang = np.arange(pos0, pos0 + length, dtype=np.float64)[:, None] * inv[None, :]
    cos, sin = np.cos(ang), np.sin(ang)
    pad = np.zeros((length, LANES - QK_ROPE))
    cc = np.tile(np.concatenate([cos, cos, pad], axis=1), (repeats, 1))
    ss = np.tile(np.concatenate([-sin, sin, pad], axis=1), (repeats, 1))
    return jnp.asarray(cc, F32), jnp.asarray(ss, F32)


def _pack_weights(w_in, conv_w, conv_b, dt_bias, a_log, d_skip, ssd_norm_g, q_norm_g, w_uq,
                  kv_norm_g, w_ukv, w_out):
    i0 = SSD_D_INNER
    i1 = i0 + SSD_CONV_DIM
    i2 = i1 + SSD_HEADS
    i3 = i2 + Q_LORA
    i4 = i3 + KV_LORA
    w_dt = jnp.pad(w_in[:, i1:i2], ((0, 0), (0, LANES - SSD_HEADS)))
    w_kr = w_in[:, i4:]
    wtail = jnp.concatenate([w_in[:, i2:i3], w_in[:, i3:i4], w_dt, w_kr,
                             w_kr[:, QK_ROPE // 2:], w_kr[:, :QK_ROPE // 2]], axis=1)
    uq = (w_uq * (SM_SCALE * LOG2E)).reshape(Q_LORA, MLA_HEADS, QK_NOPE + QK_ROPE)
    rope = uq[:, :, QK_NOPE:]
    wuq = jnp.concatenate([uq, rope[:, :, QK_ROPE // 2:], rope[:, :, :QK_ROPE // 2]], axis=2)
    wuq = wuq.reshape(Q_LORA, MLA_HEADS * QK_PAD)
    lane_pad = lambda v: jnp.pad(v, (0, LANES - SSD_HEADS)).reshape(1, LANES)
    return dict(
        wtail=wtail, wuq=wuq,
        qg=q_norm_g.reshape(1, Q_LORA), kvg=kv_norm_g.reshape(1, KV_LORA),
        cw=conv_w, cb=conv_b.reshape(1, SSD_CONV_DIM),
        dtb=lane_pad(dt_bias), alog=lane_pad(a_log),
        dsk=jnp.repeat(d_skip, SSD_HEAD_DIM).reshape(1, SSD_D_INNER),
        ng=ssd_norm_g.reshape(1, SSD_D_INNER),
    )


def kernel(x_prompt, x_sample, cache_latent, cache_k_rope, state_conv, state_ssm, ln1_g, ln1_b, ffn1_w_gate, ffn1_w_up, ffn1_w_down, w_in, conv_w, conv_b, dt_bias, a_log, d_skip, ssd_norm_g, q_norm_g, w_uq, kv_norm_g, w_ukv, w_out, ln2_g, ln2_b, ffn2_w_gate, ffn2_w_up, ffn2_w_down, ln3_g, ln3_b):
    assert w_in.shape[0] == DEPTH == 1
    l = 0
    w = _pack_weights(w_in[l], conv_w[l], conv_b[l], dt_bias[l], a_log[l], d_skip[l],
                      ssd_norm_g[l], q_norm_g[l], w_uq[l], kv_norm_g[l], w_ukv[l], w_out[l])
    f1 = _cast_bf16([ffn1_w_gate, ffn1_w_up, ffn1_w_down])
    row = lambda v: v[l].reshape(1, D_MODEL)
    nbp, lp, _ = x_prompt.shape
    nbs, ls, _ = x_sample.shape
    tm = nbs * ls
    past = cache_latent.shape[2]
    assert tm == TOKEN_TILE and lp % tm == 0 and tm % SSD_Q == 0

    x1, cast = _ffn_ln(x_sample.reshape(tm, D_MODEL), x_prompt.reshape(nbp * lp, D_MODEL), *f1,
                       row(ln1_g), row(ln1_b),
                       casts=[ffn2_w_gate, ffn2_w_up, ffn2_w_down, w_in, w["wtail"], w["wuq"], w_ukv, w_out])
    f2 = cast[:3]
    win, wtail, wuq, wukv, wo = cast[3:]
    proj_w = (win, wtail, w["qg"], wuq, w["kvg"], wukv)
    ssd_w = (w["cw"], w["cb"], w["dtb"], w["alog"], w["dsk"], w["ng"])

    cc, ss = _rope_tables(0, lp)
    lat_p, kr_p, q, kcat, v, y_p, conv_p, ssm_p = _in_proj_ssd(x1, cc, ss, *proj_w, *ssd_w, lead_tiles=1,
                                                               tm=tm, seq_len=lp)
    o_p = _attn_prompt(q, kcat, v, nb=nbp, seq_len=lp)

    cc, ss = _rope_tables(past, ls, repeats=nbs)
    z, xbc, dt, lat_s, kr_s, qa = _in_proj(x1, cc, ss, *proj_w, tokens=tm, tm=tm, seq_len=ls, absorbed=True)
    y_s, conv_s, ssm_s = _ssd_short(xbc, z, dt, state_conv[l], state_ssm[l], *ssd_w, nb=nbs, seq_len=ls)
    o_s = _attn_sample(qa, cache_latent[l], cache_k_rope[l], lat_s, kr_s, nb=nbs, past_len=past, new_len=ls)

    out_p, out_s = _mix_ffn_ln(x1, y_p, o_p, y_s, o_s, wukv, wo, row(ln2_g), row(ln2_b), *f2,
                               row(ln3_g), row(ln3_b))
    st = lambda a: a[None]
    return (out_p.reshape(nbp, lp, D_MODEL), out_s.reshape(nbs, ls, D_MODEL),
            st(lat_p.reshape(nbp, lp, KV_LORA)), st(kr_p), st(conv_p), st(ssm_p),
            st(lat_s.reshape(nbs, ls, KV_LORA)), st(kr_s.reshape(nbs, ls, QK_ROPE)), st(conv_s), st(ssm_s))
```
